```python
import math
import jax, jax.numpy as jnp
from jax import lax
import numpy as np

D_MODEL = 1024
BATCH = 8
SEQ = 4096
DEPTH = 4

N_MIXERS = 4
QB = 128
HEAD_DIM = 64
NORM_EPS = 1e-6
A_PATTERN = ((128, 1), (512, 4), (2048, 16))
A_GROUPS = len(A_PATTERN)
A_HEADS_PER_GROUP = 8
DIFF_HEADS = 8
DIFF_QK_DIM = 64
SUBLN_EPS = 1e-5
MLA_HEADS = 16
MLA_Q_RANK = 384
MLA_KV_RANK = 256
MLA_NOPE_DIM = 64
MLA_ROPE_DIM = 32
MLA_V_DIM = 64
ROPE_THETA = 10000.0
SB_HEADS = 16
D_FF_DENSE = 128 * (-(-(-(-8 * D_MODEL // 3)) // 128))
N_EXPERTS = 8
TOP_K = 2
D_FF_EXPERT = 7 * D_MODEL // 2
MOE_BLOCK = 256

kernel_name = 'hybrid_interleaved_mixer_trunk'


def rmsnorm(x, g, eps=NORM_EPS):
    xf = x.astype(jnp.float32)
    y = xf * lax.rsqrt(jnp.mean(xf * xf, axis=-1, keepdims=True) + eps)
    return (y * g.astype(jnp.float32)).astype(x.dtype)


def alibi_slopes(n):
    return jnp.asarray(2.0 ** (-8.0 * np.arange(1, n + 1) / n), dtype=jnp.float32)


def rope(x, pos):
    half = x.shape[-1] // 2
    inv_freq = ROPE_THETA ** (-jnp.arange(half, dtype=jnp.float32) / half)
    ang = pos.astype(jnp.float32)[:, None] * inv_freq[None, :]
    cos = jnp.cos(ang)[None, :, None, :]
    sin = jnp.sin(ang)[None, :, None, :]
    xf = x.astype(jnp.float32)
    x1, x2 = xf[..., :half], xf[..., half:]
    return jnp.concatenate([x1 * cos - x2 * sin, x1 * sin + x2 * cos], axis=-1).astype(x.dtype)


def sweep_query_blocks(block_fn, batch, seq):
    out = lax.map(block_fn, jnp.arange(seq // QB))
    out = jnp.moveaxis(out, 0, 1)
    return out.reshape((batch, seq) + out.shape[3:])


def causal_rel(j, seq):
    t = j * QB + jnp.arange(QB)
    return t[:, None] - jnp.arange(seq)[None, :]


def dilated_window_attention(q, k, v, window, dilation, slopes):
    B, S, H, dh = q.shape
    taps = window // dilation
    L = S // dilation
    Lp = -(-L // QB) * QB
    nb = Lp // QB

    def by_residue(t):
        t = t.reshape(B, L, dilation, H, dh).transpose(0, 2, 3, 1, 4)
        t = jnp.pad(t, ((0, 0), (0, 0), (0, 0), (0, Lp - L), (0, 0)))
        return t.reshape(B, dilation, H, nb, QB, dh)

    def with_prev(t):
        prev = jnp.pad(t, ((0, 0), (0, 0), (0, 0), (1, 0), (0, 0), (0, 0)))[:, :, :, :-1]
        return jnp.concatenate([prev, t], axis=-2)

    qr = by_residue(q)
    kk = with_prev(by_residue(k))
    vv = with_prev(by_residue(v))
    n_q = jnp.arange(nb)[:, None, None] * QB + jnp.arange(QB)[None, :, None]
    n_k = jnp.arange(nb)[:, None, None] * QB - QB + jnp.arange(2 * QB)[None, None, :]
    rel = n_q - n_k
    valid = (rel >= 0) & (rel <= taps) & (n_k >= 0)
    s = jnp.einsum('bdhnqe,bdhnke->bdhnqk', qr, kk).astype(jnp.float32) * (dh ** -0.5)
    s = s - slopes[None, None, :, None, None, None] * (dilation * rel).astype(jnp.float32)
    s = jnp.where(valid, s, -jnp.inf)
    lse = jax.nn.logsumexp(s, axis=-1)
    p = jnp.exp(s - lse[..., None])
    o = jnp.einsum('bdhnqk,bdhnke->bdhnqe', p.astype(v.dtype), vv)
    o = o.reshape(B, dilation, H, Lp, dh)[:, :, :, :L].transpose(0, 3, 1, 2, 4).reshape(B, S, H, dh)
    lse = lse.reshape(B, dilation, H, Lp)[..., :L].transpose(0, 3, 1, 2).reshape(B, S, H)
    return o, lse


def dilated_mixture_attention(h, w_qkv, w_o):
    B, S, _ = h.shape
    qkv = (h @ w_qkv).reshape(B, S, 3, A_GROUPS, A_HEADS_PER_GROUP, HEAD_DIM)
    slopes = alibi_slopes(A_GROUPS * A_HEADS_PER_GROUP).reshape(A_GROUPS, A_HEADS_PER_GROUP)
    outs, lses = [], []
    for g, (window, dilation) in enumerate(A_PATTERN):
        o, lse = dilated_window_attention(qkv[:, :, 0, g], qkv[:, :, 1, g], qkv[:, :, 2, g],
                                          window, dilation, slopes[g])
        outs.append(o)
        lses.append(lse)
    wgt = jax.nn.softmax(jnp.stack(lses), axis=0)
    mixed = jnp.sum(wgt[..., None].astype(h.dtype) * jnp.stack(outs), axis=0)
    return mixed.reshape(B, S, A_HEADS_PER_GROUP * HEAD_DIM) @ w_o


def differential_attention(h, w_qkv, lq1, lk1, lq2, lk2, subln, w_o, layer):
    B, S, _ = h.shape
    f32 = jnp.float32
    qkv = (h @ w_qkv).reshape(B, S, 3, DIFF_HEADS, 2, DIFF_QK_DIM)
    q, k = qkv[:, :, 0], qkv[:, :, 1]
    v = qkv[:, :, 2].reshape(B, S, DIFF_HEADS, 2 * DIFF_QK_DIM)
    lam_init = 0.8 - 0.6 * math.exp(-0.3 * layer)
    lam = (jnp.exp(jnp.sum(lq1.astype(f32) * lk1.astype(f32)))
           - jnp.exp(jnp.sum(lq2.astype(f32) * lk2.astype(f32))) + lam_init)
    slopes = alibi_slopes(DIFF_HEADS)
    scale = DIFF_QK_DIM ** -0.5

    def block(j):
        qb = lax.dynamic_slice_in_dim(q, j * QB, QB, axis=1)
        s = jnp.einsum('bqhme,bkhme->bhmqk', qb, k).astype(f32) * scale
        rel = causal_rel(j, S)
        s = s - slopes[None, :, None, None, None] * rel.astype(f32)
        s = jnp.where(rel >= 0, s, -jnp.inf)
        p = jax.nn.softmax(s, axis=-1)
        a = p[:, :, 0] - lam * p[:, :, 1]
        return jnp.einsum('bhqk,bkhe->bqhe', a.astype(v.dtype), v)

    o = sweep_query_blocks(block, B, S)
    o = rmsnorm(o, subln, eps=SUBLN_EPS) * (1.0 - lam_init)
    return o.reshape(B, S, DIFF_HEADS * 2 * DIFF_QK_DIM) @ w_o


def latent_attention(h, w_in, q_norm, w_q_up, kv_norm, w_kv_up, w_o):
    B, S, _ = h.shape
    f32 = jnp.float32
    lat = h @ w_in
    q_lat, kv_lat, k_pe = jnp.split(lat, [MLA_Q_RANK, MLA_Q_RANK + MLA_KV_RANK], axis=-1)
    q = (rmsnorm(q_lat, q_norm) @ w_q_up).reshape(B, S, MLA_HEADS, MLA_NOPE_DIM + MLA_ROPE_DIM)
    kv = (rmsnorm(kv_lat, kv_norm) @ w_kv_up).reshape(B, S, MLA_HEADS, MLA_NOPE_DIM + MLA_V_DIM)
    pos = jnp.arange(S)
    q_nope, q_pe = q[..., :MLA_NOPE_DIM], rope(q[..., MLA_NOPE_DIM:], pos)
    k_nope, v = kv[..., :MLA_NOPE_DIM], kv[..., MLA_NOPE_DIM:]
    k_pe = rope(k_pe[:, :, None, :], pos)[:, :, 0]
    scale = (MLA_NOPE_DIM + MLA_ROPE_DIM) ** -0.5

    def block(j):
        qn = lax.dynamic_slice_in_dim(q_nope, j * QB, QB, axis=1)
        qr = lax.dynamic_slice_in_dim(q_pe, j * QB, QB, axis=1)
        s = (jnp.einsum('bqhe,bkhe->bhqk', qn, k_nope)
             + jnp.einsum('bqhr,bkr->bhqk', qr, k_pe)).astype(f32) * scale
        s = jnp.where(causal_rel(j, S) >= 0, s, -jnp.inf)
        p = jax.nn.softmax(s, axis=-1)
        return jnp.einsum('bhqk,bkhe->bqhe', p.astype(v.dtype), v)

    o = sweep_query_blocks(block, B, S)
    return o.reshape(B, S, MLA_HEADS * MLA_V_DIM) @ w_o


def stick_breaking_attention(h, w_qkv, w_o):
    B, S, _ = h.shape
    f32 = jnp.float32
    qkv = (h @ w_qkv).reshape(B, S, 3, SB_HEADS, HEAD_DIM)
    q, k, v = qkv[:, :, 0], qkv[:, :, 1], qkv[:, :, 2]
    scale = HEAD_DIM ** -0.5

    def block(j):
        qb = lax.dynamic_slice_in_dim(q, j * QB, QB, axis=1)
        z = jnp.einsum('bqhe,bkhe->bhqk', qb, k).astype(f32) * scale
        strict = causal_rel(j, S) > 0
        log_keep = jnp.where(strict, jax.nn.log_sigmoid(-z), 0.0)
        tail = lax.cumsum(log_keep, axis=3, reverse=True) - log_keep
        a = jnp.where(strict, jnp.exp(jax.nn.log_sigmoid(z) + tail), 0.0)
        return jnp.einsum('bhqk,bkhe->bqhe', a.astype(v.dtype), v)

    o = sweep_query_blocks(block, B, S)
    return o.reshape(B, S, SB_HEADS * HEAD_DIM) @ w_o


def swiglu(h, w_gate, w_up, w_down):
    return (jax.nn.silu(h @ w_gate) * (h @ w_up)) @ w_down


def moe_swiglu(h, w_router, w_gate, w_up, w_down):
    B, S, D = h.shape
    x2 = h.reshape(-1, D)
    N = x2.shape[0]
    NK = N * TOP_K
    logits = (x2 @ w_router).astype(jnp.float32)
    top_logit, top_e = lax.top_k(logits, TOP_K)
    gates = jax.nn.softmax(top_logit, axis=-1)
    flat_e = top_e.reshape(-1)
    flat_g = gates.reshape(-1)
    flat_tok = jnp.arange(NK, dtype=jnp.int32) // TOP_K
    order = jnp.argsort(flat_e)
    se, stok, sg = flat_e[order], flat_tok[order], flat_g[order]
    counts = jnp.bincount(flat_e, length=N_EXPERTS)
    padded = (counts + MOE_BLOCK - 1) // MOE_BLOCK * MOE_BLOCK
    pad_end = jnp.cumsum(padded)
    pad_start = pad_end - padded
    start = jnp.cumsum(counts) - counts
    dest = pad_start[se] + jnp.arange(NK, dtype=jnp.int32) - start[se]
    n_blocks = -(-NK // MOE_BLOCK) + N_EXPERTS
    slot_tok = jnp.full((n_blocks * MOE_BLOCK,), N, jnp.int32).at[dest].set(stok)
    slot_gate = jnp.zeros((n_blocks * MOE_BLOCK,), jnp.float32).at[dest].set(sg)
    block_e = jnp.minimum(jnp.searchsorted(pad_end, jnp.arange(n_blocks) * MOE_BLOCK, side='right'),
                          N_EXPERTS - 1)
    x_pad = jnp.concatenate([x2, jnp.zeros((1, D), x2.dtype)], axis=0)
    xb = x_pad[slot_tok].reshape(n_blocks, MOE_BLOCK, D)

    def expert_block(args):
        xe, e = args
        return (jax.nn.silu(xe @ w_gate[e]) * (xe @ w_up[e])) @ w_down[e]

    yb = lax.map(expert_block, (xb, block_e)).reshape(-1, D)
    yb = yb * slot_gate[:, None].astype(yb.dtype)
    y = jnp.zeros((N + 1, D), yb.dtype).at[slot_tok].add(yb)
    return y[:N].reshape(B, S, D)


def setup_inputs(seed: int = 0) -> dict:
    key = jax.random.key(seed)
    ks = iter(jax.random.split(key, 64))
    f32 = jnp.float32

    def w(shape, fan_in):
        return jax.random.normal(next(ks), shape, f32) * (fan_in ** -0.5)

    def gain(n):
        return 1.0 + 0.01 * jax.random.normal(next(ks), (n,), f32)

    def small(n):
        return 0.1 * jax.random.normal(next(ks), (n,), f32)

    D = D_MODEL
    a_width = A_GROUPS * A_HEADS_PER_GROUP * HEAD_DIM
    d_width = DIFF_HEADS * 2 * DIFF_QK_DIM
    return {
        'x': jax.random.normal(next(ks), (BATCH, SEQ, D), f32),
        'a_norm': gain(D),
        'a_w_qkv': w((D, 3 * a_width), D),
        'a_w_o': w((A_HEADS_PER_GROUP * HEAD_DIM, D), A_HEADS_PER_GROUP * HEAD_DIM),
        'ffn0_norm': gain(D),
        'ffn0_w_gate': w((D, D_FF_DENSE), D),
        'ffn0_w_up': w((D, D_FF_DENSE), D),
        'ffn0_w_down': w((D_FF_DENSE, D), D_FF_DENSE),
        'b_norm': gain(D),
        'b_w_qkv': w((D, 3 * d_width), D),
        'b_lambda_q1': small(DIFF_QK_DIM),
        'b_lambda_k1': small(DIFF_QK_DIM),
        'b_lambda_q2': small(DIFF_QK_DIM),
        'b_lambda_k2': small(DIFF_QK_DIM),
        'b_subln': gain(2 * DIFF_QK_DIM),
        'b_w_o': w((d_width, D), d_width),
        'moe1_norm': gain(D),
        'moe1_w_router': w((D, N_EXPERTS), D),
        'moe1_w_gate': w((N_EXPERTS, D, D_FF_EXPERT), D),
        'moe1_w_up': w((N_EXPERTS, D, D_FF_EXPERT), D),
        'moe1_w_down': w((N_EXPERTS, D_FF_EXPERT, D), D_FF_EXPERT),
        'c_norm': gain(D),
        'c_w_in': w((D, MLA_Q_RANK + MLA_KV_RANK + MLA_ROPE_DIM), D),
        'c_q_norm': gain(MLA_Q_RANK),
        'c_w_q_up': w((MLA_Q_RANK, MLA_HEADS * (MLA_NOPE_DIM + MLA_ROPE_DIM)), MLA_Q_RANK),
        'c_kv_norm': gain(MLA_KV_RANK),
        'c_w_kv_up': w((MLA_KV_RANK, MLA_HEADS * (MLA_NOPE_DIM + MLA_V_DIM)), MLA_KV_RANK),
        'c_w_o': w((MLA_HEADS * MLA_V_DIM, D), MLA_HEADS * MLA_V_DIM),
        'ffn2_norm': gain(D),
        'ffn2_w_gate': w((D, D_FF_DENSE), D),
        'ffn2_w_up': w((D, D_FF_DENSE), D),
        'ffn2_w_down': w((D_FF_DENSE, D), D_FF_DENSE),
        'd_norm': gain(D),
        'd_w_qkv': w((D, 3 * SB_HEADS * HEAD_DIM), D),
        'd_w_o': w((SB_HEADS * HEAD_DIM, D), SB_HEADS * HEAD_DIM),
        'moe3_norm': gain(D),
        'moe3_w_router': w((D, N_EXPERTS), D),
        'moe3_w_gate': w((N_EXPERTS, D, D_FF_EXPERT), D),
        'moe3_w_up': w((N_EXPERTS, D, D_FF_EXPERT), D),
        'moe3_w_down': w((N_EXPERTS, D_FF_EXPERT, D), D_FF_EXPERT),
        'final_norm': gain(D),
    }


def reference(x, a_norm, a_w_qkv, a_w_o,
              ffn0_norm, ffn0_w_gate, ffn0_w_up, ffn0_w_down,
              b_norm, b_w_qkv, b_lambda_q1, b_lambda_k1, b_lambda_q2, b_lambda_k2, b_subln, b_w_o,
              moe1_norm, moe1_w_router, moe1_w_gate, moe1_w_up, moe1_w_down,
              c_norm, c_w_in, c_q_norm, c_w_q_up, c_kv_norm, c_w_kv_up, c_w_o,
              ffn2_norm, ffn2_w_gate, ffn2_w_up, ffn2_w_down,
              d_norm, d_w_qkv, d_w_o,
              moe3_norm, moe3_w_router, moe3_w_gate, moe3_w_up, moe3_w_down,
              final_norm):
    token_mixers = (
        lambda h, layer: dilated_mixture_attention(h, a_w_qkv, a_w_o),
        lambda h, layer: differential_attention(h, b_w_qkv, b_lambda_q1, b_lambda_k1, b_lambda_q2,
                                                b_lambda_k2, b_subln, b_w_o, layer),
        lambda h, layer: latent_attention(h, c_w_in, c_q_norm, c_w_q_up, c_kv_norm, c_w_kv_up, c_w_o),
        lambda h, layer: stick_breaking_attention(h, d_w_qkv, d_w_o),
    )
    mixer_norms = (a_norm, b_norm, c_norm, d_norm)
    channel_mixers = (
        lambda h: swiglu(h, ffn0_w_gate, ffn0_w_up, ffn0_w_down),
        lambda h: moe_swiglu(h, moe1_w_router, moe1_w_gate, moe1_w_up, moe1_w_down),
        lambda h: swiglu(h, ffn2_w_gate, ffn2_w_up, ffn2_w_down),
        lambda h: moe_swiglu(h, moe3_w_router, moe3_w_gate, moe3_w_up, moe3_w_down),
    )
    channel_norms = (ffn0_norm, moe1_norm, ffn2_norm, moe3_norm)
    for i in range(DEPTH):
        m = i % N_MIXERS
        x = x + token_mixers[m](rmsnorm(x, mixer_norms[m]), i)
        x = x + channel_mixers[i](rmsnorm(x, channel_norms[i]))
    return rmsnorm(x, final_norm)
```

```python
import functools
import math

import numpy as np
import jax
import jax.numpy as jnp
from jax import lax
from jax.experimental import pallas as pl
from jax.experimental.pallas import tpu as pltpu

F32 = jnp.float32
BF16 = jnp.bfloat16
I32 = jnp.int32

LANES = 128
HEAD_DIM = 64
NORM_EPS = 1e-6
SUBLN_EPS = 1e-5
ROPE_THETA = 10000.0
A_PATTERN = ((128, 1), (512, 4), (2048, 16))
A_HEADS = 8
QB = 128
N_EXPERTS = 8
MOE_ROWS = 512
VMEM_LIMIT = 56 * 1024 * 1024
NEG_INF = float("-inf")


def _cparams(n_axes, vmem=VMEM_LIMIT):
    return pltpu.CompilerParams(dimension_semantics=("arbitrary",) * n_axes,
                                vmem_limit_bytes=vmem)


def _dot(a, b):
    return jnp.dot(a, b, preferred_element_type=F32)


def _dot_nt(a, b):
    return lax.dot_general(a, b, (((1,), (1,)), ((), ())), preferred_element_type=F32)


def _rms(x, g, eps):
    return x * lax.rsqrt(jnp.mean(x * x, axis=-1, keepdims=True) + eps) * g


def _lane_tile(x, n):
    return x if n == 1 else jnp.concatenate([x] * n, axis=1)


def _norm_proj_kernel(x_ref, g_ref, w_ref, o_ref, *, col_chunk):
    xn = _rms(x_ref[...], g_ref[...], NORM_EPS).astype(BF16)
    n = o_ref.shape[1]
    for c in range(0, n, col_chunk):
        o_ref[:, c:c + col_chunk] = _dot(xn, w_ref[:, c:c + col_chunk]).astype(o_ref.dtype)


def norm_proj(x2, g, w, *, tm=512, col_chunk=512):
    n, d = x2.shape
    n_out = w.shape[1]
    assert n % tm == 0 and n_out % col_chunk == 0
    return pl.pallas_call(
        functools.partial(_norm_proj_kernel, col_chunk=col_chunk),
        grid=(n // tm,),
        in_specs=[pl.BlockSpec((tm, d), lambda i: (i, 0)),
                  pl.BlockSpec((1, d), lambda i: (0, 0)),
                  pl.BlockSpec((d, n_out), lambda i: (0, 0))],
        out_specs=pl.BlockSpec((tm, n_out), lambda i: (i, 0)),
        out_shape=jax.ShapeDtypeStruct((n, n_out), BF16),
        compiler_params=_cparams(1),
        name="norm_proj",
    )(x2, g.reshape(1, d), w)


def _oproj_kernel(x_ref, o_ref, w_ref, out_ref):
    out_ref[...] = x_ref[...] + _dot(o_ref[...], w_ref[...])


def oproj_res(x2, o2, w, *, tm=1024):
    n, d = x2.shape
    k = o2.shape[1]
    return pl.pallas_call(
        _oproj_kernel,
        grid=(n // tm,),
        in_specs=[pl.BlockSpec((tm, d), lambda i: (i, 0)),
                  pl.BlockSpec((tm, k), lambda i: (i, 0)),
                  pl.BlockSpec((k, d), lambda i: (0, 0))],
        out_specs=pl.BlockSpec((tm, d), lambda i: (i, 0)),
        out_shape=jax.ShapeDtypeStruct((n, d), F32),
        compiler_params=_cparams(1),
        name="oproj_res",
    )(x2, o2, w)


def _attn_a_kernel(q_ref, kc_ref, kp_ref, vc_ref, vp_ref, o_ref, lse_ref, *, slopes, scale):
    has_prev = pl.program_id(2) > 0
    row = lax.broadcasted_iota(I32, (QB, QB), 0)
    col = lax.broadcasted_iota(I32, (QB, QB), 1)
    rel_c = row - col
    rel_p = rel_c + QB
    valid_c = rel_c >= 0
    valid_p = jnp.logical_and(rel_p <= QB, has_prev)
    rel_cf = rel_c.astype(F32)
    rel_pf = rel_p.astype(F32)
    lo = lax.broadcasted_iota(I32, (QB, LANES), 1) < HEAD_DIM
    for p in range(A_HEADS // 2):
        sl = slice(p * LANES, (p + 1) * LANES)
        q = q_ref[0, :, sl]
        kc, kp, vc, vp = kc_ref[0, :, sl], kp_ref[0, :, sl], vc_ref[0, :, sl], vp_ref[0, :, sl]
        outs, lses = [], []
        for jh in range(2):
            slope = slopes[2 * p + jh]
            qm = jnp.where(lo if jh == 0 else jnp.logical_not(lo), q, jnp.zeros_like(q))
            sc = jnp.where(valid_c, _dot_nt(qm, kc) * scale - slope * rel_cf, NEG_INF)
            sp = jnp.where(valid_p, _dot_nt(qm, kp) * scale - slope * rel_pf, NEG_INF)
            m = jnp.maximum(jnp.max(sc, axis=1, keepdims=True), jnp.max(sp, axis=1, keepdims=True))
            pc = jnp.exp(sc - m)
            pp = jnp.exp(sp - m)
            l = jnp.sum(pc, axis=1, keepdims=True) + jnp.sum(pp, axis=1, keepdims=True)
            acc = _dot(pc.astype(BF16), vc) + _dot(pp.astype(BF16), vp)
            outs.append(acc / l)
            lses.append(jnp.broadcast_to(m + jnp.log(l), (QB, LANES)))
        o_ref[0, :, sl] = jnp.where(lo, outs[0], outs[1]).astype(o_ref.dtype)
        lse_ref[0, :, sl] = jnp.where(lo, lses[0], lses[1])


def attn_a_group(qkv3, g, dil, slopes):
    b, s, width = qkv3.shape
    gw = A_HEADS * HEAD_DIM
    nblk = width // gw
    l = s // dil
    assert l % QB == 0
    nb = l // QB
    view = qkv3.reshape(b, l, dil * width)

    def spec(which, prev):
        def imap(bi, r, j):
            return (bi, jnp.maximum(j - 1, 0) if prev else j, r * nblk + which * 3 + g)
        return pl.BlockSpec((1, QB, gw), imap)

    o, lse = pl.pallas_call(
        functools.partial(_attn_a_kernel, slopes=tuple(float(x) * dil for x in slopes),
                          scale=HEAD_DIM ** -0.5),
        grid=(b, dil, nb),
        in_specs=[spec(0, False), spec(1, False), spec(1, True), spec(2, False), spec(2, True)],
        out_specs=[pl.BlockSpec((1, QB, gw), lambda bi, r, j: (bi, j, r)),
                   pl.BlockSpec((1, QB, gw), lambda bi, r, j: (bi, j, r))],
        out_shape=[jax.ShapeDtypeStruct((b, l, dil * gw), BF16),
                   jax.ShapeDtypeStruct((b, l, dil * gw), F32)],
        compiler_params=_cparams(3),
        name=f"attn_a_d{dil}",
    )(view, view, view, view, view)
    return o.reshape(b * s, gw), lse.reshape(b * s, gw)


def _mix_oproj_kernel(x_ref, o0, o1, o2, l0, l1, l2, w_ref, out_ref):
    ls = [l0[...], l1[...], l2[...]]
    m = jnp.maximum(jnp.maximum(ls[0], ls[1]), ls[2])
    es = [jnp.exp(v - m) for v in ls]
    den = es[0] + es[1] + es[2]
    mixed = ((es[0] / den) * o0[...].astype(F32) + (es[1] / den) * o1[...].astype(F32)
             + (es[2] / den) * o2[...].astype(F32))
    out_ref[...] = x_ref[...] + _dot(mixed.astype(BF16), w_ref[...])


def mix_oproj_res(x2, outs, lses, w, *, tm=1024):
    n, d = x2.shape
    gw = w.shape[0]
    blk = pl.BlockSpec((tm, gw), lambda i: (i, 0))
    return pl.pallas_call(
        _mix_oproj_kernel,
        grid=(n // tm,),
        in_specs=[pl.BlockSpec((tm, d), lambda i: (i, 0))] + [blk] * 6
                 + [pl.BlockSpec((gw, d), lambda i: (0, 0))],
        out_specs=pl.BlockSpec((tm, d), lambda i: (i, 0)),
        out_shape=jax.ShapeDtypeStruct((n, d), F32),
        compiler_params=_cparams(1),
        name="mix_oproj_res",
    )(x2, *outs, *lses, w)


def _softmax_update(s, v, m_ref, l_ref, acc_ref, idx):
    tk = s.shape[1]
    m_prev = m_ref[idx]
    m_next = jnp.maximum(m_prev, jnp.max(s, axis=1, keepdims=True))
    p = jnp.exp(s - _lane_tile(m_next, tk // LANES))
    alpha = jnp.exp(m_prev - m_next)
    l_ref[idx] = alpha * l_ref[idx] + jnp.sum(p, axis=1, keepdims=True)
    acc_ref[idx] = alpha * acc_ref[idx] + _dot(p.astype(BF16), v)
    m_ref[idx] = m_next


def _init_softmax_state(m_ref, l_ref, acc_ref):
    m_ref[...] = jnp.full(m_ref.shape, NEG_INF, F32)
    l_ref[...] = jnp.zeros(l_ref.shape, F32)
    acc_ref[...] = jnp.zeros(acc_ref.shape, F32)


def _causal_mask(t):
    return lax.broadcasted_iota(I32, (t, t), 0) >= lax.broadcasted_iota(I32, (t, t), 1)


def _attn_b_kernel(q_ref, k_ref, v_ref, lq1, lk1, lq2, lk2, g_ref, o_ref, m_ref, l_ref, acc_ref,
                   *, t, slopes, scale, lam_init):
    h = pl.program_id(1)
    qi = pl.program_id(2)
    slope = jnp.float32(0.0)
    for hh, sv in enumerate(slopes):
        slope = jnp.where(h == hh, jnp.float32(sv), slope)
    q = q_ref[0]
    lo = lax.broadcasted_iota(I32, (t, LANES), 1) < HEAD_DIM
    zero = jnp.zeros_like(q)
    qs = (jnp.where(lo, q, zero), jnp.where(lo, zero, q))
    _init_softmax_state(m_ref, l_ref, acc_ref)
    colpos = lax.broadcasted_iota(I32, (1, t), 1).astype(F32)

    def block(kb, masked):
        off = pl.multiple_of(kb * t, t)
        k = k_ref[0, pl.ds(off, t), :]
        v = v_ref[0, pl.ds(off, t), :]
        bias = slope * (colpos + ((kb - qi) * t).astype(F32))
        for mi in range(2):
            s = _dot_nt(qs[mi], k) * scale + bias
            if masked:
                s = jnp.where(_causal_mask(t), s, NEG_INF)
            _softmax_update(s, v, m_ref, l_ref, acc_ref, mi)

    def body(kb, carry):
        block(kb, False)
        return carry

    lax.fori_loop(0, qi, body, 0)
    block(qi, True)

    lam = (jnp.exp(jnp.sum(lq1[...] * lk1[...], axis=1, keepdims=True))
           - jnp.exp(jnp.sum(lq2[...] * lk2[...], axis=1, keepdims=True)) + lam_init)
    o = acc_ref[0] / l_ref[0] - lam * (acc_ref[1] / l_ref[1])
    o = _rms(o, g_ref[...], SUBLN_EPS) * (1.0 - lam_init)
    o_ref[0] = o.astype(o_ref.dtype)


def attn_b(qkv3, lq1, lk1, lq2, lk2, subln, layer, *, t=512):
    b, s, width = qkv3.shape
    nh = width // (3 * LANES)
    slopes = tuple(float(2.0 ** (-8.0 * (i + 1) / nh)) for i in range(nh))
    lam_init = 0.8 - 0.6 * math.exp(-0.3 * layer)
    vec = pl.BlockSpec((1, HEAD_DIM), lambda bi, h, qi: (0, 0))
    return pl.pallas_call(
        functools.partial(_attn_b_kernel, t=t, slopes=slopes, scale=HEAD_DIM ** -0.5,
                          lam_init=lam_init),
        grid=(b, nh, s // t),
        in_specs=[pl.BlockSpec((1, t, LANES), lambda bi, h, qi: (bi, qi, h)),
                  pl.BlockSpec((1, s, LANES), lambda bi, h, qi: (bi, 0, nh + h)),
                  pl.BlockSpec((1, s, LANES), lambda bi, h, qi: (bi, 0, 2 * nh + h)),
                  vec, vec, vec, vec,
                  pl.BlockSpec((1, LANES), lambda bi, h, qi: (0, 0))],
        out_specs=pl.BlockSpec((1, t, LANES), lambda bi, h, qi: (bi, qi, h)),
        out_shape=jax.ShapeDtypeStruct((b, s, nh * LANES), BF16),
        scratch_shapes=[pltpu.VMEM((2, t, LANES), F32)] * 3,
        compiler_params=_cparams(3),
        name="attn_b",
    )(qkv3, qkv3, qkv3, lq1.reshape(1, -1), lk1.reshape(1, -1), lq2.reshape(1, -1),
      lk2.reshape(1, -1), subln.reshape(1, -1))


MLA_HEADS = 16
MLA_Q_RANK = 384
MLA_KV_RANK = 256
MLA_ROPE = 32


def _c_proj_kernel(x_ref, g_ref, wq_ref, wkv_ref, wpe_ref, qn_ref, kvn_ref, wqu_ref, wkvu_ref,
                   cos_ref, sm_ref, sp_ref, q_out, kv_out, pe_out):
    xn = _rms(x_ref[...], g_ref[...], NORM_EPS).astype(BF16)
    q_lat = _dot(xn, wq_ref[...])
    kv_lat = _dot(xn, wkv_ref[...])
    cos, sm, sp = cos_ref[...], sm_ref[...], sp_ref[...]
    pw = cos.shape[1]

    def rope(v):
        return v * cos + pltpu.roll(v, pw - MLA_ROPE // 2, 1) * sm + pltpu.roll(v, MLA_ROPE // 2, 1) * sp

    pe_out[...] = rope(_dot(xn, wpe_ref[...])).astype(pe_out.dtype)
    qn = _rms(q_lat, qn_ref[...], NORM_EPS).astype(BF16)
    for p in range(q_out.shape[1] // pw):
        sl = slice(p * pw, (p + 1) * pw)
        q_out[:, sl] = rope(_dot(qn, wqu_ref[:, sl])).astype(q_out.dtype)
    kvn = _rms(kv_lat, kvn_ref[...], NORM_EPS).astype(BF16)
    for p in range(kv_out.shape[1] // pw):
        sl = slice(p * pw, (p + 1) * pw)
        kv_out[:, sl] = _dot(kvn, wkvu_ref[:, sl]).astype(kv_out.dtype)


def _mla_layout():
    qcols = -np.ones(MLA_HEADS * LANES, np.int64)
    kvcols = np.zeros(MLA_HEADS * LANES, np.int64)
    for h in range(MLA_HEADS):
        qsrc = h * (HEAD_DIM + MLA_ROPE)
        ksrc = h * 2 * HEAD_DIM
        base = h * LANES
        if h % 2 == 0:
            qcols[base:base + MLA_ROPE] = qsrc + HEAD_DIM + np.arange(MLA_ROPE)
            qcols[base + HEAD_DIM:base + LANES] = qsrc + np.arange(HEAD_DIM)
            kvcols[base:base + HEAD_DIM] = ksrc + HEAD_DIM + np.arange(HEAD_DIM)
            kvcols[base + HEAD_DIM:base + LANES] = ksrc + np.arange(HEAD_DIM)
        else:
            qcols[base:base + HEAD_DIM] = qsrc + np.arange(HEAD_DIM)
            qcols[base + HEAD_DIM:base + HEAD_DIM + MLA_ROPE] = qsrc + HEAD_DIM + np.arange(MLA_ROPE)
            kvcols[base:base + HEAD_DIM] = ksrc + np.arange(HEAD_DIM)
            kvcols[base + HEAD_DIM:base + LANES] = ksrc + HEAD_DIM + np.arange(HEAD_DIM)
    pe_offsets = (0, LANES + HEAD_DIM)
    return qcols, kvcols, pe_offsets


def _rope_tables(s, pe_offsets):
    half = MLA_ROPE // 2
    inv_freq = ROPE_THETA ** (-jnp.arange(half, dtype=F32) / half)
    ang = jnp.arange(s, dtype=F32)[:, None] * inv_freq[None, :]
    cos, sin = jnp.cos(ang), jnp.sin(ang)
    c = jnp.ones((s, 2 * LANES), F32)
    sm = jnp.zeros((s, 2 * LANES), F32)
    sp = jnp.zeros((s, 2 * LANES), F32)
    for off in pe_offsets:
        c = c.at[:, off:off + half].set(cos).at[:, off + half:off + 2 * half].set(cos)
        sm = sm.at[:, off:off + half].set(-sin)
        sp = sp.at[:, off + half:off + 2 * half].set(sin)
    return c, sm, sp


def c_proj(x2, s, g, w_in, q_norm, w_q_up, kv_norm, w_kv_up, *, tm=512):
    n, d = x2.shape
    qcols, kvcols, pe_offsets = _mla_layout()
    wq = w_in[:, :MLA_Q_RANK].astype(BF16)
    wkv = w_in[:, MLA_Q_RANK:MLA_Q_RANK + MLA_KV_RANK].astype(BF16)
    w_pe = w_in[:, MLA_Q_RANK + MLA_KV_RANK:]
    wpe = jnp.zeros((d, 2 * LANES), F32)
    for off in pe_offsets:
        wpe = wpe.at[:, off:off + MLA_ROPE].set(w_pe)
    wpe = wpe.astype(BF16)
    wqu = jnp.where(jnp.asarray(qcols >= 0)[None, :],
                    w_q_up[:, np.maximum(qcols, 0)], 0.0).astype(BF16)
    wkvu = w_kv_up[:, kvcols].astype(BF16)
    cos, sm, sp = _rope_tables(s, pe_offsets)
    width = MLA_HEADS * LANES
    assert s % tm == 0
    nsb = s // tm
    full = lambda shape: pl.BlockSpec(shape, lambda i: (0, 0))
    tab = pl.BlockSpec((tm, 2 * LANES), lambda i: (i % nsb, 0))
    return pl.pallas_call(
        _c_proj_kernel,
        grid=(n // tm,),
        in_specs=[pl.BlockSpec((tm, d), lambda i: (i, 0)), full((1, d)),
                  full((d, MLA_Q_RANK)), full((d, MLA_KV_RANK)), full((d, 2 * LANES)),
                  full((1, MLA_Q_RANK)), full((1, MLA_KV_RANK)),
                  full((MLA_Q_RANK, width)), full((MLA_KV_RANK, width)), tab, tab, tab],
        out_specs=[pl.BlockSpec((tm, width), lambda i: (i, 0)),
                   pl.BlockSpec((tm, width), lambda i: (i, 0)),
                   pl.BlockSpec((tm, 2 * LANES), lambda i: (i, 0))],
        out_shape=[jax.ShapeDtypeStruct((n, width), BF16),
                   jax.ShapeDtypeStruct((n, width), BF16),
                   jax.ShapeDtypeStruct((n, 2 * LANES), BF16)],
        compiler_params=_cparams(1),
        name="c_proj",
    )(x2, g.reshape(1, d), wq, wkv, wpe, q_norm.reshape(1, -1), kv_norm.reshape(1, -1),
      wqu, wkvu, cos, sm, sp)


def _attn_c_kernel(q_ref, kv_ref, pe_ref, o_ref, kcat_ref, m_ref, l_ref, acc_ref, *, t, scale):
    qi = pl.program_id(2)
    pw = 2 * LANES

    @pl.when(qi == 0)
    def _():
        lane = lax.broadcasted_iota(I32, kcat_ref.shape, 1)
        is_pe = jnp.logical_or(lane < HEAD_DIM, lane >= pw - HEAD_DIM)
        kcat_ref[...] = jnp.where(is_pe, pe_ref[0], kv_ref[0])

    _init_softmax_state(m_ref, l_ref, acc_ref)

    def block(kb, masked):
        off = pl.multiple_of(kb * t, t)
        for jh in range(2):
            sl = slice(jh * LANES, (jh + 1) * LANES)
            s = _dot_nt(q_ref[0, :, sl], kcat_ref[pl.ds(off, t), sl]) * scale
            if masked:
                s = jnp.where(_causal_mask(t), s, NEG_INF)
            _softmax_update(s, kv_ref[0, pl.ds(off, t), sl], m_ref, l_ref, acc_ref, jh)

    def body(kb, carry):
        block(kb, False)
        return carry

    lax.fori_loop(0, qi, body, 0)
    block(qi, True)
    lo = lax.broadcasted_iota(I32, (t, LANES), 1) < HEAD_DIM
    o_ref[0] = jnp.where(lo, acc_ref[0] / l_ref[0], acc_ref[1] / l_ref[1]).astype(o_ref.dtype)


def attn_c(q3, kv3, pe3, *, t=512):
    b, s, width = q3.shape
    npair = width // (2 * LANES)
    return pl.pallas_call(
        functools.partial(_attn_c_kernel, t=t, scale=(HEAD_DIM + MLA_ROPE) ** -0.5),
        grid=(b, npair, s // t),
        in_specs=[pl.BlockSpec((1, t, 2 * LANES), lambda bi, p, qi: (bi, qi, p)),
                  pl.BlockSpec((1, s, 2 * LANES), lambda bi, p, qi: (bi, 0, p)),
                  pl.BlockSpec((1, s, 2 * LANES), lambda bi, p, qi: (bi, 0, 0))],
        out_specs=pl.BlockSpec((1, t, LANES), lambda bi, p, qi: (bi, qi, p)),
        out_shape=jax.ShapeDtypeStruct((b, s, npair * LANES), BF16),
        scratch_shapes=[pltpu.VMEM((s, 2 * LANES), BF16)] + [pltpu.VMEM((2, t, LANES), F32)] * 3,
        compiler_params=_cparams(3),
        name="attn_c",
    )(q3, kv3, pe3)


def _attn_d_kernel(q_ref, k_ref, v_ref, o_ref, r_ref, acc_ref, *, t, scale):
    qi = pl.program_id(2)
    q = q_ref[0]
    lo = lax.broadcasted_iota(I32, (t, LANES), 1) < HEAD_DIM
    zero = jnp.zeros_like(q)
    qs = (jnp.where(lo, q, zero), jnp.where(lo, zero, q))
    r_ref[...] = jnp.zeros(r_ref.shape, F32)
    acc_ref[...] = jnp.zeros(acc_ref.shape, F32)
    row = lax.broadcasted_iota(I32, (t, t), 0)
    col = lax.broadcasted_iota(I32, (t, t), 1)
    ones_ge = jnp.where(row >= col, 1.0, 0.0).astype(BF16)
    strict = row > col

    def block(kb, masked):
        off = pl.multiple_of(kb * t, t)
        k = k_ref[0, pl.ds(off, t), :]
        v = v_ref[0, pl.ds(off, t), :]
        for jh in range(2):
            z = _dot_nt(qs[jh], k) * scale
            sp = jnp.maximum(z, 0.0) + jnp.log1p(jnp.exp(-jnp.abs(z)))
            lk = -sp
            if masked:
                lk = jnp.where(strict, lk, 0.0)
            hi = lk.astype(BF16)
            lw = (lk - hi.astype(F32)).astype(BF16)
            csum = _dot(hi, ones_ge) + _dot(lw, ones_ge)
            r_prev = r_ref[jh]
            tail = _lane_tile(r_prev, t // LANES) + (csum - lk)
            a = jnp.exp((z - sp) + tail)
            if masked:
                a = jnp.where(strict, a, 0.0)
            acc_ref[jh] = acc_ref[jh] + _dot(a.astype(BF16), v)
            r_ref[jh] = r_prev + csum[:, 0:1]

    block(qi, True)

    def body(it, carry):
        block(qi - 1 - it, False)
        return carry

    lax.fori_loop(0, qi, body, 0)
    o_ref[0] = jnp.where(lo, acc_ref[0], acc_ref[1]).astype(o_ref.dtype)


def attn_d(qkv3, *, t=256):
    b, s, width = qkv3.shape
    npair = width // (3 * LANES)
    return pl.pallas_call(
        functools.partial(_attn_d_kernel, t=t, scale=HEAD_DIM ** -0.5),
        grid=(b, npair, s // t),
        in_specs=[pl.BlockSpec((1, t, LANES), lambda bi, p, qi: (bi, qi, p)),
                  pl.BlockSpec((1, s, LANES), lambda bi, p, qi: (bi, 0, npair + p)),
                  pl.BlockSpec((1, s, LANES), lambda bi, p, qi: (bi, 0, 2 * npair + p))],
        out_specs=pl.BlockSpec((1, t, LANES), lambda bi, p, qi: (bi, qi, p)),
        out_shape=jax.ShapeDtypeStruct((b, s, npair * LANES), BF16),
        scratch_shapes=[pltpu.VMEM((2, t, LANES), F32)] * 2,
        compiler_params=_cparams(3),
        name="attn_d",
    )(qkv3, qkv3, qkv3)


def _silu(x):
    return x / (1.0 + jnp.exp(-x))


def _ffn_kernel(x_ref, g_ref, wg_ref, wu_ref, wd_ref, out_ref, *, ff_chunk):
    x = x_ref[...]
    xn = _rms(x, g_ref[...], NORM_EPS).astype(BF16)
    y = x
    for c in range(0, wg_ref.shape[1], ff_chunk):
        h = _silu(_dot(xn, wg_ref[:, c:c + ff_chunk])) * _dot(xn, wu_ref[:, c:c + ff_chunk])
        y = y + _dot(h.astype(BF16), wd_ref[c:c + ff_chunk, :])
    out_ref[...] = y


def ffn(x2, g, wg, wu, wd, *, tm=512):
    n, d = x2.shape
    ff = wg.shape[1]
    ff_chunk = ff // 2 if (ff // 2) % LANES == 0 else ff
    const = lambda shape: pl.BlockSpec(shape, lambda i: (0, 0), pipeline_mode=pl.Buffered(1))
    return pl.pallas_call(
        functools.partial(_ffn_kernel, ff_chunk=ff_chunk),
        grid=(n // tm,),
        in_specs=[pl.BlockSpec((tm, d), lambda i: (i, 0)),
                  pl.BlockSpec((1, d), lambda i: (0, 0)),
                  const((d, ff)), const((d, ff)), const((ff, d))],
        out_specs=pl.BlockSpec((tm, d), lambda i: (i, 0)),
        out_shape=jax.ShapeDtypeStruct((n, d), F32),
        compiler_params=_cparams(1),
        name="ffn",
    )(x2, g.reshape(1, d), wg, wu, wd)


def _router_kernel(x_ref, g_ref, wr_ref, xn_ref, meta_ref, gate_ref, cnt_ref, run_ref):
    i = pl.program_id(0)
    tm = x_ref.shape[0]

    @pl.when(i == 0)
    def _():
        run_ref[...] = jnp.zeros(run_ref.shape, F32)

    xn = _rms(x_ref[...], g_ref[...], NORM_EPS)
    xn_ref[...] = xn
    logits = jnp.dot(xn, wr_ref[...], preferred_element_type=F32, precision=lax.Precision.HIGHEST)
    lane = lax.broadcasted_iota(I32, (tm, LANES), 1)
    lg = jnp.where(lane < N_EXPERTS, logits, NEG_INF)
    m1 = jnp.max(lg, axis=1, keepdims=True)
    e1 = jnp.min(jnp.where(lg == m1, lane, LANES), axis=1, keepdims=True)
    lg2 = jnp.where(lane == e1, NEG_INF, lg)
    m2 = jnp.max(lg2, axis=1, keepdims=True)
    e2 = jnp.min(jnp.where(lg2 == m2, lane, LANES), axis=1, keepdims=True)
    ex = jnp.exp(m2 - m1)
    g1 = 1.0 / (1.0 + ex)
    g2 = ex / (1.0 + ex)
    hit1 = lane == e1
    hit2 = lane == e2
    onehot = jnp.where(jnp.logical_or(hit1, hit2), 1.0, 0.0)
    earlier = (lax.broadcasted_iota(I32, (tm, tm), 0) > lax.broadcasted_iota(I32, (tm, tm), 1))
    before = _dot(jnp.where(earlier, 1.0, 0.0).astype(BF16), onehot.astype(BF16)) + run_ref[0:1, :]
    pos1 = jnp.sum(jnp.where(hit1, before, 0.0), axis=1, keepdims=True).astype(I32)
    pos2 = jnp.sum(jnp.where(hit2, before, 0.0), axis=1, keepdims=True).astype(I32)
    run_ref[...] = run_ref[...] + jnp.sum(onehot, axis=0, keepdims=True)
    meta_ref[...] = jnp.where(lane == 0, e1, jnp.where(lane == 1, e2, jnp.where(
        lane == 2, pos1, jnp.where(lane == 3, pos2, 0))))
    gate_ref[...] = jnp.where(lane == 0, g1, jnp.where(lane == 1, g2, 0.0))
    cnt_ref[...] = run_ref[...]


def router(x2, g, w_router, *, tm=512):
    n, d = x2.shape
    wr = jnp.zeros((d, LANES), F32).at[:, :N_EXPERTS].set(w_router)
    row = lambda w: pl.BlockSpec((tm, w), lambda i: (i, 0))
    return pl.pallas_call(
        _router_kernel,
        grid=(n // tm,),
        in_specs=[row(d), pl.BlockSpec((1, d), lambda i: (0, 0)),
                  pl.BlockSpec((d, LANES), lambda i: (0, 0))],
        out_specs=[row(d), row(LANES), row(LANES), pl.BlockSpec((8, LANES), lambda i: (0, 0))],
        out_shape=[jax.ShapeDtypeStruct((n, d), F32), jax.ShapeDtypeStruct((n, LANES), I32),
                   jax.ShapeDtypeStruct((n, LANES), F32), jax.ShapeDtypeStruct((8, LANES), F32)],
        scratch_shapes=[pltpu.VMEM((8, LANES), F32)],
        compiler_params=_cparams(1),
        name="router",
    )(x2, g.reshape(1, d), wr)


def _row_copy(src_ref, src_row, dst_ref, dst_row, sem):
    return pltpu.make_async_copy(src_ref.at[pl.ds(src_row, 1)], dst_ref.at[pl.ds(dst_row, 1)], sem)


def _dispatch_kernel(dest_ref, xn_ref, xb_in_ref, xb_ref, sem, *, tb):
    del xb_in_ref
    base = pl.program_id(0) * tb

    def issue(tt, carry):
        for k in range(2):
            _row_copy(xn_ref, base + tt, xb_ref, dest_ref[2 * tt + k], sem).start()
        return carry

    def drain(tt, carry):
        for k in range(2):
            _row_copy(xn_ref, 0, xb_ref, 0, sem).wait()
        return carry

    lax.fori_loop(0, tb, issue, 0)
    lax.fori_loop(0, tb, drain, 0)


def dispatch(dest_flat, xn, n_rows, *, tb=512):
    n, d = xn.shape
    xb0 = jnp.zeros((n_rows, d), F32)
    return pl.pallas_call(
        functools.partial(_dispatch_kernel, tb=tb),
        grid=(n // tb,),
        in_specs=[pl.BlockSpec((2 * tb,), lambda i: (i,), memory_space=pltpu.SMEM),
                  pl.BlockSpec(memory_space=pl.ANY), pl.BlockSpec(memory_space=pl.ANY)],
        out_specs=pl.BlockSpec(memory_space=pl.ANY),
        out_shape=jax.ShapeDtypeStruct((n_rows, d), F32),
        scratch_shapes=[pltpu.SemaphoreType.DMA(())],
        input_output_aliases={2: 0},
        compiler_params=_cparams(1),
        name="moe_dispatch",
    )(dest_flat, xn, xb0)


def _expert_kernel(be_ref, nu_ref, x_ref, wg_ref, wu_ref, wd_ref, y_ref, *, ff_chunk):
    i = pl.program_id(0)

    @pl.when(i < nu_ref[0])
    def _():
        xb = x_ref[...].astype(BF16)
        y = jnp.zeros(y_ref.shape, F32)
        for c in range(0, wg_ref.shape[2], ff_chunk):
            h = (_silu(_dot(xb, wg_ref[0, :, c:c + ff_chunk]))
                 * _dot(xb, wu_ref[0, :, c:c + ff_chunk]))
            y = y + _dot(h.astype(BF16), wd_ref[0, c:c + ff_chunk, :])
        y_ref[...] = y

    @pl.when(i >= nu_ref[0])
    def _():
        y_ref[...] = jnp.zeros(y_ref.shape, F32)


def experts(block_e, n_used, xb, wg, wu, wd, *, ff_chunk=512):
    rows, d = xb.shape
    ff = wg.shape[2]
    assert ff % ff_chunk == 0
    wspec = lambda shape: pl.BlockSpec(shape, lambda i, be, nu: (be[i], 0, 0),
                                       pipeline_mode=pl.Buffered(1))
    return pl.pallas_call(
        functools.partial(_expert_kernel, ff_chunk=ff_chunk),
        grid_spec=pltpu.PrefetchScalarGridSpec(
            num_scalar_prefetch=2,
            grid=(rows // MOE_ROWS,),
            in_specs=[pl.BlockSpec((MOE_ROWS, d), lambda i, be, nu: (i, 0)),
                      wspec((1, d, ff)), wspec((1, d, ff)), wspec((1, ff, d))],
            out_specs=pl.BlockSpec((MOE_ROWS, d), lambda i, be, nu: (i, 0)),
        ),
        out_shape=jax.ShapeDtypeStruct((rows, d), F32),
        compiler_params=_cparams(1),
        name="moe_experts",
    )(block_e, n_used, xb, wg, wu, wd)


def _combine_kernel(dest_ref, x_ref, gate_ref, fg_ref, yb_ref, out_ref, buf_ref, sem, *, tb, final):
    def issue(tt, carry):
        for k in range(2):
            _row_copy(yb_ref, dest_ref[2 * tt + k], buf_ref.at[k], tt, sem).start()
        return carry

    def drain(tt, carry):
        for k in range(2):
            _row_copy(yb_ref, 0, buf_ref.at[k], 0, sem).wait()
        return carry

    lax.fori_loop(0, tb, issue, 0)
    lax.fori_loop(0, tb, drain, 0)
    gate = gate_ref[...]
    y = x_ref[...] + gate[:, 0:1] * buf_ref[0] + gate[:, 1:2] * buf_ref[1]
    if final:
        y = _rms(y, fg_ref[...], NORM_EPS)
    out_ref[...] = y


def combine(dest_flat, x2, gates, yb, final_g, *, final, tb=512):
    n, d = x2.shape
    return pl.pallas_call(
        functools.partial(_combine_kernel, tb=tb, final=final),
        grid=(n // tb,),
        in_specs=[pl.BlockSpec((2 * tb,), lambda i: (i,), memory_space=pltpu.SMEM),
                  pl.BlockSpec((tb, d), lambda i: (i, 0)),
                  pl.BlockSpec((tb, LANES), lambda i: (i, 0)),
                  pl.BlockSpec((1, d), lambda i: (0, 0)),
                  pl.BlockSpec(memory_space=pl.ANY)],
        out_specs=pl.BlockSpec((tb, d), lambda i: (i, 0)),
        out_shape=jax.ShapeDtypeStruct((n, d), F32),
        scratch_shapes=[pltpu.VMEM((2, tb, d), F32), pltpu.SemaphoreType.DMA(())],
        compiler_params=_cparams(1),
        name="moe_combine",
    )(dest_flat, x2, gates, final_g.reshape(1, d), yb)


def moe(x2, g, w_router, wg, wu, wd, final_g, *, final):
    n, d = x2.shape
    xn, meta, gates, cnt = router(x2, g, w_router)
    counts = cnt[0, :N_EXPERTS].astype(I32)
    padded = (counts + MOE_ROWS - 1) // MOE_ROWS * MOE_ROWS
    pad_end = jnp.cumsum(padded)
    pad_start = pad_end - padded
    dest = (pad_start[meta[:, 0:2]] + meta[:, 2:4]).reshape(-1).astype(I32)
    n_blocks = -(-2 * n // MOE_ROWS) + N_EXPERTS
    block_e = jnp.minimum(jnp.searchsorted(pad_end, jnp.arange(n_blocks, dtype=I32) * MOE_ROWS,
                                           side="right"), N_EXPERTS - 1).astype(I32)
    n_used = (pad_end[-1:] // MOE_ROWS).astype(I32)
    xb = dispatch(dest, xn, n_blocks * MOE_ROWS)
    yb = experts(block_e, n_used, xb, wg.astype(BF16), wu.astype(BF16), wd.astype(BF16))
    return combine(dest, x2, gates, yb, final_g, final=final)


def kernel(x, a_norm, a_w_qkv, a_w_o, ffn0_norm, ffn0_w_gate, ffn0_w_up, ffn0_w_down, b_norm, b_w_qkv, b_lambda_q1, b_lambda_k1, b_lambda_q2, b_lambda_k2, b_subln, b_w_o, moe1_norm, moe1_w_router, moe1_w_gate, moe1_w_up, moe1_w_down, c_norm, c_w_in, c_q_norm, c_w_q_up, c_kv_norm, c_w_kv_up, c_w_o, ffn2_norm, ffn2_w_gate, ffn2_w_up, ffn2_w_down, d_norm, d_w_qkv, d_w_o, moe3_norm, moe3_w_router, moe3_w_gate, moe3_w_up, moe3_w_down, final_norm):
    b, s, d = x.shape
    n = b * s
    x2 = x.reshape(n, d)
    bf = lambda w: w.astype(BF16)

    qkv = norm_proj(x2, a_norm, bf(a_w_qkv)).reshape(b, s, -1)
    n_groups = len(A_PATTERN)
    slopes = 2.0 ** (-8.0 * np.arange(1, n_groups * A_HEADS + 1) / (n_groups * A_HEADS))
    outs, lses = [], []
    for gi, (window, dil) in enumerate(A_PATTERN):
        assert window // dil == QB
        o, lse = attn_a_group(qkv, gi, dil, slopes[gi * A_HEADS:(gi + 1) * A_HEADS])
        outs.append(o)
        lses.append(lse)
    x2 = mix_oproj_res(x2, outs, lses, bf(a_w_o))
    x2 = ffn(x2, ffn0_norm, bf(ffn0_w_gate), bf(ffn0_w_up), bf(ffn0_w_down))

    qkv = norm_proj(x2, b_norm, bf(b_w_qkv)).reshape(b, s, -1)
    o = attn_b(qkv, b_lambda_q1, b_lambda_k1, b_lambda_q2, b_lambda_k2, b_subln, layer=1)
    x2 = oproj_res(x2, o.reshape(n, -1), bf(b_w_o))
    x2 = moe(x2, moe1_norm, moe1_w_router, moe1_w_gate, moe1_w_up, moe1_w_down, final_norm,
             final=False)

    q, kv, pe = c_proj(x2, s, c_norm, c_w_in, c_q_norm, c_w_q_up, c_kv_norm, c_w_kv_up)
    o = attn_c(q.reshape(b, s, -1), kv.reshape(b, s, -1), pe.reshape(b, s, -1))
    x2 = oproj_res(x2, o.reshape(n, -1), bf(c_w_o))
    x2 = ffn(x2, ffn2_norm, bf(ffn2_w_gate), bf(ffn2_w_up), bf(ffn2_w_down))

    qkv = norm_proj(x2, d_norm, bf(d_w_qkv)).reshape(b, s, -1)
    o = attn_d(qkv)
    x2 = oproj_res(x2, o.reshape(n, -1), bf(d_w_o))
    x2 = moe(x2, moe3_norm, moe3_w_router, moe3_w_gate, moe3_w_up, moe3_w_down, final_norm,
             final=True)
    return x2.reshape(b, s, d)
```

```python
import functools
import math

import numpy as np
import jax
import jax.numpy as jnp
from jax import lax
from jax.experimental import pallas as pl
from jax.experimental.pallas import tpu as pltpu

F32 = jnp.float32
BF16 = jnp.bfloat16
I32 = jnp.int32

LANES = 128
HEAD_DIM = 64
NORM_EPS = 1e-6
SUBLN_EPS = 1e-5
ROPE_THETA = 10000.0
A_PATTERN = ((128, 1), (512, 4), (2048, 16))
A_HEADS = 8
QB = 128
N_EXPERTS = 8
MOE_ROWS = 512
VMEM_LIMIT = 56 * 1024 * 1024
NEG_INF = float("-inf")
EXP_UNDERFLOW = -105.0


def _cparams(n_axes, vmem=VMEM_LIMIT):
    return pltpu.CompilerParams(dimension_semantics=("arbitrary",) * n_axes,
                                vmem_limit_bytes=vmem)


def _dot(a, b):
    return jnp.dot(a, b, preferred_element_type=F32)


def _dot_nt(a, b):
    return lax.dot_general(a, b, (((1,), (1,)), ((), ())), preferred_element_type=F32)


def _rms(x, g, eps):
    return x * lax.rsqrt(jnp.mean(x * x, axis=-1, keepdims=True) + eps) * g


def _lane_tile(x, n):
    return x if n == 1 else jnp.concatenate([x] * n, axis=1)


def _norm_proj_kernel(x_ref, g_ref, w_ref, o_ref, *, col_chunk):
    xn = _rms(x_ref[...], g_ref[...], NORM_EPS).astype(BF16)
    n = o_ref.shape[1]
    for c in range(0, n, col_chunk):
        o_ref[:, c:c + col_chunk] = _dot(xn, w_ref[:, c:c + col_chunk]).astype(o_ref.dtype)


def norm_proj(x2, g, w, *, tm=512, col_chunk=512):
    n, d = x2.shape
    n_out = w.shape[1]
    assert n % tm == 0 and n_out % col_chunk == 0
    return pl.pallas_call(
        functools.partial(_norm_proj_kernel, col_chunk=col_chunk),
        grid=(n // tm,),
        in_specs=[pl.BlockSpec((tm, d), lambda i: (i, 0)),
                  pl.BlockSpec((1, d), lambda i: (0, 0)),
                  pl.BlockSpec((d, n_out), lambda i: (0, 0))],
        out_specs=pl.BlockSpec((tm, n_out), lambda i: (i, 0)),
        out_shape=jax.ShapeDtypeStruct((n, n_out), BF16),
        compiler_params=_cparams(1),
        name="norm_proj",
    )(x2, g.reshape(1, d), w)


def _oproj_kernel(x_ref, o_ref, w_ref, out_ref):
    out_ref[...] = x_ref[...] + _dot(o_ref[...], w_ref[...])


def oproj_res(x2, o2, w, *, tm=1024):
    n, d = x2.shape
    k = o2.shape[1]
    return pl.pallas_call(
        _oproj_kernel,
        grid=(n // tm,),
        in_specs=[pl.BlockSpec((tm, d), lambda i: (i, 0)),
                  pl.BlockSpec((tm, k), lambda i: (i, 0)),
                  pl.BlockSpec((k, d), lambda i: (0, 0))],
        out_specs=pl.BlockSpec((tm, d), lambda i: (i, 0)),
        out_shape=jax.ShapeDtypeStruct((n, d), F32),
        compiler_params=_cparams(1),
        name="oproj_res",
    )(x2, o2, w)


def _attn_a_kernel(q_ref, kc_ref, kp_ref, vc_ref, vp_ref, o_ref, lse_ref, *, slopes, scale):
    has_prev = pl.program_id(2) > 0
    row = lax.broadcasted_iota(I32, (QB, QB), 0)
    col = lax.broadcasted_iota(I32, (QB, QB), 1)
    rel_c = row - col
    rel_p = rel_c + QB
    valid_c = rel_c >= 0
    valid_p = jnp.logical_and(rel_p <= QB, has_prev)
    rel_cf = rel_c.astype(F32)
    rel_pf = rel_p.astype(F32)
    lo = lax.broadcasted_iota(I32, (QB, LANES), 1) < HEAD_DIM
    for p in range(A_HEADS // 2):
        sl = slice(p * LANES, (p + 1) * LANES)
        q = q_ref[0, :, sl] * scale
        kc, kp, vc, vp = kc_ref[0, :, sl], kp_ref[0, :, sl], vc_ref[0, :, sl], vp_ref[0, :, sl]
        outs, lses = [], []
        for jh in range(2):
            slope = slopes[2 * p + jh]
            qm = jnp.where(lo if jh == 0 else jnp.logical_not(lo), q, jnp.zeros_like(q))
            sc = jnp.where(valid_c, _dot_nt(qm, kc) - slope * rel_cf, NEG_INF)
            sp = jnp.where(valid_p, _dot_nt(qm, kp) - slope * rel_pf, NEG_INF)
            m = jnp.maximum(jnp.max(sc, axis=1, keepdims=True), jnp.max(sp, axis=1, keepdims=True))
            pc = jnp.exp(sc - m)
            pp = jnp.exp(sp - m)
            l = jnp.sum(pc, axis=1, keepdims=True) + jnp.sum(pp, axis=1, keepdims=True)
            acc = _dot(pc.astype(BF16), vc) + _dot(pp.astype(BF16), vp)
            outs.append(acc / l)
            lses.append(jnp.broadcast_to(m + jnp.log(l), (QB, LANES)))
        o_ref[0, :, sl] = jnp.where(lo, outs[0], outs[1]).astype(o_ref.dtype)
        lse_ref[0, :, sl] = jnp.where(lo, lses[0], lses[1])


def attn_a_group(qkv3, g, dil, slopes):
    b, s, width = qkv3.shape
    gw = A_HEADS * HEAD_DIM
    nblk = width // gw
    l = s // dil
    assert l % QB == 0
    nb = l // QB
    view = qkv3.reshape(b, l, dil * width)

    def spec(which, prev):
        def imap(bi, r, j):
            return (bi, jnp.maximum(j - 1, 0) if prev else j, r * nblk + which * 3 + g)
        return pl.BlockSpec((1, QB, gw), imap)

    o, lse = pl.pallas_call(
        functools.partial(_attn_a_kernel, slopes=tuple(float(x) * dil for x in slopes),
                          scale=HEAD_DIM ** -0.5),
        grid=(b, dil, nb),
        in_specs=[spec(0, False), spec(1, False), spec(1, True), spec(2, False), spec(2, True)],
        out_specs=[pl.BlockSpec((1, QB, gw), lambda bi, r, j: (bi, j, r)),
                   pl.BlockSpec((1, QB, gw), lambda bi, r, j: (bi, j, r))],
        out_shape=[jax.ShapeDtypeStruct((b, l, dil * gw), BF16),
                   jax.ShapeDtypeStruct((b, l, dil * gw), F32)],
        compiler_params=_cparams(3),
        name=f"attn_a_d{dil}",
    )(view, view, view, view, view)
    return o.reshape(b * s, gw), lse.reshape(b * s, gw)


def _mix_oproj_kernel(x_ref, o0, o1, o2, l0, l1, l2, w_ref, out_ref):
    ls = [l0[...], l1[...], l2[...]]
    m = jnp.maximum(jnp.maximum(ls[0], ls[1]), ls[2])
    es = [jnp.exp(v - m) for v in ls]
    den = es[0] + es[1] + es[2]
    mixed = ((es[0] / den) * o0[...].astype(F32) + (es[1] / den) * o1[...].astype(F32)
             + (es[2] / den) * o2[...].astype(F32))
    out_ref[...] = x_ref[...] + _dot(mixed.astype(BF16), w_ref[...])


def mix_oproj_res(x2, outs, lses, w, *, tm=1024):
    n, d = x2.shape
    gw = w.shape[0]
    blk = pl.BlockSpec((tm, gw), lambda i: (i, 0))
    return pl.pallas_call(
        _mix_oproj_kernel,
        grid=(n // tm,),
        in_specs=[pl.BlockSpec((tm, d), lambda i: (i, 0))] + [blk] * 6
                 + [pl.BlockSpec((gw, d), lambda i: (0, 0))],
        out_specs=pl.BlockSpec((tm, d), lambda i: (i, 0)),
        out_shape=jax.ShapeDtypeStruct((n, d), F32),
        compiler_params=_cparams(1),
        name="mix_oproj_res",
    )(x2, *outs, *lses, w)


def _softmax_update(s, v, m_ref, l_ref, acc_ref, idx):
    tk = s.shape[1]
    m_prev = m_ref[idx]
    m_next = jnp.maximum(m_prev, jnp.max(s, axis=1, keepdims=True))
    p = jnp.exp(s - _lane_tile(m_next, tk // LANES))
    alpha = jnp.exp(m_prev - m_next)
    l_ref[idx] = alpha * l_ref[idx] + jnp.sum(p, axis=1, keepdims=True)
    acc_ref[idx] = alpha * acc_ref[idx] + _dot(p.astype(BF16), v)
    m_ref[idx] = m_next


def _init_softmax_state(m_ref, l_ref, acc_ref):
    m_ref[...] = jnp.full(m_ref.shape, NEG_INF, F32)
    l_ref[...] = jnp.zeros(l_ref.shape, F32)
    acc_ref[...] = jnp.zeros(acc_ref.shape, F32)


def _causal_mask(t):
    return lax.broadcasted_iota(I32, (t, t), 0) >= lax.broadcasted_iota(I32, (t, t), 1)


def _attn_b_kernel(q_ref, k_ref, v_ref, lq1, lk1, lq2, lk2, g_ref, o_ref, m_ref, l_ref, acc_ref,
                   *, t, slopes, scale, lam_init):
    h = pl.program_id(1)
    qi = pl.program_id(2)
    slope = jnp.float32(0.0)
    for hh, sv in enumerate(slopes):
        slope = jnp.where(h == hh, jnp.float32(sv), slope)
    q = q_ref[0] * scale
    lo = lax.broadcasted_iota(I32, (t, LANES), 1) < HEAD_DIM
    zero = jnp.zeros_like(q)
    qs = (jnp.where(lo, q, zero), jnp.where(lo, zero, q))
    _init_softmax_state(m_ref, l_ref, acc_ref)
    colpos = lax.broadcasted_iota(I32, (1, t), 1).astype(F32)

    def block(kb, masked):
        off = pl.multiple_of(kb * t, t)
        k = k_ref[0, pl.ds(off, t), :]
        v = v_ref[0, pl.ds(off, t), :]
        bias = slope * (colpos + ((kb - qi) * t).astype(F32))
        for mi in range(2):
            s = _dot_nt(qs[mi], k) + bias
            if masked:
                s = jnp.where(_causal_mask(t), s, NEG_INF)
            _softmax_update(s, v, m_ref, l_ref, acc_ref, mi)

    def body(kb, carry):
        block(kb, False)
        return carry

    lax.fori_loop(0, qi, body, 0)
    block(qi, True)

    lam = (jnp.exp(jnp.sum(lq1[...] * lk1[...], axis=1, keepdims=True))
           - jnp.exp(jnp.sum(lq2[...] * lk2[...], axis=1, keepdims=True)) + lam_init)
    o = acc_ref[0] / l_ref[0] - lam * (acc_ref[1] / l_ref[1])
    o = _rms(o, g_ref[...], SUBLN_EPS) * (1.0 - lam_init)
    o_ref[0] = o.astype(o_ref.dtype)


def attn_b(qkv3, lq1, lk1, lq2, lk2, subln, layer, *, t=512):
    b, s, width = qkv3.shape
    nh = width // (3 * LANES)
    slopes = tuple(float(2.0 ** (-8.0 * (i + 1) / nh)) for i in range(nh))
    lam_init = 0.8 - 0.6 * math.exp(-0.3 * layer)
    vec = pl.BlockSpec((1, HEAD_DIM), lambda bi, h, qi: (0, 0))
    return pl.pallas_call(
        functools.partial(_attn_b_kernel, t=t, slopes=slopes, scale=HEAD_DIM ** -0.5,
                          lam_init=lam_init),
        grid=(b, nh, s // t),
        in_specs=[pl.BlockSpec((1, t, LANES), lambda bi, h, qi: (bi, qi, h)),
                  pl.BlockSpec((1, s, LANES), lambda bi, h, qi: (bi, 0, nh + h)),
                  pl.BlockSpec((1, s, LANES), lambda bi, h, qi: (bi, 0, 2 * nh + h)),
                  vec, vec, vec, vec,
                  pl.BlockSpec((1, LANES), lambda bi, h, qi: (0, 0))],
        out_specs=pl.BlockSpec((1, t, LANES), lambda bi, h, qi: (bi, qi, h)),
        out_shape=jax.ShapeDtypeStruct((b, s, nh * LANES), BF16),
        scratch_shapes=[pltpu.VMEM((2, t, LANES), F32)] * 3,
        compiler_params=_cparams(3),
        name="attn_b",
    )(qkv3, qkv3, qkv3, lq1.reshape(1, -1), lk1.reshape(1, -1), lq2.reshape(1, -1),
      lk2.reshape(1, -1), subln.reshape(1, -1))


MLA_HEADS = 16
MLA_Q_RANK = 384
MLA_KV_RANK = 256
MLA_ROPE = 32


def _c_proj_kernel(x_ref, g_ref, wq_ref, wkv_ref, wpe_ref, qn_ref, kvn_ref, wqu_ref, wkvu_ref,
                   cos_ref, sm_ref, sp_ref, q_out, kv_out, pe_out):
    xn = _rms(x_ref[...], g_ref[...], NORM_EPS).astype(BF16)
    q_lat = _dot(xn, wq_ref[...])
    kv_lat = _dot(xn, wkv_ref[...])
    cos, sm, sp = cos_ref[...], sm_ref[...], sp_ref[...]
    pw = cos.shape[1]

    def rope(v):
        return v * cos + pltpu.roll(v, pw - MLA_ROPE // 2, 1) * sm + pltpu.roll(v, MLA_ROPE // 2, 1) * sp

    pe_out[...] = rope(_dot(xn, wpe_ref[...])).astype(pe_out.dtype)
    qn = _rms(q_lat, qn_ref[...], NORM_EPS).astype(BF16)
    for p in range(q_out.shape[1] // pw):
        sl = slice(p * pw, (p + 1) * pw)
        q_out[:, sl] = rope(_dot(qn, wqu_ref[:, sl])).astype(q_out.dtype)
    kvn = _rms(kv_lat, kvn_ref[...], NORM_EPS).astype(BF16)
    for p in range(kv_out.shape[1] // pw):
        sl = slice(p * pw, (p + 1) * pw)
        kv_out[:, sl] = _dot(kvn, wkvu_ref[:, sl]).astype(kv_out.dtype)


def _mla_layout():
    qcols = -np.ones(MLA_HEADS * LANES, np.int64)
    kvcols = np.zeros(MLA_HEADS * LANES, np.int64)
    for h in range(MLA_HEADS):
        qsrc = h * (HEAD_DIM + MLA_ROPE)
        ksrc = h * 2 * HEAD_DIM
        base = h * LANES
        if h % 2 == 0:
            qcols[base:base + MLA_ROPE] = qsrc + HEAD_DIM + np.arange(MLA_ROPE)
            qcols[base + HEAD_DIM:base + LANES] = qsrc + np.arange(HEAD_DIM)
            kvcols[base:base + HEAD_DIM] = ksrc + HEAD_DIM + np.arange(HEAD_DIM)
            kvcols[base + HEAD_DIM:base + LANES] = ksrc + np.arange(HEAD_DIM)
        else:
            qcols[base:base + HEAD_DIM] = qsrc + np.arange(HEAD_DIM)
            qcols[base + HEAD_DIM:base + HEAD_DIM + MLA_ROPE] = qsrc + HEAD_DIM + np.arange(MLA_ROPE)
            kvcols[base:base + HEAD_DIM] = ksrc + np.arange(HEAD_DIM)
            kvcols[base + HEAD_DIM:base + LANES] = ksrc + HEAD_DIM + np.arange(HEAD_DIM)
    pe_offsets = (0, LANES + HEAD_DIM)
    return qcols, kvcols, pe_offsets


def _rope_tables(s, pe_offsets):
    half = MLA_ROPE // 2
    inv_freq = ROPE_THETA ** (-jnp.arange(half, dtype=F32) / half)
    ang = jnp.arange(s, dtype=F32)[:, None] * inv_freq[None, :]
    cos, sin = jnp.cos(ang), jnp.sin(ang)
    c = jnp.ones((s, 2 * LANES), F32)
    sm = jnp.zeros((s, 2 * LANES), F32)
    sp = jnp.zeros((s, 2 * LANES), F32)
    for off in pe_offsets:
        c = c.at[:, off:off + half].set(cos).at[:, off + half:off + 2 * half].set(cos)
        sm = sm.at[:, off:off + half].set(-sin)
        sp = sp.at[:, off + half:off + 2 * half].set(sin)
    return c, sm, sp


def c_proj(x2, s, g, w_in, q_norm, w_q_up, kv_norm, w_kv_up, *, tm=512):
    n, d = x2.shape
    qcols, kvcols, pe_offsets = _mla_layout()
    wq = w_in[:, :MLA_Q_RANK].astype(BF16)
    wkv = w_in[:, MLA_Q_RANK:MLA_Q_RANK + MLA_KV_RANK].astype(BF16)
    w_pe = w_in[:, MLA_Q_RANK + MLA_KV_RANK:]
    wpe = jnp.zeros((d, 2 * LANES), F32)
    for off in pe_offsets:
        wpe = wpe.at[:, off:off + MLA_ROPE].set(w_pe)
    wpe = wpe.astype(BF16)
    wqu = jnp.where(jnp.asarray(qcols >= 0)[None, :],
                    w_q_up[:, np.maximum(qcols, 0)], 0.0).astype(BF16)
    wkvu = w_kv_up[:, kvcols].astype(BF16)
    cos, sm, sp = _rope_tables(s, pe_offsets)
    width = MLA_HEADS * LANES
    assert s % tm == 0
    nsb = s // tm
    full = lambda shape: pl.BlockSpec(shape, lambda i: (0, 0))
    tab = pl.BlockSpec((tm, 2 * LANES), lambda i: (i % nsb, 0))
    return pl.pallas_call(
        _c_proj_kernel,
        grid=(n // tm,),
        in_specs=[pl.BlockSpec((tm, d), lambda i: (i, 0)), full((1, d)),
                  full((d, MLA_Q_RANK)), full((d, MLA_KV_RANK)), full((d, 2 * LANES)),
                  full((1, MLA_Q_RANK)), full((1, MLA_KV_RANK)),
                  full((MLA_Q_RANK, width)), full((MLA_KV_RANK, width)), tab, tab, tab],
        out_specs=[pl.BlockSpec((tm, width), lambda i: (i, 0)),
                   pl.BlockSpec((tm, width), lambda i: (i, 0)),
                   pl.BlockSpec((tm, 2 * LANES), lambda i: (i, 0))],
        out_shape=[jax.ShapeDtypeStruct((n, width), BF16),
                   jax.ShapeDtypeStruct((n, width), BF16),
                   jax.ShapeDtypeStruct((n, 2 * LANES), BF16)],
        compiler_params=_cparams(1),
        name="c_proj",
    )(x2, g.reshape(1, d), wq, wkv, wpe, q_norm.reshape(1, -1), kv_norm.reshape(1, -1),
      wqu, wkvu, cos, sm, sp)


def _attn_c_kernel(q_ref, kv_ref, pe_ref, o_ref, kcat_ref, m_ref, l_ref, acc_ref, *, t, scale):
    qi = pl.program_id(2)
    pw = 2 * LANES

    @pl.when(qi == 0)
    def _():
        lane = lax.broadcasted_iota(I32, kcat_ref.shape, 1)
        is_pe = jnp.logical_or(lane < HEAD_DIM, lane >= pw - HEAD_DIM)
        kcat_ref[...] = jnp.where(is_pe, pe_ref[0], kv_ref[0])

    _init_softmax_state(m_ref, l_ref, acc_ref)

    def block(kb, masked):
        off = pl.multiple_of(kb * t, t)
        for jh in range(2):
            sl = slice(jh * LANES, (jh + 1) * LANES)
            s = _dot_nt(q_ref[0, :, sl], kcat_ref[pl.ds(off, t), sl]) * scale
            if masked:
                s = jnp.where(_causal_mask(t), s, NEG_INF)
            _softmax_update(s, kv_ref[0, pl.ds(off, t), sl], m_ref, l_ref, acc_ref, jh)

    def body(kb, carry):
        block(kb, False)
        return carry

    lax.fori_loop(0, qi, body, 0)
    block(qi, True)
    lo = lax.broadcasted_iota(I32, (t, LANES), 1) < HEAD_DIM
    o_ref[0] = jnp.where(lo, acc_ref[0] / l_ref[0], acc_ref[1] / l_ref[1]).astype(o_ref.dtype)


def attn_c(q3, kv3, pe3, *, t=512):
    b, s, width = q3.shape
    npair = width // (2 * LANES)
    return pl.pallas_call(
        functools.partial(_attn_c_kernel, t=t, scale=(HEAD_DIM + MLA_ROPE) ** -0.5),
        grid=(b, npair, s // t),
        in_specs=[pl.BlockSpec((1, t, 2 * LANES), lambda bi, p, qi: (bi, qi, p)),
                  pl.BlockSpec((1, s, 2 * LANES), lambda bi, p, qi: (bi, 0, p)),
                  pl.BlockSpec((1, s, 2 * LANES), lambda bi, p, qi: (bi, 0, 0))],
        out_specs=pl.BlockSpec((1, t, LANES), lambda bi, p, qi: (bi, qi, p)),
        out_shape=jax.ShapeDtypeStruct((b, s, npair * LANES), BF16),
        scratch_shapes=[pltpu.VMEM((s, 2 * LANES), BF16)] + [pltpu.VMEM((2, t, LANES), F32)] * 3,
        compiler_params=_cparams(3),
        name="attn_c",
    )(q3, kv3, pe3)


def _attn_d_kernel(q_ref, k_ref, v_ref, o_ref, r_ref, acc_ref, *, t, scale):
    qi = pl.program_id(2)
    q = q_ref[0] * scale
    lo = lax.broadcasted_iota(I32, (t, LANES), 1) < HEAD_DIM
    zero = jnp.zeros_like(q)
    qs = (jnp.where(lo, q, zero), jnp.where(lo, zero, q))
    r_ref[...] = jnp.zeros(r_ref.shape, F32)
    acc_ref[...] = jnp.zeros(acc_ref.shape, F32)
    row = lax.broadcasted_iota(I32, (2 * t, t), 0)
    col = lax.broadcasted_iota(I32, (2 * t, t), 1)
    ones_ge = jnp.where(jnp.where(row >= t, row - t, row) >= col, 1.0, 0.0).astype(BF16)
    strict = lax.broadcasted_iota(I32, (t, t), 0) > lax.broadcasted_iota(I32, (t, t), 1)

    def block(kb, masked):
        off = pl.multiple_of(kb * t, t)
        k = k_ref[0, pl.ds(off, t), :]
        v = v_ref[0, pl.ds(off, t), :]
        for jh in range(2):
            z = _dot_nt(qs[jh], k)
            lg = jnp.log(1.0 + jnp.exp(-jnp.abs(z)))
            lk = jnp.minimum(-z, 0.0) - lg
            if masked:
                lk = jnp.where(strict, lk, 0.0)
            hi = lk.astype(BF16)
            lw = (lk - hi.astype(F32)).astype(BF16)
            csum = _dot(jnp.concatenate([hi, lw], axis=1), ones_ge)
            r_prev = r_ref[jh]
            tail = _lane_tile(r_prev, t // LANES) + (csum - lk)
            a = jnp.exp((jnp.minimum(z, 0.0) - lg) + tail)
            if masked:
                a = jnp.where(strict, a, 0.0)
            acc_ref[jh] = acc_ref[jh] + _dot(a.astype(BF16), v)
            r_ref[jh] = r_prev + csum[:, 0:1]

    block(qi, True)

    def cond(carry):
        it, live = carry
        return jnp.logical_and(it < qi, live)

    def body(carry):
        it, _ = carry
        block(qi - 1 - it, False)
        live = jnp.max(jnp.maximum(r_ref[0], r_ref[1])) >= EXP_UNDERFLOW
        return it + 1, live

    lax.while_loop(cond, body, (jnp.int32(0), jnp.bool_(True)))
    o_ref[0] = jnp.where(lo, acc_ref[0], acc_ref[1]).astype(o_ref.dtype)


def attn_d(qkv3, *, t=256):
    b, s, width = qkv3.shape
    npair = width // (3 * LANES)
    return pl.pallas_call(
        functools.partial(_attn_d_kernel, t=t, scale=HEAD_DIM ** -0.5),
        grid=(b, npair, s // t),
        in_specs=[pl.BlockSpec((1, t, LANES), lambda bi, p, qi: (bi, qi, p)),
                  pl.BlockSpec((1, s, LANES), lambda bi, p, qi: (bi, 0, npair + p)),
                  pl.BlockSpec((1, s, LANES), lambda bi, p, qi: (bi, 0, 2 * npair + p))],
        out_specs=pl.BlockSpec((1, t, LANES), lambda bi, p, qi: (bi, qi, p)),
        out_shape=jax.ShapeDtypeStruct((b, s, npair * LANES), BF16),
        scratch_shapes=[pltpu.VMEM((2, t, LANES), F32)] * 2,
        compiler_params=_cparams(3),
        name="attn_d",
    )(qkv3, qkv3, qkv3)


def _silu(x):
    return x / (1.0 + jnp.exp(-x))


def _ffn_kernel(x_ref, g_ref, wg_ref, wu_ref, wd_ref, out_ref, *, ff_chunk):
    x = x_ref[...]
    xn = _rms(x, g_ref[...], NORM_EPS).astype(BF16)
    y = x
    for c in range(0, wg_ref.shape[1], ff_chunk):
        h = _silu(_dot(xn, wg_ref[:, c:c + ff_chunk])) * _dot(xn, wu_ref[:, c:c + ff_chunk])
        y = y + _dot(h.astype(BF16), wd_ref[c:c + ff_chunk, :])
    out_ref[...] = y


def ffn(x2, g, wg, wu, wd, *, tm=512):
    n, d = x2.shape
    ff = wg.shape[1]
    ff_chunk = ff // 2 if (ff // 2) % LANES == 0 else ff
    const = lambda shape: pl.BlockSpec(shape, lambda i: (0, 0), pipeline_mode=pl.Buffered(1))
    return pl.pallas_call(
        functools.partial(_ffn_kernel, ff_chunk=ff_chunk),
        grid=(n // tm,),
        in_specs=[pl.BlockSpec((tm, d), lambda i: (i, 0)),
                  pl.BlockSpec((1, d), lambda i: (0, 0)),
                  const((d, ff)), const((d, ff)), const((ff, d))],
        out_specs=pl.BlockSpec((tm, d), lambda i: (i, 0)),
        out_shape=jax.ShapeDtypeStruct((n, d), F32),
        compiler_params=_cparams(1),
        name="ffn",
    )(x2, g.reshape(1, d), wg, wu, wd)


def _router_kernel(x_ref, g_ref, wr_ref, meta_ref, gate_ref, cnt_ref, run_ref):
    i = pl.program_id(0)
    tm = x_ref.shape[0]

    @pl.when(i == 0)
    def _():
        run_ref[...] = jnp.zeros(run_ref.shape, F32)

    xn = _rms(x_ref[...], g_ref[...], NORM_EPS)
    logits = jnp.dot(xn, wr_ref[...], preferred_element_type=F32, precision=lax.Precision.HIGHEST)
    lane = lax.broadcasted_iota(I32, (tm, LANES), 1)
    lg = jnp.where(lane < N_EXPERTS, logits, NEG_INF)
    m1 = jnp.max(lg, axis=1, keepdims=True)
    e1 = jnp.min(jnp.where(lg == m1, lane, LANES), axis=1, keepdims=True)
    lg2 = jnp.where(lane == e1, NEG_INF, lg)
    m2 = jnp.max(lg2, axis=1, keepdims=True)
    e2 = jnp.min(jnp.where(lg2 == m2, lane, LANES), axis=1, keepdims=True)
    ex = jnp.exp(m2 - m1)
    g1 = 1.0 / (1.0 + ex)
    g2 = ex / (1.0 + ex)
    hit1 = lane == e1
    hit2 = lane == e2
    onehot = jnp.where(jnp.logical_or(hit1, hit2), 1.0, 0.0)
    earlier = (lax.broadcasted_iota(I32, (tm, tm), 0) > lax.broadcasted_iota(I32, (tm, tm), 1))
    before = _dot(jnp.where(earlier, 1.0, 0.0).astype(BF16), onehot.astype(BF16)) + run_ref[0:1, :]
    pos1 = jnp.sum(jnp.where(hit1, before, 0.0), axis=1, keepdims=True).astype(I32)
    pos2 = jnp.sum(jnp.where(hit2, before, 0.0), axis=1, keepdims=True).astype(I32)
    run_ref[...] = run_ref[...] + jnp.sum(onehot, axis=0, keepdims=True)
    meta_ref[...] = jnp.where(lane == 0, e1, jnp.where(lane == 1, e2, jnp.where(
        lane == 2, pos1, jnp.where(lane == 3, pos2, 0))))
    gate_ref[...] = jnp.where(lane == 0, g1, jnp.where(lane == 1, g2, 0.0))
    cnt_ref[...] = run_ref[...]


def router(x2, g, w_router, *, tm=512):
    n, d = x2.shape
    wr = jnp.zeros((d, LANES), F32).at[:, :N_EXPERTS].set(w_router)
    row = lambda w: pl.BlockSpec((tm, w), lambda i: (i, 0))
    return pl.pallas_call(
        _router_kernel,
        grid=(n // tm,),
        in_specs=[row(d), pl.BlockSpec((1, d), lambda i: (0, 0)),
                  pl.BlockSpec((d, LANES), lambda i: (0, 0))],
        out_specs=[row(LANES), row(LANES), pl.BlockSpec((8, LANES), lambda i: (0, 0))],
        out_shape=[jax.ShapeDtypeStruct((n, LANES), I32),
                   jax.ShapeDtypeStruct((n, LANES), F32), jax.ShapeDtypeStruct((8, LANES), F32)],
        scratch_shapes=[pltpu.VMEM((8, LANES), F32)],
        compiler_params=_cparams(1),
        name="router",
    )(x2, g.reshape(1, d), wr)


def _row_copy(src_ref, src_row, dst_ref, dst_row, sem):
    return pltpu.make_async_copy(src_ref.at[pl.ds(src_row, 1)], dst_ref.at[pl.ds(dst_row, 1)], sem)


def _dispatch_kernel(dest_ref, x_ref, g_ref, xb_in_ref, xb_ref, xn_ref, sem, *, tb):
    del xb_in_ref
    xn_ref[...] = _rms(x_ref[...], g_ref[...], NORM_EPS)

    def issue(tt, carry):
        for k in range(2):
            _row_copy(xn_ref, tt, xb_ref, dest_ref[2 * tt + k], sem).start()
        return carry

    def drain(tt, carry):
        for k in range(2):
            _row_copy(xn_ref, 0, xb_ref, 0, sem).wait()
        return carry

    lax.fori_loop(0, tb, issue, 0)
    lax.fori_loop(0, tb, drain, 0)


def dispatch(dest_flat, x2, g, n_rows, *, tb=512):
    n, d = x2.shape
    xb0 = jnp.zeros((n_rows, d), F32)
    return pl.pallas_call(
        functools.partial(_dispatch_kernel, tb=tb),
        grid=(n // tb,),
        in_specs=[pl.BlockSpec((2 * tb,), lambda i: (i,), memory_space=pltpu.SMEM),
                  pl.BlockSpec((tb, d), lambda i: (i, 0)),
                  pl.BlockSpec((1, d), lambda i: (0, 0)),
                  pl.BlockSpec(memory_space=pl.ANY)],
        out_specs=pl.BlockSpec(memory_space=pl.ANY),
        out_shape=jax.ShapeDtypeStruct((n_rows, d), F32),
        scratch_shapes=[pltpu.VMEM((tb, d), F32), pltpu.SemaphoreType.DMA(())],
        input_output_aliases={3: 0},
        compiler_params=_cparams(1),
        name="moe_dispatch",
    )(dest_flat, x2, g.reshape(1, d), xb0)


def _expert_kernel(be_ref, nu_ref, x_ref, wg_ref, wu_ref, wd_ref, y_ref, *, ff_chunk):
    i = pl.program_id(0)

    @pl.when(i < nu_ref[0])
    def _():
        xb = x_ref[...].astype(BF16)
        y = jnp.zeros(y_ref.shape, F32)
        for c in range(0, wg_ref.shape[2], ff_chunk):
            h = (_silu(_dot(xb, wg_ref[0, :, c:c + ff_chunk]))
                 * _dot(xb, wu_ref[0, :, c:c + ff_chunk]))
            y = y + _dot(h.astype(BF16), wd_ref[0, c:c + ff_chunk, :])
        y_ref[...] = y

    @pl.when(i >= nu_ref[0])
    def _():
        y_ref[...] = jnp.zeros(y_ref.shape, F32)


def experts(block_e, n_used, xb, wg, wu, wd, *, ff_chunk=512):
    rows, d = xb.shape
    ff = wg.shape[2]
    assert ff % ff_chunk == 0
    wspec = lambda shape: pl.BlockSpec(shape, lambda i, be, nu: (be[i], 0, 0),
                                       pipeline_mode=pl.Buffered(1))
    return pl.pallas_call(
        functools.partial(_expert_kernel, ff_chunk=ff_chunk),
        grid_spec=pltpu.PrefetchScalarGridSpec(
            num_scalar_prefetch=2,
            grid=(rows // MOE_ROWS,),
            in_specs=[pl.BlockSpec((MOE_ROWS, d), lambda i, be, nu: (i, 0)),
                      wspec((1, d, ff)), wspec((1, d, ff)), wspec((1, ff, d))],
            out_specs=pl.BlockSpec((MOE_ROWS, d), lambda i, be, nu: (i, 0)),
        ),
        out_shape=jax.ShapeDtypeStruct((rows, d), F32),
        compiler_params=_cparams(1),
        name="moe_experts",
    )(block_e, n_used, xb, wg, wu, wd)


def _combine_kernel(dest_ref, x_ref, gate_ref, fg_ref, yb_ref, out_ref, buf_ref, sem, *, tb, final):
    def issue(tt, carry):
        for k in range(2):
            _row_copy(yb_ref, dest_ref[2 * tt + k], buf_ref.at[k], tt, sem).start()
        return carry

    def drain(tt, carry):
        for k in range(2):
            _row_copy(yb_ref, 0, buf_ref.at[k], 0, sem).wait()
        return carry

    lax.fori_loop(0, tb, issue, 0)
    lax.fori_loop(0, tb, drain, 0)
    gate = gate_ref[...]
    y = x_ref[...] + gate[:, 0:1] * buf_ref[0] + gate[:, 1:2] * buf_ref[1]
    if final:
        y = _rms(y, fg_ref[...], NORM_EPS)
    out_ref[...] = y


def combine(dest_flat, x2, gates, yb, final_g, *, final, tb=512):
    n, d = x2.shape
    return pl.pallas_call(
        functools.partial(_combine_kernel, tb=tb, final=final),
        grid=(n // tb,),
        in_specs=[pl.BlockSpec((2 * tb,), lambda i: (i,), memory_space=pltpu.SMEM),
                  pl.BlockSpec((tb, d), lambda i: (i, 0)),
                  pl.BlockSpec((tb, LANES), lambda i: (i, 0)),
                  pl.BlockSpec((1, d), lambda i: (0, 0)),
                  pl.BlockSpec(memory_space=pl.ANY)],
        out_specs=pl.BlockSpec((tb, d), lambda i: (i, 0)),
        out_shape=jax.ShapeDtypeStruct((n, d), F32),
        scratch_shapes=[pltpu.VMEM((2, tb, d), F32), pltpu.SemaphoreType.DMA(())],
        compiler_params=_cparams(1),
        name="moe_combine",
    )(dest_flat, x2, gates, final_g.reshape(1, d), yb)


def moe(x2, g, w_router, wg, wu, wd, final_g, *, final):
    n, d = x2.shape
    meta, gates, cnt = router(x2, g, w_router)
    counts = cnt[0, :N_EXPERTS].astype(I32)
    padded = (counts + MOE_ROWS - 1) // MOE_ROWS * MOE_ROWS
    pad_end = jnp.cumsum(padded)
    pad_start = pad_end - padded
    dest = (pad_start[meta[:, 0:2]] + meta[:, 2:4]).reshape(-1).astype(I32)
    n_blocks = -(-2 * n // MOE_ROWS) + N_EXPERTS
    starts = jnp.arange(n_blocks, dtype=I32) * MOE_ROWS
    block_e = jnp.minimum(jnp.sum((pad_end[None, :] <= starts[:, None]).astype(I32), axis=1),
                          N_EXPERTS - 1)
    n_used = (pad_end[-1:] // MOE_ROWS).astype(I32)
    xb = dispatch(dest, x2, g, n_blocks * MOE_ROWS)
    yb = experts(block_e, n_used, xb, wg.astype(BF16), wu.astype(BF16), wd.astype(BF16))
    return combine(dest, x2, gates, yb, final_g, final=final)


def kernel(x, a_norm, a_w_qkv, a_w_o, ffn0_norm, ffn0_w_gate, ffn0_w_up, ffn0_w_down, b_norm, b_w_qkv, b_lambda_q1, b_lambda_k1, b_lambda_q2, b_lambda_k2, b_subln, b_w_o, moe1_norm, moe1_w_router, moe1_w_gate, moe1_w_up, moe1_w_down, c_norm, c_w_in, c_q_norm, c_w_q_up, c_kv_norm, c_w_kv_up, c_w_o, ffn2_norm, ffn2_w_gate, ffn2_w_up, ffn2_w_down, d_norm, d_w_qkv, d_w_o, moe3_norm, moe3_w_router, moe3_w_gate, moe3_w_up, moe3_w_down, final_norm):
    b, s, d = x.shape
    n = b * s
    x2 = x.reshape(n, d)
    bf = lambda w: w.astype(BF16)

    qkv = norm_proj(x2, a_norm, bf(a_w_qkv)).reshape(b, s, -1)
    n_groups = len(A_PATTERN)
    slopes = 2.0 ** (-8.0 * np.arange(1, n_groups * A_HEADS + 1) / (n_groups * A_HEADS))
    outs, lses = [], []
    for gi, (window, dil) in enumerate(A_PATTERN):
        assert window // dil == QB
        o, lse = attn_a_group(qkv, gi, dil, slopes[gi * A_HEADS:(gi + 1) * A_HEADS])
        outs.append(o)
        lses.append(lse)
    x2 = mix_oproj_res(x2, outs, lses, bf(a_w_o))
    x2 = ffn(x2, ffn0_norm, bf(ffn0_w_gate), bf(ffn0_w_up), bf(ffn0_w_down))

    qkv = norm_proj(x2, b_norm, bf(b_w_qkv)).reshape(b, s, -1)
    o = attn_b(qkv, b_lambda_q1, b_lambda_k1, b_lambda_q2, b_lambda_k2, b_subln, layer=1)
    x2 = oproj_res(x2, o.reshape(n, -1), bf(b_w_o))
    x2 = moe(x2, moe1_norm, moe1_w_router, moe1_w_gate, moe1_w_up, moe1_w_down, final_norm,
             final=False)

    q, kv, pe = c_proj(x2, s, c_norm, c_w_in, c_q_norm, c_w_q_up, c_kv_norm, c_w_kv_up)
    o = attn_c(q.reshape(b, s, -1), kv.reshape(b, s, -1), pe.reshape(b, s, -1))
    x2 = oproj_res(x2, o.reshape(n, -1), bf(c_w_o))
    x2 = ffn(x2, ffn2_norm, bf(ffn2_w_gate), bf(ffn2_w_up), bf(ffn2_w_down))

    qkv = norm_proj(x2, d_norm, bf(d_w_qkv)).reshape(b, s, -1)
    o = attn_d(qkv)
    x2 = oproj_res(x2, o.reshape(n, -1), bf(d_w_o))
    x2 = moe(x2, moe3_norm, moe3_w_router, moe3_w_gate, moe3_w_up, moe3_w_down, final_norm,
             final=True)
    return x2.reshape(b, s, d)
```

```python
import functools
import math

import numpy as np
import jax
import jax.numpy as jnp
from jax import lax
from jax.experimental import pallas as pl
from jax.experimental.pallas import tpu as pltpu

F32 = jnp.float32
BF16 = jnp.bfloat16
I32 = jnp.int32

LANES = 128
MXU_DIM = 256
HEAD_DIM = 64
NORM_EPS = 1e-6
SUBLN_EPS = 1e-5
ROPE_THETA = 10000.0
A_PATTERN = ((128, 1), (512, 4), (2048, 16))
A_HEADS = 8
QB = 128
N_EXPERTS = 8
MOE_ROWS = 512
VMEM_LIMIT = 56 * 1024 * 1024
NEG_INF = float("-inf")
EXP_UNDERFLOW = -105.0


def _cparams(n_axes, vmem=VMEM_LIMIT):
    return pltpu.CompilerParams(dimension_semantics=("arbitrary",) * n_axes,
                                vmem_limit_bytes=vmem)


def _dot(a, b):
    return jnp.dot(a, b, preferred_element_type=F32)


def _dot_nt(a, b):
    return lax.dot_general(a, b, (((1,), (1,)), ((), ())), preferred_element_type=F32)


def _rms(x, g, eps):
    return x * lax.rsqrt(jnp.mean(x * x, axis=-1, keepdims=True) + eps) * g


def _lane_tile(x, n):
    return x if n == 1 else jnp.concatenate([x] * n, axis=1)


def _norm_proj_kernel(x_ref, g_ref, w_ref, o_ref, *, col_chunk):
    xn = _rms(x_ref[...], g_ref[...], NORM_EPS).astype(BF16)
    n = o_ref.shape[1]
    for c in range(0, n, col_chunk):
        o_ref[:, c:c + col_chunk] = _dot(xn, w_ref[:, c:c + col_chunk]).astype(o_ref.dtype)


def norm_proj(x2, g, w, *, tm=512, col_chunk=512):
    n, d = x2.shape
    n_out = w.shape[1]
    assert n % tm == 0 and n_out % col_chunk == 0
    return pl.pallas_call(
        functools.partial(_norm_proj_kernel, col_chunk=col_chunk),
        grid=(n // tm,),
        in_specs=[pl.BlockSpec((tm, d), lambda i: (i, 0)),
                  pl.BlockSpec((1, d), lambda i: (0, 0)),
                  pl.BlockSpec((d, n_out), lambda i: (0, 0))],
        out_specs=pl.BlockSpec((tm, n_out), lambda i: (i, 0)),
        out_shape=jax.ShapeDtypeStruct((n, n_out), BF16),
        compiler_params=_cparams(1),
        name="norm_proj",
    )(x2, g.reshape(1, d), w)


def _oproj_kernel(x_ref, o_ref, w_ref, out_ref):
    out_ref[...] = x_ref[...] + _dot(o_ref[...], w_ref[...])


def oproj_res(x2, o2, w, *, tm=1024):
    n, d = x2.shape
    k = o2.shape[1]
    return pl.pallas_call(
        _oproj_kernel,
        grid=(n // tm,),
        in_specs=[pl.BlockSpec((tm, d), lambda i: (i, 0)),
                  pl.BlockSpec((tm, k), lambda i: (i, 0)),
                  pl.BlockSpec((k, d), lambda i: (0, 0))],
        out_specs=pl.BlockSpec((tm, d), lambda i: (i, 0)),
        out_shape=jax.ShapeDtypeStruct((n, d), F32),
        compiler_params=_cparams(1),
        name="oproj_res",
    )(x2, o2, w)


LOG2E = math.log2(math.e)
LN2 = math.log(2.0)


def _a_proj_kernel(x_ref, g_ref, w_ref, *rest, dils, gcols):
    outs, scr = rest[:len(dils)], rest[len(dils)]
    xn = _rms(x_ref[...], g_ref[...], NORM_EPS).astype(BF16)
    tm = x_ref.shape[0]
    for gi, dil in enumerate(dils):
        res = _dot(xn, w_ref[:, gi * gcols:(gi + 1) * gcols])
        if dil == 1:
            outs[gi][...] = res.astype(BF16)
        else:
            for c in range(gcols // LANES):
                scr[c] = res[:, c * LANES:(c + 1) * LANES]
            for r in range(dil):
                for c in range(gcols // LANES):
                    col = r * gcols + c * LANES
                    outs[gi][:, col:col + LANES] = (
                        scr[c, pl.ds(r, tm // dil, stride=dil), :].astype(BF16))


def a_proj(x2, g, w, dils, *, tm=512):
    n, d = x2.shape
    gcols = w.shape[1] // len(dils)
    return pl.pallas_call(
        functools.partial(_a_proj_kernel, dils=dils, gcols=gcols),
        grid=(n // tm,),
        in_specs=[pl.BlockSpec((tm, d), lambda i: (i, 0)),
                  pl.BlockSpec((1, d), lambda i: (0, 0)),
                  pl.BlockSpec((d, w.shape[1]), lambda i: (0, 0))],
        out_specs=[pl.BlockSpec((tm // dil, dil * gcols), lambda i: (i, 0)) for dil in dils],
        out_shape=[jax.ShapeDtypeStruct((n // dil, dil * gcols), BF16) for dil in dils],
        scratch_shapes=[pltpu.VMEM((gcols // LANES, tm, LANES), F32)],
        compiler_params=_cparams(1),
        name="a_proj",
    )(x2, g.reshape(1, d), w)


def _attn_a_kernel(q_ref, kc_ref, kp_ref, vc_ref, vp_ref, bc_ref, bp_ref, o_ref, lse_ref):
    has_prev = pl.program_id(2) > 0
    hw = MXU_DIM
    nh = hw // HEAD_DIM
    head_of_lane = lax.broadcasted_iota(I32, (QB, hw), 1) // HEAD_DIM
    for half in range(q_ref.shape[2] // hw):
        sl = slice(half * hw, (half + 1) * hw)
        tab = slice(half * nh * QB, (half + 1) * nh * QB)
        q = q_ref[0, :, sl]
        zero = jnp.zeros_like(q)
        q_all = jnp.concatenate([jnp.where(head_of_lane == h, q, zero) for h in range(nh)], axis=0)
        sc = _dot_nt(q_all, kc_ref[0, :, sl]) + bc_ref[tab, :]
        sp = jnp.where(has_prev, _dot_nt(q_all, kp_ref[0, :, sl]) + bp_ref[tab, :], NEG_INF)
        m = jnp.maximum(jnp.max(sc, axis=1, keepdims=True), jnp.max(sp, axis=1, keepdims=True))
        pc = jnp.exp2(sc - m)
        pp = jnp.exp2(sp - m)
        l = jnp.sum(pc, axis=1, keepdims=True) + jnp.sum(pp, axis=1, keepdims=True)
        acc = _dot(pc.astype(BF16), vc_ref[0, :, sl]) + _dot(pp.astype(BF16), vp_ref[0, :, sl])
        lse = (m + jnp.log2(l)) * LN2
        o = jnp.zeros((QB, hw), F32)
        ls = jnp.zeros((QB, hw), F32)
        for h in range(nh):
            rows = slice(h * QB, (h + 1) * QB)
            mine = head_of_lane == h
            o = jnp.where(mine, acc[rows] / l[rows], o)
            ls = jnp.where(mine, lse[rows], ls)
        o_ref[0, :, sl] = o.astype(o_ref.dtype)
        lse_ref[0, :, sl] = ls


def _a_bias_tables(dil, slopes):
    rel_c = np.arange(QB)[:, None] - np.arange(QB)[None, :]
    rel_p = rel_c + QB
    sl = np.asarray(slopes, np.float64)[:, None, None] * dil * LOG2E
    bc = np.where(rel_c >= 0, -sl * rel_c, -np.inf).reshape(-1, QB)
    bp = np.where(rel_p <= QB, -sl * rel_p, -np.inf).reshape(-1, QB)
    return jnp.asarray(bc, F32), jnp.asarray(bp, F32)


def attn_a_group(view2, b, dil, slopes):
    gw = A_HEADS * HEAD_DIM
    l = view2.shape[0] // b
    assert l % QB == 0 and view2.shape[1] == dil * 3 * gw
    nb = l // QB
    view = view2.reshape(b, l, dil * 3 * gw)
    bc, bp = _a_bias_tables(dil, slopes)

    def spec(which, prev):
        def imap(bi, r, j):
            return (bi, jnp.maximum(j - 1, 0) if prev else j, r * 3 + which)
        return pl.BlockSpec((1, QB, gw), imap)

    tab = pl.BlockSpec((A_HEADS * QB, QB), lambda bi, r, j: (0, 0))
    o, lse = pl.pallas_call(
        _attn_a_kernel,
        grid=(b, dil, nb),
        in_specs=[spec(0, False), spec(1, False), spec(1, True), spec(2, False), spec(2, True),
                  tab, tab],
        out_specs=[pl.BlockSpec((1, QB, gw), lambda bi, r, j: (bi, j, r)),
                   pl.BlockSpec((1, QB, gw), lambda bi, r, j: (bi, j, r))],
        out_shape=[jax.ShapeDtypeStruct((b, l, dil * gw), BF16),
                   jax.ShapeDtypeStruct((b, l, dil * gw), F32)],
        compiler_params=_cparams(3),
        name=f"attn_a_d{dil}",
    )(view, view, view, view, view, bc, bp)
    return o.reshape(b * l, dil * gw), lse.reshape(b * l, dil * gw)


def _mix_oproj_kernel(x_ref, *rest, dils, gw):
    ng = len(dils)
    o_refs, l_refs = rest[:ng], rest[ng:2 * ng]
    w_ref, out_ref, o_scr, l_scr = rest[2 * ng:]
    tm = x_ref.shape[0]
    os_, ls = [], []
    for gi, dil in enumerate(dils):
        if dil == 1:
            os_.append(o_refs[gi][...].astype(F32))
            ls.append(l_refs[gi][...])
        else:
            nct = gw // LANES
            for r in range(dil):
                rows = pl.ds(r, tm // dil, stride=dil)
                for c in range(nct):
                    cols = slice(r * gw + c * LANES, r * gw + (c + 1) * LANES)
                    o_scr[gi, c, rows, :] = o_refs[gi][:, cols].astype(F32)
                    l_scr[gi, c, rows, :] = l_refs[gi][:, cols]
            os_.append(jnp.concatenate([o_scr[gi, c] for c in range(nct)], axis=1))
            ls.append(jnp.concatenate([l_scr[gi, c] for c in range(nct)], axis=1))
    m = functools.reduce(jnp.maximum, ls)
    es = [jnp.exp(v - m) for v in ls]
    den = functools.reduce(lambda a, c: a + c, es)
    mixed = functools.reduce(lambda a, c: a + c, [(e / den) * o for e, o in zip(es, os_)])
    out_ref[...] = x_ref[...] + _dot(mixed.astype(BF16), w_ref[...])


def mix_oproj_res(x2, outs, lses, w, dils, *, tm=1024):
    n, d = x2.shape
    gw = w.shape[0]
    ng = len(dils)
    views = [pl.BlockSpec((tm // dil, dil * gw), lambda i: (i, 0)) for dil in dils]
    return pl.pallas_call(
        functools.partial(_mix_oproj_kernel, dils=dils, gw=gw),
        grid=(n // tm,),
        in_specs=[pl.BlockSpec((tm, d), lambda i: (i, 0))] + views + views
                 + [pl.BlockSpec((gw, d), lambda i: (0, 0))],
        out_specs=pl.BlockSpec((tm, d), lambda i: (i, 0)),
        out_shape=jax.ShapeDtypeStruct((n, d), F32),
        scratch_shapes=[pltpu.VMEM((ng, gw // LANES, tm, LANES), F32)] * 2,
        compiler_params=_cparams(1),
        name="mix_oproj_res",
    )(x2, *outs, *lses, w)


def _softmax_update(s, v, m_ref, l_ref, acc_ref, idx):
    tk = s.shape[1]
    m_prev = m_ref[idx]
    m_next = jnp.maximum(m_prev, jnp.max(s, axis=1, keepdims=True))
    p = jnp.exp2(s - _lane_tile(m_next, tk // LANES))
    alpha = jnp.exp2(m_prev - m_next)
    if l_ref is not None:
        l_ref[idx] = alpha * l_ref[idx] + jnp.sum(p, axis=1, keepdims=True)
    acc_ref[idx] = alpha * acc_ref[idx] + _dot(p.astype(BF16), v)
    m_ref[idx] = m_next


def _init_softmax_state(m_ref, l_ref, acc_ref):
    m_ref[...] = jnp.full(m_ref.shape, NEG_INF, F32)
    if l_ref is not None:
        l_ref[...] = jnp.zeros(l_ref.shape, F32)
    acc_ref[...] = jnp.zeros(acc_ref.shape, F32)


def _causal_sweep(qi, t, n_chain, scores, update, s_ref):
    def qk(kb, buf):
        for c in range(n_chain):
            s_ref[buf, c] = scores(kb, c)

    def upd(kb, buf, masked):
        for c in range(n_chain):
            s = s_ref[buf, c]
            if masked:
                s = jnp.where(_causal_mask(t), s, NEG_INF)
            update(s, kb, c)

    qk(0, 0)

    def body(i, carry):
        kb = 2 * i
        qk(kb + 1, 1)
        upd(kb, 0, False)
        qk(kb + 2, 0)
        upd(kb + 1, 1, False)
        return carry

    lax.fori_loop(0, lax.shift_right_logical(qi, 1), body, 0)
    odd = jnp.bitwise_and(qi, 1) == 1

    @pl.when(odd)
    def _():
        qk(qi, 1)
        upd(qi - 1, 0, False)
        upd(qi, 1, True)

    @pl.when(jnp.logical_not(odd))
    def _():
        upd(qi, 0, True)


def _split3(x):
    hi = x.astype(BF16).astype(F32)
    mid = (x - hi).astype(BF16).astype(F32)
    lo = (x - hi - mid).astype(BF16).astype(F32)
    return hi, mid, lo


def _causal_mask(t):
    return lax.broadcasted_iota(I32, (t, t), 0) >= lax.broadcasted_iota(I32, (t, t), 1)


def _attn_b_kernel(q_ref, k_ref, v_ref, lq1, lk1, lq2, lk2, g_ref, o_ref, kb_ref, s_ref, m_ref,
                   l_ref, acc_ref, *, t, slopes, lam_init):
    h = pl.program_id(1)
    qi = pl.program_id(2)
    n_bias = 3

    @pl.when(qi == 0)
    def _():
        slope = jnp.float32(0.0)
        for hh, sv in enumerate(slopes):
            slope = jnp.where(h == hh, jnp.float32(sv), slope)
        kf = k_ref[0].astype(F32)
        lane = lax.broadcasted_iota(I32, kf.shape, 1)
        parts = _split3(slope * lax.broadcasted_iota(I32, kf.shape, 0).astype(F32))

        def bias_lanes(base):
            out = jnp.zeros_like(kf)
            for j, part in enumerate(parts):
                out = jnp.where(lane == base + j, part, out)
            return out

        kb_ref[0] = jnp.where(lane < HEAD_DIM, kf, bias_lanes(HEAD_DIM)).astype(BF16)
        kb_ref[1] = jnp.where(lane >= HEAD_DIM, kf, bias_lanes(0)).astype(BF16)

    q = q_ref[0].astype(F32)
    lane = lax.broadcasted_iota(I32, (t, LANES), 1)
    ones_hi = jnp.where(lane < HEAD_DIM + n_bias, 1.0, 0.0)
    ones_lo = jnp.where(lane < n_bias, 1.0, 0.0)
    qs = (jnp.where(lane < HEAD_DIM, q, ones_hi).astype(BF16),
          jnp.where(lane >= HEAD_DIM, q, ones_lo).astype(BF16))
    _init_softmax_state(m_ref, l_ref, acc_ref)

    def rows(kb):
        return pl.ds(pl.multiple_of(kb * t, t), t)

    def scores(kb, mi):
        return _dot_nt(qs[mi], kb_ref[mi, rows(kb), :])

    def update(s, kb, mi):
        _softmax_update(s, v_ref[0, rows(kb), :], m_ref, l_ref, acc_ref, mi)

    _causal_sweep(qi, t, 2, scores, update, s_ref)

    lam = (jnp.exp(jnp.sum(lq1[...] * lk1[...], axis=1, keepdims=True))
           - jnp.exp(jnp.sum(lq2[...] * lk2[...], axis=1, keepdims=True)) + lam_init)
    o = acc_ref[0] / l_ref[0] - lam * (acc_ref[1] / l_ref[1])
    o = _rms(o, g_ref[...], SUBLN_EPS) * (1.0 - lam_init)
    o_ref[0] = o.astype(o_ref.dtype)


def attn_b(qkv3, lq1, lk1, lq2, lk2, subln, layer, *, t=512):
    b, s, width = qkv3.shape
    nh = width // (3 * LANES)
    slopes = tuple(float(2.0 ** (-8.0 * (i + 1) / nh)) * LOG2E for i in range(nh))
    lam_init = 0.8 - 0.6 * math.exp(-0.3 * layer)
    vec = pl.BlockSpec((1, HEAD_DIM), lambda bi, h, qi: (0, 0))
    return pl.pallas_call(
        functools.partial(_attn_b_kernel, t=t, slopes=slopes, lam_init=lam_init),
        grid=(b, nh, s // t),
        in_specs=[pl.BlockSpec((1, t, LANES), lambda bi, h, qi: (bi, qi, h)),
                  pl.BlockSpec((1, s, LANES), lambda bi, h, qi: (bi, 0, nh + h)),
                  pl.BlockSpec((1, s, LANES), lambda bi, h, qi: (bi, 0, 2 * nh + h)),
                  vec, vec, vec, vec,
                  pl.BlockSpec((1, LANES), lambda bi, h, qi: (0, 0))],
        out_specs=pl.BlockSpec((1, t, LANES), lambda bi, h, qi: (bi, qi, h)),
        out_shape=jax.ShapeDtypeStruct((b, s, nh * LANES), BF16),
        scratch_shapes=[pltpu.VMEM((2, s, LANES), BF16), pltpu.VMEM((2, 2, t, t), F32)]
                       + [pltpu.VMEM((2, t, LANES), F32)] * 3,
        compiler_params=_cparams(3),
        name="attn_b",
    )(qkv3, qkv3, qkv3, lq1.reshape(1, -1), lk1.reshape(1, -1), lq2.reshape(1, -1),
      lk2.reshape(1, -1), subln.reshape(1, -1))


MLA_HEADS = 16
MLA_Q_RANK = 384
MLA_KV_RANK = 256
MLA_ROPE = 32


def _c_proj_kernel(x_ref, g_ref, wq_ref, wkv_ref, wpe_ref, qn_ref, kvn_ref, wqu_ref, wkvu_ref,
                   cos_ref, sm_ref, sp_ref, q_out, kv_out, pe_out):
    xn = _rms(x_ref[...], g_ref[...], NORM_EPS).astype(BF16)
    q_lat = _dot(xn, wq_ref[...])
    kv_lat = _dot(xn, wkv_ref[...])
    cos, sm, sp = cos_ref[...], sm_ref[...], sp_ref[...]
    pw = cos.shape[1]

    def rope(v):
        return v * cos + pltpu.roll(v, pw - MLA_ROPE // 2, 1) * sm + pltpu.roll(v, MLA_ROPE // 2, 1) * sp

    pe_out[...] = rope(_dot(xn, wpe_ref[...])).astype(pe_out.dtype)
    qn = _rms(q_lat, qn_ref[...], NORM_EPS).astype(BF16)
    for p in range(q_out.shape[1] // pw):
        sl = slice(p * pw, (p + 1) * pw)
        q_out[:, sl] = rope(_dot(qn, wqu_ref[:, sl])).astype(q_out.dtype)
    kvn = _rms(kv_lat, kvn_ref[...], NORM_EPS).astype(BF16)
    for p in range(kv_out.shape[1] // pw):
        sl = slice(p * pw, (p + 1) * pw)
        kv_out[:, sl] = _dot(kvn, wkvu_ref[:, sl]).astype(kv_out.dtype)


def _mla_layout():
    qcols = -np.ones(MLA_HEADS * LANES, np.int64)
    kvcols = np.zeros(MLA_HEADS * LANES, np.int64)
    for h in range(MLA_HEADS):
        qsrc = h * (HEAD_DIM + MLA_ROPE)
        ksrc = h * 2 * HEAD_DIM
        base = h * LANES
        if h % 2 == 0:
            qcols[base:base + MLA_ROPE] = qsrc + HEAD_DIM + np.arange(MLA_ROPE)
            qcols[base + HEAD_DIM:base + LANES] = qsrc + np.arange(HEAD_DIM)
            kvcols[base:base + HEAD_DIM] = ksrc + HEAD_DIM + np.arange(HEAD_DIM)
            kvcols[base + HEAD_DIM:base + LANES] = ksrc + np.arange(HEAD_DIM)
        else:
            qcols[base:base + HEAD_DIM] = qsrc + np.arange(HEAD_DIM)
            qcols[base + HEAD_DIM:base + HEAD_DIM + MLA_ROPE] = qsrc + HEAD_DIM + np.arange(MLA_ROPE)
            kvcols[base:base + HEAD_DIM] = ksrc + np.arange(HEAD_DIM)
            kvcols[base + HEAD_DIM:base + LANES] = ksrc + HEAD_DIM + np.arange(HEAD_DIM)
    pe_offsets = (0, LANES + HEAD_DIM)
    return qcols, kvcols, pe_offsets


def _rope_tables(s, pe_offsets):
    half = MLA_ROPE // 2
    inv_freq = ROPE_THETA ** (-jnp.arange(half, dtype=F32) / half)
    ang = jnp.arange(s, dtype=F32)[:, None] * inv_freq[None, :]
    cos, sin = jnp.cos(ang), jnp.sin(ang)
    c = jnp.ones((s, 2 * LANES), F32)
    sm = jnp.zeros((s, 2 * LANES), F32)
    sp = jnp.zeros((s, 2 * LANES), F32)
    for off in pe_offsets:
        c = c.at[:, off:off + half].set(cos).at[:, off + half:off + 2 * half].set(cos)
        sm = sm.at[:, off:off + half].set(-sin)
        sp = sp.at[:, off + half:off + 2 * half].set(sin)
    return c, sm, sp


def c_proj(x2, s, g, w_in, q_norm, w_q_up, kv_norm, w_kv_up, *, tm=512):
    n, d = x2.shape
    qcols, kvcols, pe_offsets = _mla_layout()
    wq = w_in[:, :MLA_Q_RANK].astype(BF16)
    wkv = w_in[:, MLA_Q_RANK:MLA_Q_RANK + MLA_KV_RANK].astype(BF16)
    w_pe = w_in[:, MLA_Q_RANK + MLA_KV_RANK:]
    wpe = jnp.zeros((d, 2 * LANES), F32)
    for off in pe_offsets:
        wpe = wpe.at[:, off:off + MLA_ROPE].set(w_pe)
    wpe = wpe.astype(BF16)
    q_scale = (HEAD_DIM + MLA_ROPE) ** -0.5 * LOG2E
    wqu = jnp.where(jnp.asarray(qcols >= 0)[None, :],
                    w_q_up[:, np.maximum(qcols, 0)] * q_scale, 0.0).astype(BF16)
    wkvu = w_kv_up[:, kvcols].astype(BF16)
    cos, sm, sp = _rope_tables(s, pe_offsets)
    width = MLA_HEADS * LANES
    assert s % tm == 0
    nsb = s // tm
    full = lambda shape: pl.BlockSpec(shape, lambda i: (0, 0))
    tab = pl.BlockSpec((tm, 2 * LANES), lambda i: (i % nsb, 0))
    return pl.pallas_call(
        _c_proj_kernel,
        grid=(n // tm,),
        in_specs=[pl.BlockSpec((tm, d), lambda i: (i, 0)), full((1, d)),
                  full((d, MLA_Q_RANK)), full((d, MLA_KV_RANK)), full((d, 2 * LANES)),
                  full((1, MLA_Q_RANK)), full((1, MLA_KV_RANK)),
                  full((MLA_Q_RANK, width)), full((MLA_KV_RANK, width)), tab, tab, tab],
        out_specs=[pl.BlockSpec((tm, width), lambda i: (i, 0)),
                   pl.BlockSpec((tm, width), lambda i: (i, 0)),
                   pl.BlockSpec((tm, 2 * LANES), lambda i: (i, 0))],
        out_shape=[jax.ShapeDtypeStruct((n, width), BF16),
                   jax.ShapeDtypeStruct((n, width), BF16),
                   jax.ShapeDtypeStruct((n, 2 * LANES), BF16)],
        compiler_params=_cparams(1),
        name="c_proj",
    )(x2, g.reshape(1, d), wq, wkv, wpe, q_norm.reshape(1, -1), kv_norm.reshape(1, -1),
      wqu, wkvu, cos, sm, sp)


def _attn_c_kernel(q_ref, kv_ref, pe_ref, o_ref, kcat_ref, vcat_ref, s_ref, m_ref, acc_ref, *, t):
    qi = pl.program_id(2)
    pw = 2 * LANES

    @pl.when(qi == 0)
    def _():
        lane = lax.broadcasted_iota(I32, kcat_ref.shape, 1)
        is_v = jnp.logical_or(lane < HEAD_DIM, lane >= pw - HEAD_DIM)
        kv = kv_ref[0]
        kcat_ref[...] = jnp.where(is_v, pe_ref[0], kv)
        vcat_ref[...] = jnp.where(is_v, kv, jnp.ones_like(kv))

    _init_softmax_state(m_ref, None, acc_ref)

    def rows(kb):
        return pl.ds(pl.multiple_of(kb * t, t), t)

    def lanes(jh):
        return slice(jh * LANES, (jh + 1) * LANES)

    def scores(kb, jh):
        return _dot_nt(q_ref[0, :, lanes(jh)], kcat_ref[rows(kb), lanes(jh)])

    def update(s, kb, jh):
        _softmax_update(s, vcat_ref[rows(kb), lanes(jh)], m_ref, None, acc_ref, jh)

    _causal_sweep(qi, t, 2, scores, update, s_ref)
    lo = lax.broadcasted_iota(I32, (t, LANES), 1) < HEAD_DIM
    outs = [acc_ref[jh] / pltpu.roll(acc_ref[jh], HEAD_DIM, 1) for jh in range(2)]
    o_ref[0] = jnp.where(lo, outs[0], outs[1]).astype(o_ref.dtype)


def attn_c(q3, kv3, pe3, *, t=512):
    b, s, width = q3.shape
    npair = width // (2 * LANES)
    return pl.pallas_call(
        functools.partial(_attn_c_kernel, t=t),
        grid=(b, npair, s // t),
        in_specs=[pl.BlockSpec((1, t, 2 * LANES), lambda bi, p, qi: (bi, qi, p)),
                  pl.BlockSpec((1, s, 2 * LANES), lambda bi, p, qi: (bi, 0, p)),
                  pl.BlockSpec((1, s, 2 * LANES), lambda bi, p, qi: (bi, 0, 0))],
        out_specs=pl.BlockSpec((1, t, LANES), lambda bi, p, qi: (bi, qi, p)),
        out_shape=jax.ShapeDtypeStruct((b, s, npair * LANES), BF16),
        scratch_shapes=[pltpu.VMEM((s, 2 * LANES), BF16)] * 2 + [pltpu.VMEM((2, 2, t, t), F32)]
                       + [pltpu.VMEM((2, t, LANES), F32)] * 2,
        compiler_params=_cparams(3),
        name="attn_c",
    )(q3, kv3, pe3)


def _attn_d_kernel(q_ref, k_ref, v_ref, o_ref, r_ref, acc_ref, *, t, scale):
    qi = pl.program_id(2)
    q = q_ref[0] * scale
    lo = lax.broadcasted_iota(I32, (t, LANES), 1) < HEAD_DIM
    zero = jnp.zeros_like(q)
    qs = (jnp.where(lo, q, zero), jnp.where(lo, zero, q))
    r_ref[...] = jnp.zeros(r_ref.shape, F32)
    acc_ref[...] = jnp.zeros(acc_ref.shape, F32)
    row = lax.broadcasted_iota(I32, (2 * t, t), 0)
    col = lax.broadcasted_iota(I32, (2 * t, t), 1)
    ones_ge = jnp.where(jnp.where(row >= t, row - t, row) >= col, 1.0, 0.0).astype(BF16)
    strict = lax.broadcasted_iota(I32, (t, t), 0) > lax.broadcasted_iota(I32, (t, t), 1)

    def block(kb, masked):
        off = pl.multiple_of(kb * t, t)
        k = k_ref[0, pl.ds(off, t), :]
        v = v_ref[0, pl.ds(off, t), :]
        for jh in range(2):
            z = _dot_nt(qs[jh], k)
            lg = jnp.log(1.0 + jnp.exp(-jnp.abs(z)))
            lk = jnp.minimum(-z, 0.0) - lg
            if masked:
                lk = jnp.where(strict, lk, 0.0)
            hi = lk.astype(BF16)
            lw = (lk - hi.astype(F32)).astype(BF16)
            csum = _dot(jnp.concatenate([hi, lw], axis=1), ones_ge)
            r_prev = r_ref[jh]
            tail = _lane_tile(r_prev, t // LANES) + (csum - lk)
            a = jnp.exp((jnp.minimum(z, 0.0) - lg) + tail)
            if masked:
                a = jnp.where(strict, a, 0.0)
            acc_ref[jh] = acc_ref[jh] + _dot(a.astype(BF16), v)
            r_ref[jh] = r_prev + csum[:, 0:1]

    def live():
        return jnp.max(jnp.maximum(r_ref[0], r_ref[1])) >= EXP_UNDERFLOW

    def cond(carry):
        it, alive = carry
        return jnp.logical_and(it < qi, alive)

    def body(carry):
        it, _ = carry
        block(qi - 1 - it, False)
        return it + 1, live()

    @pl.when(qi == 0)
    def _():
        block(qi, True)

    @pl.when(qi > 0)
    def _():
        block(qi, True)
        block(qi - 1, False)
        lax.while_loop(cond, body, (jnp.int32(1), live()))

    o_ref[0] = jnp.where(lo, acc_ref[0], acc_ref[1]).astype(o_ref.dtype)


def attn_d(qkv3, *, t=256):
    b, s, width = qkv3.shape
    npair = width // (3 * LANES)
    return pl.pallas_call(
        functools.partial(_attn_d_kernel, t=t, scale=HEAD_DIM ** -0.5),
        grid=(b, npair, s // t),
        in_specs=[pl.BlockSpec((1, t, LANES), lambda bi, p, qi: (bi, qi, p)),
                  pl.BlockSpec((1, s, LANES), lambda bi, p, qi: (bi, 0, npair + p)),
                  pl.BlockSpec((1, s, LANES), lambda bi, p, qi: (bi, 0, 2 * npair + p))],
        out_specs=pl.BlockSpec((1, t, LANES), lambda bi, p, qi: (bi, qi, p)),
        out_shape=jax.ShapeDtypeStruct((b, s, npair * LANES), BF16),
        scratch_shapes=[pltpu.VMEM((2, t, LANES), F32)] * 2,
        compiler_params=_cparams(3),
        name="attn_d",
    )(qkv3, qkv3, qkv3)


def _silu(x):
    return x / (1.0 + jnp.exp(-x))


def _ffn_kernel(x_ref, g_ref, wg_ref, wu_ref, wd_ref, out_ref, *, ff_chunk):
    x = x_ref[...]
    xn = _rms(x, g_ref[...], NORM_EPS).astype(BF16)
    y = x
    for c in range(0, wg_ref.shape[1], ff_chunk):
        h = _silu(_dot(xn, wg_ref[:, c:c + ff_chunk])) * _dot(xn, wu_ref[:, c:c + ff_chunk])
        y = y + _dot(h.astype(BF16), wd_ref[c:c + ff_chunk, :])
    out_ref[...] = y


def ffn(x2, g, wg, wu, wd, *, tm=512):
    n, d = x2.shape
    ff = wg.shape[1]
    ff_chunk = ff // 2 if (ff // 2) % LANES == 0 else ff
    const = lambda shape: pl.BlockSpec(shape, lambda i: (0, 0), pipeline_mode=pl.Buffered(1))
    return pl.pallas_call(
        functools.partial(_ffn_kernel, ff_chunk=ff_chunk),
        grid=(n // tm,),
        in_specs=[pl.BlockSpec((tm, d), lambda i: (i, 0)),
                  pl.BlockSpec((1, d), lambda i: (0, 0)),
                  const((d, ff)), const((d, ff)), const((ff, d))],
        out_specs=pl.BlockSpec((tm, d), lambda i: (i, 0)),
        out_shape=jax.ShapeDtypeStruct((n, d), F32),
        compiler_params=_cparams(1),
        name="ffn",
    )(x2, g.reshape(1, d), wg, wu, wd)


def _router_kernel(x_ref, g_ref, wr_ref, meta_ref, gate_ref, cnt_ref, run_ref):
    i = pl.program_id(0)
    tm = x_ref.shape[0]

    @pl.when(i == 0)
    def _():
        run_ref[...] = jnp.zeros(run_ref.shape, F32)

    xn = _rms(x_ref[...], g_ref[...], NORM_EPS)
    logits = jnp.dot(xn, wr_ref[...], preferred_element_type=F32, precision=lax.Precision.HIGHEST)
    lane = lax.broadcasted_iota(I32, (tm, LANES), 1)
    lg = jnp.where(lane < N_EXPERTS, logits, NEG_INF)
    m1 = jnp.max(lg, axis=1, keepdims=True)
    e1 = jnp.min(jnp.where(lg == m1, lane, LANES), axis=1, keepdims=True)
    lg2 = jnp.where(lane == e1, NEG_INF, lg)
    m2 = jnp.max(lg2, axis=1, keepdims=True)
    e2 = jnp.min(jnp.where(lg2 == m2, lane, LANES), axis=1, keepdims=True)
    ex = jnp.exp(m2 - m1)
    g1 = 1.0 / (1.0 + ex)
    g2 = ex / (1.0 + ex)
    hit1 = lane == e1
    hit2 = lane == e2
    onehot = jnp.where(jnp.logical_or(hit1, hit2), 1.0, 0.0)
    earlier = (lax.broadcasted_iota(I32, (tm, tm), 0) > lax.broadcasted_iota(I32, (tm, tm), 1))
    before = _dot(jnp.where(earlier, 1.0, 0.0).astype(BF16), onehot.astype(BF16)) + run_ref[0:1, :]
    pos1 = jnp.sum(jnp.where(hit1, before, 0.0), axis=1, keepdims=True).astype(I32)
    pos2 = jnp.sum(jnp.where(hit2, before, 0.0), axis=1, keepdims=True).astype(I32)
    run_ref[...] = run_ref[...] + jnp.sum(onehot, axis=0, keepdims=True)
    meta_ref[...] = jnp.where(lane == 0, e1, jnp.where(lane == 1, e2, jnp.where(
        lane == 2, pos1, jnp.where(lane == 3, pos2, 0))))
    gate_ref[...] = jnp.where(lane == 0, g1, jnp.where(lane == 1, g2, 0.0))
    cnt_ref[...] = run_ref[...]


def router(x2, g, w_router, *, tm=512):
    n, d = x2.shape
    wr = jnp.zeros((d, LANES), F32).at[:, :N_EXPERTS].set(w_router)
    row = lambda w: pl.BlockSpec((tm, w), lambda i: (i, 0))
    return pl.pallas_call(
        _router_kernel,
        grid=(n // tm,),
        in_specs=[row(d), pl.BlockSpec((1, d), lambda i: (0, 0)),
                  pl.BlockSpec((d, LANES), lambda i: (0, 0))],
        out_specs=[row(LANES), row(LANES), pl.BlockSpec((8, LANES), lambda i: (0, 0))],
        out_shape=[jax.ShapeDtypeStruct((n, LANES), I32),
                   jax.ShapeDtypeStruct((n, LANES), F32), jax.ShapeDtypeStruct((8, LANES), F32)],
        scratch_shapes=[pltpu.VMEM((8, LANES), F32)],
        compiler_params=_cparams(1),
        name="router",
    )(x2, g.reshape(1, d), wr)


def _row_copy(src_ref, src_row, dst_ref, dst_row, sem):
    return pltpu.make_async_copy(src_ref.at[pl.ds(src_row, 1)], dst_ref.at[pl.ds(dst_row, 1)], sem)


def _dispatch_kernel(dest_ref, x_ref, g_ref, xb_in_ref, xb_ref, xn_ref, sem, *, tb):
    del xb_in_ref
    xn_ref[...] = _rms(x_ref[...], g_ref[...], NORM_EPS)

    def issue(tt, carry):
        for k in range(2):
            _row_copy(xn_ref, tt, xb_ref, dest_ref[2 * tt + k], sem).start()
        return carry

    def drain(tt, carry):
        for k in range(2):
            _row_copy(xn_ref, 0, xb_ref, 0, sem).wait()
        return carry

    lax.fori_loop(0, tb, issue, 0)
    lax.fori_loop(0, tb, drain, 0)


def dispatch(dest_flat, x2, g, n_rows, *, tb=512):
    n, d = x2.shape
    xb0 = jnp.zeros((n_rows, d), F32)
    return pl.pallas_call(
        functools.partial(_dispatch_kernel, tb=tb),
        grid=(n // tb,),
        in_specs=[pl.BlockSpec((2 * tb,), lambda i: (i,), memory_space=pltpu.SMEM),
                  pl.BlockSpec((tb, d), lambda i: (i, 0)),
                  pl.BlockSpec((1, d), lambda i: (0, 0)),
                  pl.BlockSpec(memory_space=pl.ANY)],
        out_specs=pl.BlockSpec(memory_space=pl.ANY),
        out_shape=jax.ShapeDtypeStruct((n_rows, d), F32),
        scratch_shapes=[pltpu.VMEM((tb, d), F32), pltpu.SemaphoreType.DMA(())],
        input_output_aliases={3: 0},
        compiler_params=_cparams(1),
        name="moe_dispatch",
    )(dest_flat, x2, g.reshape(1, d), xb0)


def _expert_kernel(be_ref, nu_ref, x_ref, wg_ref, wu_ref, wd_ref, y_ref, *, ff_chunk):
    i = pl.program_id(0)

    @pl.when(i < nu_ref[0])
    def _():
        xb = x_ref[...].astype(BF16)
        y = jnp.zeros(y_ref.shape, F32)
        for c in range(0, wg_ref.shape[2], ff_chunk):
            h = (_silu(_dot(xb, wg_ref[0, :, c:c + ff_chunk]))
                 * _dot(xb, wu_ref[0, :, c:c + ff_chunk]))
            y = y + _dot(h.astype(BF16), wd_ref[0, c:c + ff_chunk, :])
        y_ref[...] = y

    @pl.when(i >= nu_ref[0])
    def _():
        y_ref[...] = jnp.zeros(y_ref.shape, F32)


def experts(block_e, n_used, xb, wg, wu, wd, *, ff_chunk=512):
    rows, d = xb.shape
    ff = wg.shape[2]
    assert ff % ff_chunk == 0
    wspec = lambda shape: pl.BlockSpec(shape, lambda i, be, nu: (be[i], 0, 0),
                                       pipeline_mode=pl.Buffered(1))
    return pl.pallas_call(
        functools.partial(_expert_kernel, ff_chunk=ff_chunk),
        grid_spec=pltpu.PrefetchScalarGridSpec(
            num_scalar_prefetch=2,
            grid=(rows // MOE_ROWS,),
            in_specs=[pl.BlockSpec((MOE_ROWS, d), lambda i, be, nu: (i, 0)),
                      wspec((1, d, ff)), wspec((1, d, ff)), wspec((1, ff, d))],
            out_specs=pl.BlockSpec((MOE_ROWS, d), lambda i, be, nu: (i, 0)),
        ),
        out_shape=jax.ShapeDtypeStruct((rows, d), F32),
        compiler_params=_cparams(1),
        name="moe_experts",
    )(block_e, n_used, xb, wg, wu, wd)


def _combine_kernel(dest_ref, x_ref, gate_ref, fg_ref, yb_ref, out_ref, buf_ref, sem, *, tb, final):
    def issue(tt, carry):
        for k in range(2):
            _row_copy(yb_ref, dest_ref[2 * tt + k], buf_ref.at[k], tt, sem).start()
        return carry

    def drain(tt, carry):
        for k in range(2):
            _row_copy(yb_ref, 0, buf_ref.at[k], 0, sem).wait()
        return carry

    lax.fori_loop(0, tb, issue, 0)
    lax.fori_loop(0, tb, drain, 0)
    gate = gate_ref[...]
    y = x_ref[...] + gate[:, 0:1] * buf_ref[0] + gate[:, 1:2] * buf_ref[1]
    if final:
        y = _rms(y, fg_ref[...], NORM_EPS)
    out_ref[...] = y


def combine(dest_flat, x2, gates, yb, final_g, *, final, tb=512):
    n, d = x2.shape
    return pl.pallas_call(
        functools.partial(_combine_kernel, tb=tb, final=final),
        grid=(n // tb,),
        in_specs=[pl.BlockSpec((2 * tb,), lambda i: (i,), memory_space=pltpu.SMEM),
                  pl.BlockSpec((tb, d), lambda i: (i, 0)),
                  pl.BlockSpec((tb, LANES), lambda i: (i, 0)),
                  pl.BlockSpec((1, d), lambda i: (0, 0)),
                  pl.BlockSpec(memory_space=pl.ANY)],
        out_specs=pl.BlockSpec((tb, d), lambda i: (i, 0)),
        out_shape=jax.ShapeDtypeStruct((n, d), F32),
        scratch_shapes=[pltpu.VMEM((2, tb, d), F32), pltpu.SemaphoreType.DMA(())],
        compiler_params=_cparams(1),
        name="moe_combine",
    )(dest_flat, x2, gates, final_g.reshape(1, d), yb)


def moe(x2, g, w_router, wg, wu, wd, final_g, *, final):
    n, d = x2.shape
    meta, gates, cnt = router(x2, g, w_router)
    counts = cnt[0, :N_EXPERTS].astype(I32)
    padded = (counts + MOE_ROWS - 1) // MOE_ROWS * MOE_ROWS
    pad_end = jnp.cumsum(padded)
    pad_start = pad_end - padded
    dest = (pad_start[meta[:, 0:2]] + meta[:, 2:4]).reshape(-1).astype(I32)
    n_blocks = -(-2 * n // MOE_ROWS) + N_EXPERTS
    starts = jnp.arange(n_blocks, dtype=I32) * MOE_ROWS
    block_e = jnp.minimum(jnp.sum((pad_end[None, :] <= starts[:, None]).astype(I32), axis=1),
                          N_EXPERTS - 1)
    n_used = (pad_end[-1:] // MOE_ROWS).astype(I32)
    xb = dispatch(dest, x2, g, n_blocks * MOE_ROWS)
    yb = experts(block_e, n_used, xb, wg.astype(BF16), wu.astype(BF16), wd.astype(BF16))
    return combine(dest, x2, gates, yb, final_g, final=final)


def _bf16(w):
    return w.astype(BF16)


def mixer_a(x2, b, norm, w_qkv, w_o):
    d = x2.shape[1]
    n_groups = len(A_PATTERN)
    dils = tuple(dil for _, dil in A_PATTERN)
    gw = A_HEADS * HEAD_DIM
    w = w_qkv.reshape(d, 3, n_groups, gw)
    w = w * jnp.asarray([HEAD_DIM ** -0.5 * LOG2E, 1.0, 1.0], F32).reshape(1, 3, 1, 1)
    w = _bf16(w.transpose(0, 2, 1, 3).reshape(d, n_groups * 3 * gw))
    views = a_proj(x2, norm, w, dils)
    slopes = 2.0 ** (-8.0 * np.arange(1, n_groups * A_HEADS + 1) / (n_groups * A_HEADS))
    outs, lses = [], []
    for gi, (window, dil) in enumerate(A_PATTERN):
        assert window // dil == QB
        o, lse = attn_a_group(views[gi], b, dil, slopes[gi * A_HEADS:(gi + 1) * A_HEADS])
        outs.append(o)
        lses.append(lse)
    return mix_oproj_res(x2, outs, lses, _bf16(w_o), dils)


def mixer_b(x2, b, norm, w_qkv, lq1, lk1, lq2, lk2, subln, w_o, *, layer, t=512):
    n = x2.shape[0]
    n_q = w_qkv.shape[1] // 3
    w = jnp.concatenate([w_qkv[:, :n_q] * (HEAD_DIM ** -0.5 * LOG2E), w_qkv[:, n_q:]], axis=1)
    qkv = norm_proj(x2, norm, _bf16(w)).reshape(b, n // b, -1)
    o = attn_b(qkv, lq1, lk1, lq2, lk2, subln, layer, t=t)
    return oproj_res(x2, o.reshape(n, -1), _bf16(w_o))


def mixer_c(x2, b, norm, w_in, q_norm, w_q_up, kv_norm, w_kv_up, w_o, *, t=512):
    n = x2.shape[0]
    s = n // b
    q, kv, pe = c_proj(x2, s, norm, w_in, q_norm, w_q_up, kv_norm, w_kv_up)
    o = attn_c(q.reshape(b, s, -1), kv.reshape(b, s, -1), pe.reshape(b, s, -1), t=t)
    return oproj_res(x2, o.reshape(n, -1), _bf16(w_o))


def mixer_d(x2, b, norm, w_qkv, w_o, *, t=256):
    n = x2.shape[0]
    qkv = norm_proj(x2, norm, _bf16(w_qkv)).reshape(b, n // b, -1)
    o = attn_d(qkv, t=t)
    return oproj_res(x2, o.reshape(n, -1), _bf16(w_o))
def kernel(x, a_norm, a_w_qkv, a_w_o, ffn0_norm, ffn0_w_gate, ffn0_w_up, ffn0_w_down, b_norm, b_w_qkv, b_lambda_q1, b_lambda_k1, b_lambda_q2, b_lambda_k2, b_subln, b_w_o, moe1_norm, moe1_w_router, moe1_w_gate, moe1_w_up, moe1_w_down, c_norm, c_w_in, c_q_norm, c_w_q_up, c_kv_norm, c_w_kv_up, c_w_o, ffn2_norm, ffn2_w_gate, ffn2_w_up, ffn2_w_down, d_norm, d_w_qkv, d_w_o, moe3_norm, moe3_w_router, moe3_w_gate, moe3_w_up, moe3_w_down, final_norm):
    b, s, d = x.shape
    x2 = x.reshape(b * s, d)
    bf = _bf16
    x2 = mixer_a(x2, b, a_norm, a_w_qkv, a_w_o)
    x2 = ffn(x2, ffn0_norm, bf(ffn0_w_gate), bf(ffn0_w_up), bf(ffn0_w_down))
    x2 = mixer_b(x2, b, b_norm, b_w_qkv, b_lambda_q1, b_lambda_k1, b_lambda_q2, b_lambda_k2,
                 b_subln, b_w_o, layer=1)
    x2 = moe(x2, moe1_norm, moe1_w_router, moe1_w_gate, moe1_w_up, moe1_w_down, final_norm,
             final=False)
    x2 = mixer_c(x2, b, c_norm, c_w_in, c_q_norm, c_w_q_up, c_kv_norm, c_w_kv_up, c_w_o)
    x2 = ffn(x2, ffn2_norm, bf(ffn2_w_gate), bf(ffn2_w_up), bf(ffn2_w_down))
    x2 = mixer_d(x2, b, d_norm, d_w_qkv, d_w_o)
    x2 = moe(x2, moe3_norm, moe3_w_router, moe3_w_gate, moe3_w_up, moe3_w_down, final_norm,
             final=True)
    return x2.reshape(b, s, d)
```

```python
import functools
import math

import numpy as np
import jax
import jax.numpy as jnp
from jax import lax
from jax.experimental import pallas as pl
from jax.experimental.pallas import tpu as pltpu

F32 = jnp.float32
BF16 = jnp.bfloat16
I32 = jnp.int32

LANES = 128
MXU_DIM = 256
HEAD_DIM = 64
NORM_EPS = 1e-6
SUBLN_EPS = 1e-5
ROPE_THETA = 10000.0
A_PATTERN = ((128, 1), (512, 4), (2048, 16))
A_HEADS = 8
QB = 128
N_EXPERTS = 8
MOE_ROWS = 512
VMEM_LIMIT = 56 * 1024 * 1024
NEG_INF = float("-inf")
EXP_UNDERFLOW = -105.0


def _cparams(n_axes, vmem=VMEM_LIMIT):
    return pltpu.CompilerParams(dimension_semantics=("arbitrary",) * n_axes,
                                vmem_limit_bytes=vmem)


def _dot(a, b):
    return jnp.dot(a, b, preferred_element_type=F32)


def _dot_nt(a, b):
    return lax.dot_general(a, b, (((1,), (1,)), ((), ())), preferred_element_type=F32)


def _rms(x, g, eps):
    return x * lax.rsqrt(jnp.mean(x * x, axis=-1, keepdims=True) + eps) * g


def _lane_tile(x, n):
    return x if n == 1 else jnp.concatenate([x] * n, axis=1)


def _norm_proj_kernel(x_ref, g_ref, w_ref, o_ref, *, col_chunk):
    xn = _rms(x_ref[...], g_ref[...], NORM_EPS).astype(BF16)
    n = o_ref.shape[1]
    for c in range(0, n, col_chunk):
        o_ref[:, c:c + col_chunk] = _dot(xn, w_ref[:, c:c + col_chunk]).astype(o_ref.dtype)


def norm_proj(x2, g, w, *, tm=512, col_chunk=512):
    n, d = x2.shape
    n_out = w.shape[1]
    assert n % tm == 0 and n_out % col_chunk == 0
    return pl.pallas_call(
        functools.partial(_norm_proj_kernel, col_chunk=col_chunk),
        grid=(n // tm,),
        in_specs=[pl.BlockSpec((tm, d), lambda i: (i, 0)),
                  pl.BlockSpec((1, d), lambda i: (0, 0)),
                  pl.BlockSpec((d, n_out), lambda i: (0, 0))],
        out_specs=pl.BlockSpec((tm, n_out), lambda i: (i, 0)),
        out_shape=jax.ShapeDtypeStruct((n, n_out), BF16),
        compiler_params=_cparams(1),
        name="norm_proj",
    )(x2, g.reshape(1, d), w)


def _oproj_kernel(x_ref, o_ref, w_ref, out_ref):
    out_ref[...] = x_ref[...] + _dot(o_ref[...], w_ref[...])


def oproj_res(x2, o2, w, *, tm=1024):
    n, d = x2.shape
    k = o2.shape[1]
    return pl.pallas_call(
        _oproj_kernel,
        grid=(n // tm,),
        in_specs=[pl.BlockSpec((tm, d), lambda i: (i, 0)),
                  pl.BlockSpec((tm, k), lambda i: (i, 0)),
                  pl.BlockSpec((k, d), lambda i: (0, 0))],
        out_specs=pl.BlockSpec((tm, d), lambda i: (i, 0)),
        out_shape=jax.ShapeDtypeStruct((n, d), F32),
        compiler_params=_cparams(1),
        name="oproj_res",
    )(x2, o2, w)


LOG2E = math.log2(math.e)
LN2 = math.log(2.0)


def _a_proj_kernel(x_ref, g_ref, w_ref, *rest, dils, gcols):
    outs, scr = rest[:len(dils)], rest[len(dils)]
    xn = _rms(x_ref[...], g_ref[...], NORM_EPS).astype(BF16)
    tm = x_ref.shape[0]
    for gi, dil in enumerate(dils):
        res = _dot(xn, w_ref[:, gi * gcols:(gi + 1) * gcols])
        if dil == 1:
            outs[gi][...] = res.astype(BF16)
        else:
            for c in range(gcols // LANES):
                scr[c] = res[:, c * LANES:(c + 1) * LANES]
            for r in range(dil):
                for c in range(gcols // LANES):
                    col = r * gcols + c * LANES
                    outs[gi][:, col:col + LANES] = (
                        scr[c, pl.ds(r, tm // dil, stride=dil), :].astype(BF16))


def a_proj(x2, g, w, dils, *, tm=512):
    n, d = x2.shape
    gcols = w.shape[1] // len(dils)
    return pl.pallas_call(
        functools.partial(_a_proj_kernel, dils=dils, gcols=gcols),
        grid=(n // tm,),
        in_specs=[pl.BlockSpec((tm, d), lambda i: (i, 0)),
                  pl.BlockSpec((1, d), lambda i: (0, 0)),
                  pl.BlockSpec((d, w.shape[1]), lambda i: (0, 0))],
        out_specs=[pl.BlockSpec((tm // dil, dil * gcols), lambda i: (i, 0)) for dil in dils],
        out_shape=[jax.ShapeDtypeStruct((n // dil, dil * gcols), BF16) for dil in dils],
        scratch_shapes=[pltpu.VMEM((gcols // LANES, tm, LANES), F32)],
        compiler_params=_cparams(1),
        name="a_proj",
    )(x2, g.reshape(1, d), w)


def _attn_a_kernel(q_ref, kc_ref, kp_ref, vc_ref, vp_ref, bc_ref, bp_ref, o_ref, lse_ref):
    has_prev = pl.program_id(2) > 0
    hw = MXU_DIM
    nh = hw // HEAD_DIM
    head_of_lane = lax.broadcasted_iota(I32, (QB, hw), 1) // HEAD_DIM
    for half in range(q_ref.shape[2] // hw):
        sl = slice(half * hw, (half + 1) * hw)
        tab = slice(half * nh * QB, (half + 1) * nh * QB)
        q = q_ref[0, :, sl]
        zero = jnp.zeros_like(q)
        q_all = jnp.concatenate([jnp.where(head_of_lane == h, q, zero) for h in range(nh)], axis=0)
        sc = _dot_nt(q_all, kc_ref[0, :, sl]) + bc_ref[tab, :]
        sp = jnp.where(has_prev, _dot_nt(q_all, kp_ref[0, :, sl]) + bp_ref[tab, :], NEG_INF)
        m = jnp.maximum(jnp.max(sc, axis=1, keepdims=True), jnp.max(sp, axis=1, keepdims=True))
        pc = jnp.exp2(sc - m)
        pp = jnp.exp2(sp - m)
        l = jnp.sum(pc, axis=1, keepdims=True) + jnp.sum(pp, axis=1, keepdims=True)
        acc = _dot(pc.astype(BF16), vc_ref[0, :, sl]) + _dot(pp.astype(BF16), vp_ref[0, :, sl])
        lse = (m + jnp.log2(l)) * LN2
        o = jnp.zeros((QB, hw), F32)
        ls = jnp.zeros((QB, hw), F32)
        for h in range(nh):
            rows = slice(h * QB, (h + 1) * QB)
            mine = head_of_lane == h
            o = jnp.where(mine, acc[rows] / l[rows], o)
            ls = jnp.where(mine, lse[rows], ls)
        o_ref[0, :, sl] = o.astype(o_ref.dtype)
        lse_ref[0, :, sl] = ls


def _a_bias_tables(dil, slopes):
    rel_c = np.arange(QB)[:, None] - np.arange(QB)[None, :]
    rel_p = rel_c + QB
    sl = np.asarray(slopes, np.float64)[:, None, None] * dil * LOG2E
    bc = np.where(rel_c >= 0, -sl * rel_c, -np.inf).reshape(-1, QB)
    bp = np.where(rel_p <= QB, -sl * rel_p, -np.inf).reshape(-1, QB)
    return jnp.asarray(bc, F32), jnp.asarray(bp, F32)


def attn_a_group(view2, b, dil, slopes):
    gw = A_HEADS * HEAD_DIM
    l = view2.shape[0] // b
    assert l % QB == 0 and view2.shape[1] == dil * 3 * gw
    nb = l // QB
    view = view2.reshape(b, l, dil * 3 * gw)
    bc, bp = _a_bias_tables(dil, slopes)

    def spec(which, prev):
        def imap(bi, r, j):
            return (bi, jnp.maximum(j - 1, 0) if prev else j, r * 3 + which)
        return pl.BlockSpec((1, QB, gw), imap)

    tab = pl.BlockSpec((A_HEADS * QB, QB), lambda bi, r, j: (0, 0))
    o, lse = pl.pallas_call(
        _attn_a_kernel,
        grid=(b, dil, nb),
        in_specs=[spec(0, False), spec(1, False), spec(1, True), spec(2, False), spec(2, True),
                  tab, tab],
        out_specs=[pl.BlockSpec((1, QB, gw), lambda bi, r, j: (bi, j, r)),
                   pl.BlockSpec((1, QB, gw), lambda bi, r, j: (bi, j, r))],
        out_shape=[jax.ShapeDtypeStruct((b, l, dil * gw), BF16),
                   jax.ShapeDtypeStruct((b, l, dil * gw), F32)],
        compiler_params=_cparams(3),
        name=f"attn_a_d{dil}",
    )(view, view, view, view, view, bc, bp)
    return o.reshape(b * l, dil * gw), lse.reshape(b * l, dil * gw)


def _mix_oproj_kernel(x_ref, *rest, dils, gw):
    ng = len(dils)
    o_refs, l_refs = rest[:ng], rest[ng:2 * ng]
    w_ref, out_ref, o_scr, l_scr = rest[2 * ng:]
    tm = x_ref.shape[0]
    os_, ls = [], []
    for gi, dil in enumerate(dils):
        if dil == 1:
            os_.append(o_refs[gi][...].astype(F32))
            ls.append(l_refs[gi][...])
        else:
            nct = gw // LANES
            for r in range(dil):
                rows = pl.ds(r, tm // dil, stride=dil)
                for c in range(nct):
                    cols = slice(r * gw + c * LANES, r * gw + (c + 1) * LANES)
                    o_scr[gi, c, rows, :] = o_refs[gi][:, cols].astype(F32)
                    l_scr[gi, c, rows, :] = l_refs[gi][:, cols]
            os_.append(jnp.concatenate([o_scr[gi, c] for c in range(nct)], axis=1))
            ls.append(jnp.concatenate([l_scr[gi, c] for c in range(nct)], axis=1))
    m = functools.reduce(jnp.maximum, ls)
    es = [jnp.exp(v - m) for v in ls]
    den = functools.reduce(lambda a, c: a + c, es)
    mixed = functools.reduce(lambda a, c: a + c, [(e / den) * o for e, o in zip(es, os_)])
    out_ref[...] = x_ref[...] + _dot(mixed.astype(BF16), w_ref[...])


def mix_oproj_res(x2, outs, lses, w, dils, *, tm=1024):
    n, d = x2.shape
    gw = w.shape[0]
    ng = len(dils)
    views = [pl.BlockSpec((tm // dil, dil * gw), lambda i: (i, 0)) for dil in dils]
    return pl.pallas_call(
        functools.partial(_mix_oproj_kernel, dils=dils, gw=gw),
        grid=(n // tm,),
        in_specs=[pl.BlockSpec((tm, d), lambda i: (i, 0))] + views + views
                 + [pl.BlockSpec((gw, d), lambda i: (0, 0))],
        out_specs=pl.BlockSpec((tm, d), lambda i: (i, 0)),
        out_shape=jax.ShapeDtypeStruct((n, d), F32),
        scratch_shapes=[pltpu.VMEM((ng, gw // LANES, tm, LANES), F32)] * 2,
        compiler_params=_cparams(1),
        name="mix_oproj_res",
    )(x2, *outs, *lses, w)


def _softmax_update(s, v, m_ref, acc_ref, idx):
    tk = s.shape[1]
    m_prev = m_ref[idx]
    m_next = jnp.maximum(m_prev, jnp.max(s, axis=1, keepdims=True))
    p = jnp.exp2(s - _lane_tile(m_next, tk // LANES))
    alpha = jnp.exp2(m_prev - m_next)
    acc_ref[idx] = (_lane_tile(alpha, acc_ref.shape[-1] // LANES) * acc_ref[idx]
                    + _dot(p.astype(BF16), v))
    m_ref[idx] = m_next


def _init_softmax_state(m_ref, acc_ref):
    m_ref[...] = jnp.full(m_ref.shape, NEG_INF, F32)
    acc_ref[...] = jnp.zeros(acc_ref.shape, F32)


def _causal_sweep(qi, t, n_chain, scores, update, s_ref):
    def qk(kb, buf):
        for c in range(n_chain):
            s_ref[buf, c] = scores(kb, c)

    def upd(kb, buf, masked):
        for c in range(n_chain):
            s = s_ref[buf, c]
            if masked:
                s = jnp.where(_causal_mask(s.shape[0], t), s, NEG_INF)
            update(s, kb, c)

    qk(0, 0)

    def body(i, carry):
        kb = 2 * i
        qk(kb + 1, 1)
        upd(kb, 0, False)
        qk(kb + 2, 0)
        upd(kb + 1, 1, False)
        return carry

    lax.fori_loop(0, lax.shift_right_logical(qi, 1), body, 0)
    odd = jnp.bitwise_and(qi, 1) == 1

    @pl.when(odd)
    def _():
        qk(qi, 1)
        upd(qi - 1, 0, False)
        upd(qi, 1, True)

    @pl.when(jnp.logical_not(odd))
    def _():
        upd(qi, 0, True)


def _split3(x):
    hi = x.astype(BF16).astype(F32)
    mid = (x - hi).astype(BF16).astype(F32)
    lo = (x - hi - mid).astype(BF16).astype(F32)
    return hi, mid, lo


def _causal_mask(n_rows, t):
    row = lax.broadcasted_iota(I32, (n_rows, t), 0)
    for _ in range(n_rows // t - 1):
        row = jnp.where(row >= t, row - t, row)
    return row >= lax.broadcasted_iota(I32, (n_rows, t), 1)


def _attn_b_kernel(q_ref, k_ref, v_ref, lq1, lk1, lq2, lk2, g_ref, o_ref, kb_ref, vb_ref, s_ref,
                   m_ref, acc_ref, *, t, slopes, lam_init):
    h = pl.program_id(1)
    qi = pl.program_id(2)
    n_bias = 3

    @pl.when(qi == 0)
    def _():
        slope = jnp.float32(0.0)
        for hh, sv in enumerate(slopes):
            slope = jnp.where(h == hh, jnp.float32(sv), slope)
        shape = (k_ref.shape[1], LANES)
        lane = lax.broadcasted_iota(I32, shape, 1)
        parts = _split3(slope * lax.broadcasted_iota(I32, shape, 0).astype(F32))
        bias = jnp.zeros(shape, F32)
        for j, part in enumerate(parts):
            bias = jnp.where(lane == j, part, bias)
        kb_ref[:, :LANES] = k_ref[0]
        kb_ref[:, LANES:] = bias.astype(BF16)
        vb_ref[:, :LANES] = v_ref[0]
        vb_ref[:, LANES:] = jnp.ones(shape, BF16)

    q = q_ref[0].astype(F32)
    lane = lax.broadcasted_iota(I32, (t, LANES), 1)
    ones = jnp.where(lane < n_bias, 1.0, 0.0)
    q_all = jnp.concatenate(
        [jnp.concatenate([jnp.where(lane < HEAD_DIM, q, 0.0), ones], axis=1),
         jnp.concatenate([jnp.where(lane >= HEAD_DIM, q, 0.0), ones], axis=1)], axis=0).astype(BF16)
    _init_softmax_state(m_ref, acc_ref)

    def rows(kb):
        return pl.ds(pl.multiple_of(kb * t, t), t)

    def scores(kb, c):
        return _dot_nt(q_all, kb_ref[rows(kb), :])

    def update(s, kb, c):
        _softmax_update(s, vb_ref[rows(kb), :], m_ref, acc_ref, c)

    _causal_sweep(qi, t, 1, scores, update, s_ref)

    lam = (jnp.exp(jnp.sum(lq1[...] * lk1[...], axis=1, keepdims=True))
           - jnp.exp(jnp.sum(lq2[...] * lk2[...], axis=1, keepdims=True)) + lam_init)
    acc = acc_ref[0]
    o = acc[:t, :LANES] / acc[:t, LANES:] - lam * (acc[t:, :LANES] / acc[t:, LANES:])
    o = _rms(o, g_ref[...], SUBLN_EPS) * (1.0 - lam_init)
    o_ref[0] = o.astype(o_ref.dtype)


def attn_b(qkv3, lq1, lk1, lq2, lk2, subln, layer, *, t=512):
    b, s, width = qkv3.shape
    nh = width // (3 * LANES)
    slopes = tuple(float(2.0 ** (-8.0 * (i + 1) / nh)) * LOG2E for i in range(nh))
    lam_init = 0.8 - 0.6 * math.exp(-0.3 * layer)
    vec = pl.BlockSpec((1, HEAD_DIM), lambda bi, h, qi: (0, 0))
    return pl.pallas_call(
        functools.partial(_attn_b_kernel, t=t, slopes=slopes, lam_init=lam_init),
        grid=(b, nh, s // t),
        in_specs=[pl.BlockSpec((1, t, LANES), lambda bi, h, qi: (bi, qi, h)),
                  pl.BlockSpec((1, s, LANES), lambda bi, h, qi: (bi, 0, nh + h)),
                  pl.BlockSpec((1, s, LANES), lambda bi, h, qi: (bi, 0, 2 * nh + h)),
                  vec, vec, vec, vec,
                  pl.BlockSpec((1, LANES), lambda bi, h, qi: (0, 0))],
        out_specs=pl.BlockSpec((1, t, LANES), lambda bi, h, qi: (bi, qi, h)),
        out_shape=jax.ShapeDtypeStruct((b, s, nh * LANES), BF16),
        scratch_shapes=[pltpu.VMEM((s, 2 * LANES), BF16)] * 2 + [pltpu.VMEM((2, 1, 2 * t, t), F32),
                        pltpu.VMEM((1, 2 * t, LANES), F32), pltpu.VMEM((1, 2 * t, 2 * LANES), F32)],
        compiler_params=_cparams(3),
        name="attn_b",
    )(qkv3, qkv3, qkv3, lq1.reshape(1, -1), lk1.reshape(1, -1), lq2.reshape(1, -1),
      lk2.reshape(1, -1), subln.reshape(1, -1))


MLA_HEADS = 16
MLA_Q_RANK = 384
MLA_KV_RANK = 256
MLA_ROPE = 32


def _c_proj_kernel(x_ref, g_ref, wq_ref, wkv_ref, wpe_ref, qn_ref, kvn_ref, wqu_ref, wkvu_ref,
                   cos_ref, sm_ref, sp_ref, q_out, kv_out, pe_out):
    xn = _rms(x_ref[...], g_ref[...], NORM_EPS).astype(BF16)
    q_lat = _dot(xn, wq_ref[...])
    kv_lat = _dot(xn, wkv_ref[...])
    cos, sm, sp = cos_ref[...], sm_ref[...], sp_ref[...]
    pw = cos.shape[1]

    def rope(v):
        return v * cos + pltpu.roll(v, pw - MLA_ROPE // 2, 1) * sm + pltpu.roll(v, MLA_ROPE // 2, 1) * sp

    pe_out[...] = rope(_dot(xn, wpe_ref[...])).astype(pe_out.dtype)
    qn = _rms(q_lat, qn_ref[...], NORM_EPS).astype(BF16)
    for p in range(q_out.shape[1] // pw):
        sl = slice(p * pw, (p + 1) * pw)
        q_out[:, sl] = rope(_dot(qn, wqu_ref[:, sl])).astype(q_out.dtype)
    kvn = _rms(kv_lat, kvn_ref[...], NORM_EPS).astype(BF16)
    for p in range(kv_out.shape[1] // pw):
        sl = slice(p * pw, (p + 1) * pw)
        kv_out[:, sl] = _dot(kvn, wkvu_ref[:, sl]).astype(kv_out.dtype)


def _mla_layout():
    qcols = -np.ones(MLA_HEADS * LANES, np.int64)
    kvcols = np.zeros(MLA_HEADS * LANES, np.int64)
    for h in range(MLA_HEADS):
        qsrc = h * (HEAD_DIM + MLA_ROPE)
        ksrc = h * 2 * HEAD_DIM
        base = h * LANES
        if h % 2 == 0:
            qcols[base:base + MLA_ROPE] = qsrc + HEAD_DIM + np.arange(MLA_ROPE)
            qcols[base + HEAD_DIM:base + LANES] = qsrc + np.arange(HEAD_DIM)
            kvcols[base:base + HEAD_DIM] = ksrc + HEAD_DIM + np.arange(HEAD_DIM)
            kvcols[base + HEAD_DIM:base + LANES] = ksrc + np.arange(HEAD_DIM)
        else:
            qcols[base:base + HEAD_DIM] = qsrc + np.arange(HEAD_DIM)
            qcols[base + HEAD_DIM:base + HEAD_DIM + MLA_ROPE] = qsrc + HEAD_DIM + np.arange(MLA_ROPE)
            kvcols[base:base + HEAD_DIM] = ksrc + np.arange(HEAD_DIM)
            kvcols[base + HEAD_DIM:base + LANES] = ksrc + HEAD_DIM + np.arange(HEAD_DIM)
    pe_offsets = (0, LANES + HEAD_DIM)
    return qcols, kvcols, pe_offsets


def _rope_tables(s, pe_offsets):
    half = MLA_ROPE // 2
    inv_freq = ROPE_THETA ** (-jnp.arange(half, dtype=F32) / half)
    ang = jnp.arange(s, dtype=F32)[:, None] * inv_freq[None, :]
    cos, sin = jnp.cos(ang), jnp.sin(ang)
    c = jnp.ones((s, 2 * LANES), F32)
    sm = jnp.zeros((s, 2 * LANES), F32)
    sp = jnp.zeros((s, 2 * LANES), F32)
    for off in pe_offsets:
        c = c.at[:, off:off + half].set(cos).at[:, off + half:off + 2 * half].set(cos)
        sm = sm.at[:, off:off + half].set(-sin)
        sp = sp.at[:, off + half:off + 2 * half].set(sin)
    return c, sm, sp


def c_proj(x2, s, g, w_in, q_norm, w_q_up, kv_norm, w_kv_up, *, tm=512):
    n, d = x2.shape
    qcols, kvcols, pe_offsets = _mla_layout()
    wq = w_in[:, :MLA_Q_RANK].astype(BF16)
    wkv = w_in[:, MLA_Q_RANK:MLA_Q_RANK + MLA_KV_RANK].astype(BF16)
    w_pe = w_in[:, MLA_Q_RANK + MLA_KV_RANK:]
    wpe = jnp.zeros((d, 2 * LANES), F32)
    for off in pe_offsets:
        wpe = wpe.at[:, off:off + MLA_ROPE].set(w_pe)
    wpe = wpe.astype(BF16)
    q_scale = (HEAD_DIM + MLA_ROPE) ** -0.5 * LOG2E
    wqu = jnp.where(jnp.asarray(qcols >= 0)[None, :],
                    w_q_up[:, np.maximum(qcols, 0)] * q_scale, 0.0).astype(BF16)
    wkvu = w_kv_up[:, kvcols].astype(BF16)
    cos, sm, sp = _rope_tables(s, pe_offsets)
    width = MLA_HEADS * LANES
    assert s % tm == 0
    nsb = s // tm
    full = lambda shape: pl.BlockSpec(shape, lambda i: (0, 0))
    tab = pl.BlockSpec((tm, 2 * LANES), lambda i: (i % nsb, 0))
    return pl.pallas_call(
        _c_proj_kernel,
        grid=(n // tm,),
        in_specs=[pl.BlockSpec((tm, d), lambda i: (i, 0)), full((1, d)),
                  full((d, MLA_Q_RANK)), full((d, MLA_KV_RANK)), full((d, 2 * LANES)),
                  full((1, MLA_Q_RANK)), full((1, MLA_KV_RANK)),
                  full((MLA_Q_RANK, width)), full((MLA_KV_RANK, width)), tab, tab, tab],
        out_specs=[pl.BlockSpec((tm, width), lambda i: (i, 0)),
                   pl.BlockSpec((tm, width), lambda i: (i, 0)),
                   pl.BlockSpec((tm, 2 * LANES), lambda i: (i, 0))],
        out_shape=[jax.ShapeDtypeStruct((n, width), BF16),
                   jax.ShapeDtypeStruct((n, width), BF16),
                   jax.ShapeDtypeStruct((n, 2 * LANES), BF16)],
        compiler_params=_cparams(1),
        name="c_proj",
    )(x2, g.reshape(1, d), wq, wkv, wpe, q_norm.reshape(1, -1), kv_norm.reshape(1, -1),
      wqu, wkvu, cos, sm, sp)


def _attn_c_kernel(q_ref, kv_ref, pe_ref, o_ref, kcat_ref, vcat_ref, s_ref, m_ref, acc_ref, *, t):
    qi = pl.program_id(2)
    pw = 2 * LANES

    @pl.when(qi == 0)
    def _():
        lane = lax.broadcasted_iota(I32, kcat_ref.shape, 1)
        is_v = jnp.logical_or(lane < HEAD_DIM, lane >= pw - HEAD_DIM)
        kv = kv_ref[0]
        kcat_ref[...] = jnp.where(is_v, pe_ref[0], kv)
        vcat_ref[...] = jnp.where(is_v, kv, jnp.ones_like(kv))

    _init_softmax_state(m_ref, acc_ref)
    q = q_ref[0]
    first = lax.broadcasted_iota(I32, q.shape, 1) < LANES
    zero = jnp.zeros_like(q)
    q_all = jnp.concatenate([jnp.where(first, q, zero), jnp.where(first, zero, q)], axis=0)

    def rows(kb):
        return pl.ds(pl.multiple_of(kb * t, t), t)

    def scores(kb, c):
        return _dot_nt(q_all, kcat_ref[rows(kb), :])

    def update(s, kb, c):
        _softmax_update(s, vcat_ref[rows(kb), :], m_ref, acc_ref, c)

    _causal_sweep(qi, t, 1, scores, update, s_ref)
    lo = lax.broadcasted_iota(I32, (t, LANES), 1) < HEAD_DIM
    top = acc_ref[0, :t, :LANES]
    bot = acc_ref[0, t:, LANES:]
    o_ref[0] = jnp.where(lo, top / pltpu.roll(top, HEAD_DIM, 1),
                         bot / pltpu.roll(bot, HEAD_DIM, 1)).astype(o_ref.dtype)


def attn_c(q3, kv3, pe3, *, t=512):
    b, s, width = q3.shape
    npair = width // (2 * LANES)
    return pl.pallas_call(
        functools.partial(_attn_c_kernel, t=t),
        grid=(b, npair, s // t),
        in_specs=[pl.BlockSpec((1, t, 2 * LANES), lambda bi, p, qi: (bi, qi, p)),
                  pl.BlockSpec((1, s, 2 * LANES), lambda bi, p, qi: (bi, 0, p)),
                  pl.BlockSpec((1, s, 2 * LANES), lambda bi, p, qi: (bi, 0, 0))],
        out_specs=pl.BlockSpec((1, t, LANES), lambda bi, p, qi: (bi, qi, p)),
        out_shape=jax.ShapeDtypeStruct((b, s, npair * LANES), BF16),
        scratch_shapes=[pltpu.VMEM((s, 2 * LANES), BF16)] * 2 + [pltpu.VMEM((2, 1, 2 * t, t), F32),
                        pltpu.VMEM((1, 2 * t, LANES), F32), pltpu.VMEM((1, 2 * t, 2 * LANES), F32)],
        compiler_params=_cparams(3),
        name="attn_c",
    )(q3, kv3, pe3)


def _attn_d_kernel(q_ref, k_ref, v_ref, o_ref, r_ref, acc_ref, *, t, scale):
    qi = pl.program_id(2)
    q = q_ref[0] * scale
    lo = lax.broadcasted_iota(I32, (t, LANES), 1) < HEAD_DIM
    zero = jnp.zeros_like(q)
    q_all = jnp.concatenate([jnp.where(lo, q, zero), jnp.where(lo, zero, q)], axis=0)
    r_ref[...] = jnp.zeros(r_ref.shape, F32)
    acc_ref[...] = jnp.zeros(acc_ref.shape, F32)
    row = lax.broadcasted_iota(I32, (2 * t, t), 0)
    row = jnp.where(row >= t, row - t, row)
    col = lax.broadcasted_iota(I32, (2 * t, t), 1)
    ones_ge = jnp.where(row >= col, 1.0, 0.0).astype(BF16)
    strict = row > col

    def block(kb, masked):
        off = pl.multiple_of(kb * t, t)
        z = _dot_nt(q_all, k_ref[0, pl.ds(off, t), :])
        lg = jnp.log2(1.0 + jnp.exp2(jnp.abs(z) * (-LOG2E))) * LN2
        ls = jnp.minimum(z, 0.0) - lg
        lk = ls - z
        if masked:
            lk = jnp.where(strict, lk, 0.0)
        hi = lk.astype(BF16)
        lw = (lk - hi.astype(F32)).astype(BF16)
        csum = _dot(jnp.concatenate([hi, lw], axis=1), ones_ge)
        r_prev = r_ref[...]
        a = jnp.exp(ls + (_lane_tile(r_prev, t // LANES) + (csum - lk)))
        if masked:
            a = jnp.where(strict, a, 0.0)
        acc_ref[...] = acc_ref[...] + _dot(a.astype(BF16), v_ref[0, pl.ds(off, t), :])
        r_ref[...] = r_prev + csum[:, 0:1]

    def live():
        return jnp.max(r_ref[...]) >= EXP_UNDERFLOW

    def cond(carry):
        it, alive = carry
        return jnp.logical_and(it < qi, alive)

    def body(carry):
        it, _ = carry
        block(qi - 1 - it, False)
        return it + 1, live()

    @pl.when(qi == 0)
    def _():
        block(qi, True)

    @pl.when(qi > 0)
    def _():
        block(qi, True)
        block(qi - 1, False)
        lax.while_loop(cond, body, (jnp.int32(1), live()))

    o_ref[0] = jnp.where(lo, acc_ref[:t], acc_ref[t:]).astype(o_ref.dtype)


def attn_d(qkv3, *, t=256):
    b, s, width = qkv3.shape
    npair = width // (3 * LANES)
    return pl.pallas_call(
        functools.partial(_attn_d_kernel, t=t, scale=HEAD_DIM ** -0.5),
        grid=(b, npair, s // t),
        in_specs=[pl.BlockSpec((1, t, LANES), lambda bi, p, qi: (bi, qi, p)),
                  pl.BlockSpec((1, s, LANES), lambda bi, p, qi: (bi, 0, npair + p)),
                  pl.BlockSpec((1, s, LANES), lambda bi, p, qi: (bi, 0, 2 * npair + p))],
        out_specs=pl.BlockSpec((1, t, LANES), lambda bi, p, qi: (bi, qi, p)),
        out_shape=jax.ShapeDtypeStruct((b, s, npair * LANES), BF16),
        scratch_shapes=[pltpu.VMEM((2 * t, LANES), F32)] * 2,
        compiler_params=_cparams(3),
        name="attn_d",
    )(qkv3, qkv3, qkv3)


def _silu(x):
    return x / (1.0 + jnp.exp(-x))


def _ffn_kernel(x_ref, g_ref, wg_ref, wu_ref, wd_ref, out_ref, *, ff_chunk):
    x = x_ref[...]
    xn = _rms(x, g_ref[...], NORM_EPS).astype(BF16)
    y = x
    for c in range(0, wg_ref.shape[1], ff_chunk):
        h = _silu(_dot(xn, wg_ref[:, c:c + ff_chunk])) * _dot(xn, wu_ref[:, c:c + ff_chunk])
        y = y + _dot(h.astype(BF16), wd_ref[c:c + ff_chunk, :])
    out_ref[...] = y


def ffn(x2, g, wg, wu, wd, *, tm=512):
    n, d = x2.shape
    ff = wg.shape[1]
    ff_chunk = ff // 2 if (ff // 2) % LANES == 0 else ff
    const = lambda shape: pl.BlockSpec(shape, lambda i: (0, 0), pipeline_mode=pl.Buffered(1))
    return pl.pallas_call(
        functools.partial(_ffn_kernel, ff_chunk=ff_chunk),
        grid=(n // tm,),
        in_specs=[pl.BlockSpec((tm, d), lambda i: (i, 0)),
                  pl.BlockSpec((1, d), lambda i: (0, 0)),
                  const((d, ff)), const((d, ff)), const((ff, d))],
        out_specs=pl.BlockSpec((tm, d), lambda i: (i, 0)),
        out_shape=jax.ShapeDtypeStruct((n, d), F32),
        compiler_params=_cparams(1),
        name="ffn",
    )(x2, g.reshape(1, d), wg, wu, wd)


def _router_kernel(x_ref, g_ref, wr_ref, meta_ref, gate_ref, cnt_ref, run_ref):
    i = pl.program_id(0)
    tm = x_ref.shape[0]

    @pl.when(i == 0)
    def _():
        run_ref[...] = jnp.zeros(run_ref.shape, F32)

    xn = _rms(x_ref[...], g_ref[...], NORM_EPS)
    logits = jnp.dot(xn, wr_ref[...], preferred_element_type=F32, precision=lax.Precision.HIGHEST)
    lane = lax.broadcasted_iota(I32, (tm, LANES), 1)
    lg = jnp.where(lane < N_EXPERTS, logits, NEG_INF)
    m1 = jnp.max(lg, axis=1, keepdims=True)
    e1 = jnp.min(jnp.where(lg == m1, lane, LANES), axis=1, keepdims=True)
    lg2 = jnp.where(lane == e1, NEG_INF, lg)
    m2 = jnp.max(lg2, axis=1, keepdims=True)
    e2 = jnp.min(jnp.where(lg2 == m2, lane, LANES), axis=1, keepdims=True)
    ex = jnp.exp(m2 - m1)
    g1 = 1.0 / (1.0 + ex)
    g2 = ex / (1.0 + ex)
    hit1 = lane == e1
    hit2 = lane == e2
    onehot = jnp.where(jnp.logical_or(hit1, hit2), 1.0, 0.0)
    earlier = (lax.broadcasted_iota(I32, (tm, tm), 0) > lax.broadcasted_iota(I32, (tm, tm), 1))
    before = _dot(jnp.where(earlier, 1.0, 0.0).astype(BF16), onehot.astype(BF16)) + run_ref[0:1, :]
    pos1 = jnp.sum(jnp.where(hit1, before, 0.0), axis=1, keepdims=True).astype(I32)
    pos2 = jnp.sum(jnp.where(hit2, before, 0.0), axis=1, keepdims=True).astype(I32)
    run_ref[...] = run_ref[...] + jnp.sum(onehot, axis=0, keepdims=True)
    meta_ref[...] = jnp.where(lane == 0, e1, jnp.where(lane == 1, e2, jnp.where(
        lane == 2, pos1, jnp.where(lane == 3, pos2, 0))))
    gate_ref[...] = jnp.where(lane == 0, g1, jnp.where(lane == 1, g2, 0.0))
    cnt_ref[...] = run_ref[...]


def router(x2, g, w_router, *, tm=512):
    n, d = x2.shape
    wr = jnp.zeros((d, LANES), F32).at[:, :N_EXPERTS].set(w_router)
    row = lambda w: pl.BlockSpec((tm, w), lambda i: (i, 0))
    return pl.pallas_call(
        _router_kernel,
        grid=(n // tm,),
        in_specs=[row(d), pl.BlockSpec((1, d), lambda i: (0, 0)),
                  pl.BlockSpec((d, LANES), lambda i: (0, 0))],
        out_specs=[row(LANES), row(LANES), pl.BlockSpec((8, LANES), lambda i: (0, 0))],
        out_shape=[jax.ShapeDtypeStruct((n, LANES), I32),
                   jax.ShapeDtypeStruct((n, LANES), F32), jax.ShapeDtypeStruct((8, LANES), F32)],
        scratch_shapes=[pltpu.VMEM((8, LANES), F32)],
        compiler_params=_cparams(1),
        name="router",
    )(x2, g.reshape(1, d), wr)


def _row_copy(src_ref, src_row, dst_ref, dst_row, sem):
    return pltpu.make_async_copy(src_ref.at[pl.ds(src_row, 1)], dst_ref.at[pl.ds(dst_row, 1)], sem)


def _dispatch_kernel(dest_ref, x_ref, g_ref, xb_in_ref, xb_ref, xn_ref, sem, *, tb):
    del xb_in_ref
    xn_ref[...] = _rms(x_ref[...], g_ref[...], NORM_EPS)

    def issue(tt, carry):
        for k in range(2):
            _row_copy(xn_ref, tt, xb_ref, dest_ref[2 * tt + k], sem).start()
        return carry

    lax.fori_loop(0, tb, issue, 0, unroll=8)
    for k in range(2):
        pltpu.make_async_copy(xn_ref, xb_ref.at[pl.ds(0, tb)], sem).wait()


def dispatch(dest_flat, x2, g, n_rows, *, tb=512):
    n, d = x2.shape
    xb0 = jnp.zeros((n_rows, d), F32)
    return pl.pallas_call(
        functools.partial(_dispatch_kernel, tb=tb),
        grid=(n // tb,),
        in_specs=[pl.BlockSpec((2 * tb,), lambda i: (i,), memory_space=pltpu.SMEM),
                  pl.BlockSpec((tb, d), lambda i: (i, 0)),
                  pl.BlockSpec((1, d), lambda i: (0, 0)),
                  pl.BlockSpec(memory_space=pl.ANY)],
        out_specs=pl.BlockSpec(memory_space=pl.ANY),
        out_shape=jax.ShapeDtypeStruct((n_rows, d), F32),
        scratch_shapes=[pltpu.VMEM((tb, d), F32), pltpu.SemaphoreType.DMA(())],
        input_output_aliases={3: 0},
        compiler_params=_cparams(1),
        name="moe_dispatch",
    )(dest_flat, x2, g.reshape(1, d), xb0)


def _expert_kernel(be_ref, nu_ref, x_ref, wg_ref, wu_ref, wd_ref, y_ref, *, ff_chunk):
    i = pl.program_id(0)

    @pl.when(i < nu_ref[0])
    def _():
        xb = x_ref[...].astype(BF16)
        y = jnp.zeros(y_ref.shape, F32)
        for c in range(0, wg_ref.shape[2], ff_chunk):
            h = (_silu(_dot(xb, wg_ref[0, :, c:c + ff_chunk]))
                 * _dot(xb, wu_ref[0, :, c:c + ff_chunk]))
            y = y + _dot(h.astype(BF16), wd_ref[0, c:c + ff_chunk, :])
        y_ref[...] = y

    @pl.when(i >= nu_ref[0])
    def _():
        y_ref[...] = jnp.zeros(y_ref.shape, F32)


def experts(block_e, n_used, xb, wg, wu, wd, *, ff_chunk=512):
    rows, d = xb.shape
    ff = wg.shape[2]
    assert ff % ff_chunk == 0
    wspec = lambda shape: pl.BlockSpec(shape, lambda i, be, nu: (be[i], 0, 0),
                                       pipeline_mode=pl.Buffered(1))
    return pl.pallas_call(
        functools.partial(_expert_kernel, ff_chunk=ff_chunk),
        grid_spec=pltpu.PrefetchScalarGridSpec(
            num_scalar_prefetch=2,
            grid=(rows // MOE_ROWS,),
            in_specs=[pl.BlockSpec((MOE_ROWS, d), lambda i, be, nu: (i, 0)),
                      wspec((1, d, ff)), wspec((1, d, ff)), wspec((1, ff, d))],
            out_specs=pl.BlockSpec((MOE_ROWS, d), lambda i, be, nu: (i, 0)),
        ),
        out_shape=jax.ShapeDtypeStruct((rows, d), F32),
        compiler_params=_cparams(1),
        name="moe_experts",
    )(block_e, n_used, xb, wg, wu, wd)


def _combine_kernel(dest_ref, x_ref, gate_ref, fg_ref, yb_ref, out_ref, buf_ref, sem, *, tb, final):
    def issue(tt, carry):
        for k in range(2):
            _row_copy(yb_ref, dest_ref[2 * tt + k], buf_ref.at[k], tt, sem).start()
        return carry

    lax.fori_loop(0, tb, issue, 0, unroll=8)
    for k in range(2):
        pltpu.make_async_copy(yb_ref.at[pl.ds(0, tb)], buf_ref.at[k], sem).wait()
    gate = gate_ref[...]
    y = x_ref[...] + gate[:, 0:1] * buf_ref[0] + gate[:, 1:2] * buf_ref[1]
    if final:
        y = _rms(y, fg_ref[...], NORM_EPS)
    out_ref[...] = y


def combine(dest_flat, x2, gates, yb, final_g, *, final, tb=512):
    n, d = x2.shape
    return pl.pallas_call(
        functools.partial(_combine_kernel, tb=tb, final=final),
        grid=(n // tb,),
        in_specs=[pl.BlockSpec((2 * tb,), lambda i: (i,), memory_space=pltpu.SMEM),
                  pl.BlockSpec((tb, d), lambda i: (i, 0)),
                  pl.BlockSpec((tb, LANES), lambda i: (i, 0)),
                  pl.BlockSpec((1, d), lambda i: (0, 0)),
                  pl.BlockSpec(memory_space=pl.ANY)],
        out_specs=pl.BlockSpec((tb, d), lambda i: (i, 0)),
        out_shape=jax.ShapeDtypeStruct((n, d), F32),
        scratch_shapes=[pltpu.VMEM((2, tb, d), F32), pltpu.SemaphoreType.DMA(())],
        compiler_params=_cparams(1),
        name="moe_combine",
    )(dest_flat, x2, gates, final_g.reshape(1, d), yb)


def moe(x2, g, w_router, wg, wu, wd, final_g, *, final):
    n, d = x2.shape
    meta, gates, cnt = router(x2, g, w_router)
    counts = cnt[0, :N_EXPERTS].astype(I32)
    padded = (counts + MOE_ROWS - 1) // MOE_ROWS * MOE_ROWS
    pad_end = jnp.cumsum(padded)
    pad_start = pad_end - padded
    dest = (pad_start[meta[:, 0:2]] + meta[:, 2:4]).reshape(-1).astype(I32)
    n_blocks = -(-2 * n // MOE_ROWS) + N_EXPERTS
    starts = jnp.arange(n_blocks, dtype=I32) * MOE_ROWS
    block_e = jnp.minimum(jnp.sum((pad_end[None, :] <= starts[:, None]).astype(I32), axis=1),
                          N_EXPERTS - 1)
    n_used = (pad_end[-1:] // MOE_ROWS).astype(I32)
    xb = dispatch(dest, x2, g, n_blocks * MOE_ROWS)
    yb = experts(block_e, n_used, xb, wg.astype(BF16), wu.astype(BF16), wd.astype(BF16))
    return combine(dest, x2, gates, yb, final_g, final=final)


def _bf16(w):
    return w.astype(BF16)


def mixer_a(x2, b, norm, w_qkv, w_o):
    d = x2.shape[1]
    n_groups = len(A_PATTERN)
    dils = tuple(dil for _, dil in A_PATTERN)
    gw = A_HEADS * HEAD_DIM
    w = w_qkv.reshape(d, 3, n_groups, gw)
    w = w * jnp.asarray([HEAD_DIM ** -0.5 * LOG2E, 1.0, 1.0], F32).reshape(1, 3, 1, 1)
    w = _bf16(w.transpose(0, 2, 1, 3).reshape(d, n_groups * 3 * gw))
    views = a_proj(x2, norm, w, dils)
    slopes = 2.0 ** (-8.0 * np.arange(1, n_groups * A_HEADS + 1) / (n_groups * A_HEADS))
    outs, lses = [], []
    for gi, (window, dil) in enumerate(A_PATTERN):
        assert window // dil == QB
        o, lse = attn_a_group(views[gi], b, dil, slopes[gi * A_HEADS:(gi + 1) * A_HEADS])
        outs.append(o)
        lses.append(lse)
    return mix_oproj_res(x2, outs, lses, _bf16(w_o), dils)


def mixer_b(x2, b, norm, w_qkv, lq1, lk1, lq2, lk2, subln, w_o, *, layer, t=512):
    n = x2.shape[0]
    n_q = w_qkv.shape[1] // 3
    w = jnp.concatenate([w_qkv[:, :n_q] * (HEAD_DIM ** -0.5 * LOG2E), w_qkv[:, n_q:]], axis=1)
    qkv = norm_proj(x2, norm, _bf16(w)).reshape(b, n // b, -1)
    o = attn_b(qkv, lq1, lk1, lq2, lk2, subln, layer, t=t)
    return oproj_res(x2, o.reshape(n, -1), _bf16(w_o))


def mixer_c(x2, b, norm, w_in, q_norm, w_q_up, kv_norm, w_kv_up, w_o, *, t=512):
    n = x2.shape[0]
    s = n // b
    q, kv, pe = c_proj(x2, s, norm, w_in, q_norm, w_q_up, kv_norm, w_kv_up)
    o = attn_c(q.reshape(b, s, -1), kv.reshape(b, s, -1), pe.reshape(b, s, -1), t=t)
    return oproj_res(x2, o.reshape(n, -1), _bf16(w_o))


def mixer_d(x2, b, norm, w_qkv, w_o, *, t=256):
    n = x2.shape[0]
    qkv = norm_proj(x2, norm, _bf16(w_qkv)).reshape(b, n // b, -1)
    o = attn_d(qkv, t=t)
    return oproj_res(x2, o.reshape(n, -1), _bf16(w_o))
def kernel(x, a_norm, a_w_qkv, a_w_o, ffn0_norm, ffn0_w_gate, ffn0_w_up, ffn0_w_down, b_norm, b_w_qkv, b_lambda_q1, b_lambda_k1, b_lambda_q2, b_lambda_k2, b_subln, b_w_o, moe1_norm, moe1_w_router, moe1_w_gate, moe1_w_up, moe1_w_down, c_norm, c_w_in, c_q_norm, c_w_q_up, c_kv_norm, c_w_kv_up, c_w_o, ffn2_norm, ffn2_w_gate, ffn2_w_up, ffn2_w_down, d_norm, d_w_qkv, d_w_o, moe3_norm, moe3_w_router, moe3_w_gate, moe3_w_up, moe3_w_down, final_norm):
    b, s, d = x.shape
    x2 = x.reshape(b * s, d)
    bf = _bf16
    x2 = mixer_a(x2, b, a_norm, a_w_qkv, a_w_o)
    x2 = ffn(x2, ffn0_norm, bf(ffn0_w_gate), bf(ffn0_w_up), bf(ffn0_w_down))
    x2 = mixer_b(x2, b, b_norm, b_w_qkv, b_lambda_q1, b_lambda_k1, b_lambda_q2, b_lambda_k2,
                 b_subln, b_w_o, layer=1)
    x2 = moe(x2, moe1_norm, moe1_w_router, moe1_w_gate, moe1_w_up, moe1_w_down, final_norm,
             final=False)
    x2 = mixer_c(x2, b, c_norm, c_w_in, c_q_norm, c_w_q_up, c_kv_norm, c_w_kv_up, c_w_o)
    x2 = ffn(x2, ffn2_norm, bf(ffn2_w_gate), bf(ffn2_w_up), bf(ffn2_w_down))
    x2 = mixer_d(x2, b, d_norm, d_w_qkv, d_w_o)
    x2 = moe(x2, moe3_norm, moe3_w_router, moe3_w_gate, moe3_w_up, moe3_w_down, final_norm,
             final=True)
    return x2.reshape(b, s, d)
```

```python
import functools
import math

import numpy as np
import jax
import jax.numpy as jnp
from jax import lax
from jax.experimental import pallas as pl
from jax.experimental.pallas import tpu as pltpu

F32 = jnp.float32
BF16 = jnp.bfloat16
I32 = jnp.int32

LANES = 128
MXU_DIM = 256
HEAD_DIM = 64
NORM_EPS = 1e-6
SUBLN_EPS = 1e-5
ROPE_THETA = 10000.0
A_PATTERN = ((128, 1), (512, 4), (2048, 16))
A_HEADS = 8
QB = 128
N_EXPERTS = 8
MOE_ROWS = 512
VMEM_LIMIT = 56 * 1024 * 1024
NEG_INF = float("-inf")
EXP_UNDERFLOW = -105.0


def _cparams(n_axes, vmem=VMEM_LIMIT):
    return pltpu.CompilerParams(dimension_semantics=("arbitrary",) * n_axes,
                                vmem_limit_bytes=vmem)


def _dot(a, b):
    return jnp.dot(a, b, preferred_element_type=F32)


def _dot_nt(a, b):
    return lax.dot_general(a, b, (((1,), (1,)), ((), ())), preferred_element_type=F32)


def _rms(x, g, eps):
    return x * lax.rsqrt(jnp.mean(x * x, axis=-1, keepdims=True) + eps) * g


def _lane_tile(x, n):
    return x if n == 1 else jnp.concatenate([x] * n, axis=1)


def _norm_proj_kernel(x_ref, g_ref, w_ref, o_ref, *, col_chunk):
    xn = _rms(x_ref[...], g_ref[...], NORM_EPS).astype(BF16)
    n = o_ref.shape[1]
    for c in range(0, n, col_chunk):
        o_ref[:, c:c + col_chunk] = _dot(xn, w_ref[:, c:c + col_chunk]).astype(o_ref.dtype)


def norm_proj(x2, g, w, *, tm=512, col_chunk=512):
    n, d = x2.shape
    n_out = w.shape[1]
    assert n % tm == 0 and n_out % col_chunk == 0
    return pl.pallas_call(
        functools.partial(_norm_proj_kernel, col_chunk=col_chunk),
        grid=(n // tm,),
        in_specs=[pl.BlockSpec((tm, d), lambda i: (i, 0)),
                  pl.BlockSpec((1, d), lambda i: (0, 0)),
                  pl.BlockSpec((d, n_out), lambda i: (0, 0))],
        out_specs=pl.BlockSpec((tm, n_out), lambda i: (i, 0)),
        out_shape=jax.ShapeDtypeStruct((n, n_out), BF16),
        compiler_params=_cparams(1),
        name="norm_proj",
    )(x2, g.reshape(1, d), w)


def _oproj_kernel(x_ref, o_ref, w_ref, out_ref):
    out_ref[...] = x_ref[...] + _dot(o_ref[...], w_ref[...])


def oproj_res(x2, o2, w, *, tm=1024):
    n, d = x2.shape
    k = o2.shape[1]
    return pl.pallas_call(
        _oproj_kernel,
        grid=(n // tm,),
        in_specs=[pl.BlockSpec((tm, d), lambda i: (i, 0)),
                  pl.BlockSpec((tm, k), lambda i: (i, 0)),
                  pl.BlockSpec((k, d), lambda i: (0, 0))],
        out_specs=pl.BlockSpec((tm, d), lambda i: (i, 0)),
        out_shape=jax.ShapeDtypeStruct((n, d), F32),
        compiler_params=_cparams(1),
        name="oproj_res",
    )(x2, o2, w)


LOG2E = math.log2(math.e)
LN2 = math.log(2.0)


def _a_proj_kernel(x_ref, g_ref, w_ref, *rest, dils, gcols):
    outs, scr = rest[:len(dils)], rest[len(dils)]
    xn = _rms(x_ref[...], g_ref[...], NORM_EPS).astype(BF16)
    tm = x_ref.shape[0]
    for gi, dil in enumerate(dils):
        res = _dot(xn, w_ref[:, gi * gcols:(gi + 1) * gcols])
        if dil == 1:
            outs[gi][...] = res.astype(BF16)
        else:
            for c in range(gcols // LANES):
                scr[c] = res[:, c * LANES:(c + 1) * LANES]
            for r in range(dil):
                for c in range(gcols // LANES):
                    col = r * gcols + c * LANES
                    outs[gi][:, col:col + LANES] = (
                        scr[c, pl.ds(r, tm // dil, stride=dil), :].astype(BF16))


def a_proj(x2, g, w, dils, *, tm=512):
    n, d = x2.shape
    gcols = w.shape[1] // len(dils)
    return pl.pallas_call(
        functools.partial(_a_proj_kernel, dils=dils, gcols=gcols),
        grid=(n // tm,),
        in_specs=[pl.BlockSpec((tm, d), lambda i: (i, 0)),
                  pl.BlockSpec((1, d), lambda i: (0, 0)),
                  pl.BlockSpec((d, w.shape[1]), lambda i: (0, 0))],
        out_specs=[pl.BlockSpec((tm // dil, dil * gcols), lambda i: (i, 0)) for dil in dils],
        out_shape=[jax.ShapeDtypeStruct((n // dil, dil * gcols), BF16) for dil in dils],
        scratch_shapes=[pltpu.VMEM((gcols // LANES, tm, LANES), F32)],
        compiler_params=_cparams(1),
        name="a_proj",
    )(x2, g.reshape(1, d), w)


def _attn_a_kernel(q_ref, kc_ref, kp_ref, vc_ref, vp_ref, bc_ref, bp_ref, o_ref, lse_ref):
    has_prev = pl.program_id(2) > 0
    hw = MXU_DIM
    nh = hw // HEAD_DIM
    head_of_lane = lax.broadcasted_iota(I32, (QB, hw), 1) // HEAD_DIM
    for half in range(q_ref.shape[2] // hw):
        sl = slice(half * hw, (half + 1) * hw)
        tab = slice(half * nh * QB, (half + 1) * nh * QB)
        q = q_ref[0, :, sl]
        zero = jnp.zeros_like(q)
        q_all = jnp.concatenate([jnp.where(head_of_lane == h, q, zero) for h in range(nh)], axis=0)
        sc = _dot_nt(q_all, kc_ref[0, :, sl]) + bc_ref[tab, :]
        sp = jnp.where(has_prev, _dot_nt(q_all, kp_ref[0, :, sl]) + bp_ref[tab, :], NEG_INF)
        m = jnp.maximum(jnp.max(sc, axis=1, keepdims=True), jnp.max(sp, axis=1, keepdims=True))
        pc = jnp.exp2(sc - m)
        pp = jnp.exp2(sp - m)
        l = jnp.sum(pc, axis=1, keepdims=True) + jnp.sum(pp, axis=1, keepdims=True)
        acc = _dot(pc.astype(BF16), vc_ref[0, :, sl]) + _dot(pp.astype(BF16), vp_ref[0, :, sl])
        lse = (m + jnp.log2(l)) * LN2
        o = jnp.zeros((QB, hw), F32)
        ls = jnp.zeros((QB, hw), F32)
        for h in range(nh):
            rows = slice(h * QB, (h + 1) * QB)
            mine = head_of_lane == h
            o = jnp.where(mine, acc[rows] / l[rows], o)
            ls = jnp.where(mine, lse[rows], ls)
        o_ref[0, :, sl] = o.astype(o_ref.dtype)
        lse_ref[0, :, sl] = ls


def _a_bias_tables(dil, slopes):
    rel_c = np.arange(QB)[:, None] - np.arange(QB)[None, :]
    rel_p = rel_c + QB
    sl = np.asarray(slopes, np.float64)[:, None, None] * dil * LOG2E
    bc = np.where(rel_c >= 0, -sl * rel_c, -np.inf).reshape(-1, QB)
    bp = np.where(rel_p <= QB, -sl * rel_p, -np.inf).reshape(-1, QB)
    return jnp.asarray(bc, F32), jnp.asarray(bp, F32)


def attn_a_group(view2, b, dil, slopes):
    gw = A_HEADS * HEAD_DIM
    l = view2.shape[0] // b
    assert l % QB == 0 and view2.shape[1] == dil * 3 * gw
    nb = l // QB
    view = view2.reshape(b, l, dil * 3 * gw)
    bc, bp = _a_bias_tables(dil, slopes)

    def spec(which, prev):
        def imap(bi, r, j):
            return (bi, jnp.maximum(j - 1, 0) if prev else j, r * 3 + which)
        return pl.BlockSpec((1, QB, gw), imap)

    tab = pl.BlockSpec((A_HEADS * QB, QB), lambda bi, r, j: (0, 0))
    o, lse = pl.pallas_call(
        _attn_a_kernel,
        grid=(b, dil, nb),
        in_specs=[spec(0, False), spec(1, False), spec(1, True), spec(2, False), spec(2, True),
                  tab, tab],
        out_specs=[pl.BlockSpec((1, QB, gw), lambda bi, r, j: (bi, j, r)),
                   pl.BlockSpec((1, QB, gw), lambda bi, r, j: (bi, j, r))],
        out_shape=[jax.ShapeDtypeStruct((b, l, dil * gw), BF16),
                   jax.ShapeDtypeStruct((b, l, dil * gw), F32)],
        compiler_params=_cparams(3),
        name=f"attn_a_d{dil}",
    )(view, view, view, view, view, bc, bp)
    return o.reshape(b * l, dil * gw), lse.reshape(b * l, dil * gw)


def _mix_oproj_kernel(x_ref, *rest, dils, gw):
    ng = len(dils)
    o_refs, l_refs = rest[:ng], rest[ng:2 * ng]
    w_ref, out_ref, o_scr, l_scr = rest[2 * ng:]
    tm = x_ref.shape[0]
    os_, ls = [], []
    for gi, dil in enumerate(dils):
        if dil == 1:
            os_.append(o_refs[gi][...].astype(F32))
            ls.append(l_refs[gi][...])
        else:
            nct = gw // LANES
            for r in range(dil):
                rows = pl.ds(r, tm // dil, stride=dil)
                for c in range(nct):
                    cols = slice(r * gw + c * LANES, r * gw + (c + 1) * LANES)
                    o_scr[gi, c, rows, :] = o_refs[gi][:, cols].astype(F32)
                    l_scr[gi, c, rows, :] = l_refs[gi][:, cols]
            os_.append(jnp.concatenate([o_scr[gi, c] for c in range(nct)], axis=1))
            ls.append(jnp.concatenate([l_scr[gi, c] for c in range(nct)], axis=1))
    m = functools.reduce(jnp.maximum, ls)
    es = [jnp.exp(v - m) for v in ls]
    den = functools.reduce(lambda a, c: a + c, es)
    mixed = functools.reduce(lambda a, c: a + c, [(e / den) * o for e, o in zip(es, os_)])
    out_ref[...] = x_ref[...] + _dot(mixed.astype(BF16), w_ref[...])


def mix_oproj_res(x2, outs, lses, w, dils, *, tm=1024):
    n, d = x2.shape
    gw = w.shape[0]
    ng = len(dils)
    views = [pl.BlockSpec((tm // dil, dil * gw), lambda i: (i, 0)) for dil in dils]
    return pl.pallas_call(
        functools.partial(_mix_oproj_kernel, dils=dils, gw=gw),
        grid=(n // tm,),
        in_specs=[pl.BlockSpec((tm, d), lambda i: (i, 0))] + views + views
                 + [pl.BlockSpec((gw, d), lambda i: (0, 0))],
        out_specs=pl.BlockSpec((tm, d), lambda i: (i, 0)),
        out_shape=jax.ShapeDtypeStruct((n, d), F32),
        scratch_shapes=[pltpu.VMEM((ng, gw // LANES, tm, LANES), F32)] * 2,
        compiler_params=_cparams(1),
        name="mix_oproj_res",
    )(x2, *outs, *lses, w)


def _softmax_update(s, v, m_ref, acc_ref, idx):
    tk = s.shape[1]
    m_prev = m_ref[idx]
    m_next = jnp.maximum(m_prev, jnp.max(s, axis=1, keepdims=True))
    p = jnp.exp2(s - _lane_tile(m_next, tk // LANES))
    alpha = jnp.exp2(m_prev - m_next)
    acc_ref[idx] = (_lane_tile(alpha, acc_ref.shape[-1] // LANES) * acc_ref[idx]
                    + _dot(p.astype(BF16), v))
    m_ref[idx] = m_next


def _init_softmax_state(m_ref, acc_ref):
    m_ref[...] = jnp.full(m_ref.shape, NEG_INF, F32)
    acc_ref[...] = jnp.zeros(acc_ref.shape, F32)


def _causal_sweep(qi, t, n_chain, scores, update, s_ref):
    def qk(kb, buf):
        for c in range(n_chain):
            s_ref[buf, c] = scores(kb, c)

    def upd(kb, buf, masked):
        for c in range(n_chain):
            s = s_ref[buf, c]
            if masked:
                s = jnp.where(_causal_mask(s.shape[0], t), s, NEG_INF)
            update(s, kb, c)

    qk(0, 0)

    def body(i, carry):
        kb = 2 * i
        qk(kb + 1, 1)
        upd(kb, 0, False)
        qk(kb + 2, 0)
        upd(kb + 1, 1, False)
        return carry

    lax.fori_loop(0, lax.shift_right_logical(qi, 1), body, 0)
    odd = jnp.bitwise_and(qi, 1) == 1

    @pl.when(odd)
    def _():
        qk(qi, 1)
        upd(qi - 1, 0, False)
        upd(qi, 1, True)

    @pl.when(jnp.logical_not(odd))
    def _():
        upd(qi, 0, True)


def _split3(x):
    hi = x.astype(BF16).astype(F32)
    mid = (x - hi).astype(BF16).astype(F32)
    lo = (x - hi - mid).astype(BF16).astype(F32)
    return hi, mid, lo


def _causal_mask(n_rows, t):
    row = lax.broadcasted_iota(I32, (n_rows, t), 0)
    for _ in range(n_rows // t - 1):
        row = jnp.where(row >= t, row - t, row)
    return row >= lax.broadcasted_iota(I32, (n_rows, t), 1)


def _attn_b_kernel(q_ref, k_ref, v_ref, lq1, lk1, lq2, lk2, g_ref, o_ref, kb_ref, vb_ref, s_ref,
                   m_ref, acc_ref, *, t, slopes, lam_init):
    h = pl.program_id(1)
    qi = pl.program_id(2)
    n_bias = 3

    @pl.when(qi == 0)
    def _():
        slope = jnp.float32(0.0)
        for hh, sv in enumerate(slopes):
            slope = jnp.where(h == hh, jnp.float32(sv), slope)
        shape = (k_ref.shape[1], LANES)
        lane = lax.broadcasted_iota(I32, shape, 1)
        parts = _split3(slope * lax.broadcasted_iota(I32, shape, 0).astype(F32))
        bias = jnp.zeros(shape, F32)
        for j, part in enumerate(parts):
            bias = jnp.where(lane == j, part, bias)
        kb_ref[:, :LANES] = k_ref[0]
        kb_ref[:, LANES:] = bias.astype(BF16)
        vb_ref[:, :LANES] = v_ref[0]
        vb_ref[:, LANES:] = jnp.ones(shape, BF16)

    q = q_ref[0].astype(F32)
    lane = lax.broadcasted_iota(I32, (t, LANES), 1)
    ones = jnp.where(lane < n_bias, 1.0, 0.0)
    q_all = jnp.concatenate(
        [jnp.concatenate([jnp.where(lane < HEAD_DIM, q, 0.0), ones], axis=1),
         jnp.concatenate([jnp.where(lane >= HEAD_DIM, q, 0.0), ones], axis=1)], axis=0).astype(BF16)
    _init_softmax_state(m_ref, acc_ref)

    def rows(kb):
        return pl.ds(pl.multiple_of(kb * t, t), t)

    def scores(kb, c):
        return _dot_nt(q_all, kb_ref[rows(kb), :])

    def update(s, kb, c):
        _softmax_update(s, vb_ref[rows(kb), :], m_ref, acc_ref, c)

    _causal_sweep(qi, t, 1, scores, update, s_ref)

    lam = (jnp.exp(jnp.sum(lq1[...] * lk1[...], axis=1, keepdims=True))
           - jnp.exp(jnp.sum(lq2[...] * lk2[...], axis=1, keepdims=True)) + lam_init)
    acc = acc_ref[0]
    o = acc[:t, :LANES] / acc[:t, LANES:] - lam * (acc[t:, :LANES] / acc[t:, LANES:])
    o = _rms(o, g_ref[...], SUBLN_EPS) * (1.0 - lam_init)
    o_ref[0] = o.astype(o_ref.dtype)


def attn_b(qkv3, lq1, lk1, lq2, lk2, subln, layer, *, t=512):
    b, s, width = qkv3.shape
    nh = width // (3 * LANES)
    slopes = tuple(float(2.0 ** (-8.0 * (i + 1) / nh)) * LOG2E for i in range(nh))
    lam_init = 0.8 - 0.6 * math.exp(-0.3 * layer)
    vec = pl.BlockSpec((1, HEAD_DIM), lambda bi, h, qi: (0, 0))
    return pl.pallas_call(
        functools.partial(_attn_b_kernel, t=t, slopes=slopes, lam_init=lam_init),
        grid=(b, nh, s // t),
        in_specs=[pl.BlockSpec((1, t, LANES), lambda bi, h, qi: (bi, qi, h)),
                  pl.BlockSpec((1, s, LANES), lambda bi, h, qi: (bi, 0, nh + h)),
                  pl.BlockSpec((1, s, LANES), lambda bi, h, qi: (bi, 0, 2 * nh + h)),
                  vec, vec, vec, vec,
                  pl.BlockSpec((1, LANES), lambda bi, h, qi: (0, 0))],
        out_specs=pl.BlockSpec((1, t, LANES), lambda bi, h, qi: (bi, qi, h)),
        out_shape=jax.ShapeDtypeStruct((b, s, nh * LANES), BF16),
        scratch_shapes=[pltpu.VMEM((s, 2 * LANES), BF16)] * 2 + [pltpu.VMEM((2, 1, 2 * t, t), F32),
                        pltpu.VMEM((1, 2 * t, LANES), F32), pltpu.VMEM((1, 2 * t, 2 * LANES), F32)],
        compiler_params=_cparams(3),
        name="attn_b",
    )(qkv3, qkv3, qkv3, lq1.reshape(1, -1), lk1.reshape(1, -1), lq2.reshape(1, -1),
      lk2.reshape(1, -1), subln.reshape(1, -1))


MLA_HEADS = 16
MLA_Q_RANK = 384
MLA_KV_RANK = 256
MLA_ROPE = 32


def _c_proj_kernel(x_ref, g_ref, wq_ref, wkv_ref, wpe_ref, qn_ref, kvn_ref, wqu_ref, wkvu_ref,
                   cos_ref, sm_ref, sp_ref, q_out, kv_out, pe_out):
    xn = _rms(x_ref[...], g_ref[...], NORM_EPS).astype(BF16)
    q_lat = _dot(xn, wq_ref[...])
    kv_lat = _dot(xn, wkv_ref[...])
    cos, sm, sp = cos_ref[...], sm_ref[...], sp_ref[...]
    pw = cos.shape[1]

    def rope(v):
        return v * cos + pltpu.roll(v, pw - MLA_ROPE // 2, 1) * sm + pltpu.roll(v, MLA_ROPE // 2, 1) * sp

    pe_out[...] = rope(_dot(xn, wpe_ref[...])).astype(pe_out.dtype)
    qn = _rms(q_lat, qn_ref[...], NORM_EPS).astype(BF16)
    for p in range(q_out.shape[1] // pw):
        sl = slice(p * pw, (p + 1) * pw)
        q_out[:, sl] = rope(_dot(qn, wqu_ref[:, sl])).astype(q_out.dtype)
    kvn = _rms(kv_lat, kvn_ref[...], NORM_EPS).astype(BF16)
    for p in range(kv_out.shape[1] // pw):
        sl = slice(p * pw, (p + 1) * pw)
        kv_out[:, sl] = _dot(kvn, wkvu_ref[:, sl]).astype(kv_out.dtype)


def _mla_layout():
    qcols = -np.ones(MLA_HEADS * LANES, np.int64)
    kvcols = np.zeros(MLA_HEADS * LANES, np.int64)
    for h in range(MLA_HEADS):
        qsrc = h * (HEAD_DIM + MLA_ROPE)
        ksrc = h * 2 * HEAD_DIM
        base = h * LANES
        if h % 2 == 0:
            qcols[base:base + MLA_ROPE] = qsrc + HEAD_DIM + np.arange(MLA_ROPE)
            qcols[base + HEAD_DIM:base + LANES] = qsrc + np.arange(HEAD_DIM)
            kvcols[base:base + HEAD_DIM] = ksrc + HEAD_DIM + np.arange(HEAD_DIM)
            kvcols[base + HEAD_DIM:base + LANES] = ksrc + np.arange(HEAD_DIM)
        else:
            qcols[base:base + HEAD_DIM] = qsrc + np.arange(HEAD_DIM)
            qcols[base + HEAD_DIM:base + HEAD_DIM + MLA_ROPE] = qsrc + HEAD_DIM + np.arange(MLA_ROPE)
            kvcols[base:base + HEAD_DIM] = ksrc + np.arange(HEAD_DIM)
            kvcols[base + HEAD_DIM:base + LANES] = ksrc + HEAD_DIM + np.arange(HEAD_DIM)
    pe_offsets = (0, LANES + HEAD_DIM)
    return qcols, kvcols, pe_offsets


def _rope_tables(s, pe_offsets):
    half = MLA_ROPE // 2
    inv_freq = ROPE_THETA ** (-jnp.arange(half, dtype=F32) / half)
    ang = jnp.arange(s, dtype=F32)[:, None] * inv_freq[None, :]
    cos, sin = jnp.cos(ang), jnp.sin(ang)
    c = jnp.ones((s, 2 * LANES), F32)
    sm = jnp.zeros((s, 2 * LANES), F32)
    sp = jnp.zeros((s, 2 * LANES), F32)
    for off in pe_offsets:
        c = c.at[:, off:off + half].set(cos).at[:, off + half:off + 2 * half].set(cos)
        sm = sm.at[:, off:off + half].set(-sin)
        sp = sp.at[:, off + half:off + 2 * half].set(sin)
    return c, sm, sp


def c_proj(x2, s, g, w_in, q_norm, w_q_up, kv_norm, w_kv_up, *, tm=512):
    n, d = x2.shape
    qcols, kvcols, pe_offsets = _mla_layout()
    wq = w_in[:, :MLA_Q_RANK].astype(BF16)
    wkv = w_in[:, MLA_Q_RANK:MLA_Q_RANK + MLA_KV_RANK].astype(BF16)
    w_pe = w_in[:, MLA_Q_RANK + MLA_KV_RANK:]
    wpe = jnp.zeros((d, 2 * LANES), F32)
    for off in pe_offsets:
        wpe = wpe.at[:, off:off + MLA_ROPE].set(w_pe)
    wpe = wpe.astype(BF16)
    q_scale = (HEAD_DIM + MLA_ROPE) ** -0.5 * LOG2E
    wqu = jnp.where(jnp.asarray(qcols >= 0)[None, :],
                    w_q_up[:, np.maximum(qcols, 0)] * q_scale, 0.0).astype(BF16)
    wkvu = w_kv_up[:, kvcols].astype(BF16)
    cos, sm, sp = _rope_tables(s, pe_offsets)
    width = MLA_HEADS * LANES
    assert s % tm == 0
    nsb = s // tm
    full = lambda shape: pl.BlockSpec(shape, lambda i: (0, 0))
    tab = pl.BlockSpec((tm, 2 * LANES), lambda i: (i % nsb, 0))
    return pl.pallas_call(
        _c_proj_kernel,
        grid=(n // tm,),
        in_specs=[pl.BlockSpec((tm, d), lambda i: (i, 0)), full((1, d)),
                  full((d, MLA_Q_RANK)), full((d, MLA_KV_RANK)), full((d, 2 * LANES)),
                  full((1, MLA_Q_RANK)), full((1, MLA_KV_RANK)),
                  full((MLA_Q_RANK, width)), full((MLA_KV_RANK, width)), tab, tab, tab],
        out_specs=[pl.BlockSpec((tm, width), lambda i: (i, 0)),
                   pl.BlockSpec((tm, width), lambda i: (i, 0)),
                   pl.BlockSpec((tm, 2 * LANES), lambda i: (i, 0))],
        out_shape=[jax.ShapeDtypeStruct((n, width), BF16),
                   jax.ShapeDtypeStruct((n, width), BF16),
                   jax.ShapeDtypeStruct((n, 2 * LANES), BF16)],
        compiler_params=_cparams(1),
        name="c_proj",
    )(x2, g.reshape(1, d), wq, wkv, wpe, q_norm.reshape(1, -1), kv_norm.reshape(1, -1),
      wqu, wkvu, cos, sm, sp)


def _attn_c_kernel(q_ref, kv_ref, pe_ref, o_ref, kcat_ref, vcat_ref, s_ref, m_ref, acc_ref, *, t):
    qi = pl.program_id(2)
    pw = 2 * LANES

    @pl.when(qi == 0)
    def _():
        lane = lax.broadcasted_iota(I32, kcat_ref.shape, 1)
        is_v = jnp.logical_or(lane < HEAD_DIM, lane >= pw - HEAD_DIM)
        kv = kv_ref[0]
        kcat_ref[...] = jnp.where(is_v, pe_ref[0], kv)
        vcat_ref[...] = jnp.where(is_v, kv, jnp.ones_like(kv))

    _init_softmax_state(m_ref, acc_ref)
    q = q_ref[0]
    first = lax.broadcasted_iota(I32, q.shape, 1) < LANES
    zero = jnp.zeros_like(q)
    q_all = jnp.concatenate([jnp.where(first, q, zero), jnp.where(first, zero, q)], axis=0)

    def rows(kb):
        return pl.ds(pl.multiple_of(kb * t, t), t)

    def scores(kb, c):
        return _dot_nt(q_all, kcat_ref[rows(kb), :])

    def update(s, kb, c):
        _softmax_update(s, vcat_ref[rows(kb), :], m_ref, acc_ref, c)

    _causal_sweep(qi, t, 1, scores, update, s_ref)
    lo = lax.broadcasted_iota(I32, (t, LANES), 1) < HEAD_DIM
    top = acc_ref[0, :t, :LANES]
    bot = acc_ref[0, t:, LANES:]
    o_ref[0] = jnp.where(lo, top / pltpu.roll(top, HEAD_DIM, 1),
                         bot / pltpu.roll(bot, HEAD_DIM, 1)).astype(o_ref.dtype)


def attn_c(q3, kv3, pe3, *, t=512):
    b, s, width = q3.shape
    npair = width // (2 * LANES)
    return pl.pallas_call(
        functools.partial(_attn_c_kernel, t=t),
        grid=(b, npair, s // t),
        in_specs=[pl.BlockSpec((1, t, 2 * LANES), lambda bi, p, qi: (bi, qi, p)),
                  pl.BlockSpec((1, s, 2 * LANES), lambda bi, p, qi: (bi, 0, p)),
                  pl.BlockSpec((1, s, 2 * LANES), lambda bi, p, qi: (bi, 0, 0))],
        out_specs=pl.BlockSpec((1, t, LANES), lambda bi, p, qi: (bi, qi, p)),
        out_shape=jax.ShapeDtypeStruct((b, s, npair * LANES), BF16),
        scratch_shapes=[pltpu.VMEM((s, 2 * LANES), BF16)] * 2 + [pltpu.VMEM((2, 1, 2 * t, t), F32),
                        pltpu.VMEM((1, 2 * t, LANES), F32), pltpu.VMEM((1, 2 * t, 2 * LANES), F32)],
        compiler_params=_cparams(3),
        name="attn_c",
    )(q3, kv3, pe3)


def _attn_d_kernel(q_ref, k_ref, v_ref, o_ref, r_ref, acc_ref, *, t, scale):
    qi = pl.program_id(2)
    q = q_ref[0] * scale
    hw = q.shape[1]
    nh = hw // HEAD_DIM
    head_of_lane = lax.broadcasted_iota(I32, (t, hw), 1) // HEAD_DIM
    zero = jnp.zeros_like(q)
    q_all = jnp.concatenate([jnp.where(head_of_lane == h, q, zero) for h in range(nh)], axis=0)
    r_ref[...] = jnp.zeros(r_ref.shape, F32)
    acc_ref[...] = jnp.zeros(acc_ref.shape, F32)

    def row_in_block(n_rows):
        row = lax.broadcasted_iota(I32, (n_rows, t), 0)
        for _ in range(n_rows // t - 1):
            row = jnp.where(row >= t, row - t, row)
        return row

    ones_ge = jnp.where(row_in_block(2 * t) >= lax.broadcasted_iota(I32, (2 * t, t), 1),
                        1.0, 0.0).astype(BF16)
    strict = row_in_block(nh * t) > lax.broadcasted_iota(I32, (nh * t, t), 1)

    def block(kb, masked):
        off = pl.multiple_of(kb * t, t)
        z = _dot_nt(q_all, k_ref[0, pl.ds(off, t), :])
        lg = jnp.log2(1.0 + jnp.exp2(jnp.abs(z) * (-LOG2E))) * LN2
        ls = jnp.minimum(z, 0.0) - lg
        lk = ls - z
        if masked:
            lk = jnp.where(strict, lk, 0.0)
        hi = lk.astype(BF16)
        lw = (lk - hi.astype(F32)).astype(BF16)
        csum = _dot(jnp.concatenate([hi, lw], axis=1), ones_ge)
        r_prev = r_ref[...]
        a = jnp.exp(ls + (_lane_tile(r_prev, t // LANES) + (csum - lk)))
        if masked:
            a = jnp.where(strict, a, 0.0)
        acc_ref[...] = acc_ref[...] + _dot(a.astype(BF16), v_ref[0, pl.ds(off, t), :])
        r_ref[...] = r_prev + csum[:, 0:1]

    def live():
        return jnp.max(r_ref[...]) >= EXP_UNDERFLOW

    def cond(carry):
        it, alive = carry
        return jnp.logical_and(it < qi, alive)

    def body(carry):
        it, _ = carry
        block(qi - 1 - it, False)
        return it + 1, live()

    @pl.when(qi == 0)
    def _():
        block(qi, True)

    @pl.when(qi > 0)
    def _():
        block(qi, True)
        block(qi - 1, False)
        lax.while_loop(cond, body, (jnp.int32(1), live()))

    o = acc_ref[:t]
    for h in range(1, nh):
        o = jnp.where(head_of_lane == h, acc_ref[h * t:(h + 1) * t], o)
    o_ref[0] = o.astype(o_ref.dtype)


def attn_d(qkv3, *, t=256):
    b, s, width = qkv3.shape
    hw = MXU_DIM
    ngrp = width // (3 * hw)
    nh = hw // HEAD_DIM
    return pl.pallas_call(
        functools.partial(_attn_d_kernel, t=t, scale=HEAD_DIM ** -0.5),
        grid=(b, ngrp, s // t),
        in_specs=[pl.BlockSpec((1, t, hw), lambda bi, p, qi: (bi, qi, p)),
                  pl.BlockSpec((1, s, hw), lambda bi, p, qi: (bi, 0, ngrp + p)),
                  pl.BlockSpec((1, s, hw), lambda bi, p, qi: (bi, 0, 2 * ngrp + p))],
        out_specs=pl.BlockSpec((1, t, hw), lambda bi, p, qi: (bi, qi, p)),
        out_shape=jax.ShapeDtypeStruct((b, s, ngrp * hw), BF16),
        scratch_shapes=[pltpu.VMEM((nh * t, LANES), F32), pltpu.VMEM((nh * t, hw), F32)],
        compiler_params=_cparams(3),
        name="attn_d",
    )(qkv3, qkv3, qkv3)


def _silu(x):
    return x / (1.0 + jnp.exp(-x))


def _ffn_kernel(x_ref, g_ref, wg_ref, wu_ref, wd_ref, out_ref, *, ff_chunk):
    x = x_ref[...]
    xn = _rms(x, g_ref[...], NORM_EPS).astype(BF16)
    y = x
    for c in range(0, wg_ref.shape[1], ff_chunk):
        h = _silu(_dot(xn, wg_ref[:, c:c + ff_chunk])) * _dot(xn, wu_ref[:, c:c + ff_chunk])
        y = y + _dot(h.astype(BF16), wd_ref[c:c + ff_chunk, :])
    out_ref[...] = y


def ffn(x2, g, wg, wu, wd, *, tm=512):
    n, d = x2.shape
    ff = wg.shape[1]
    ff_chunk = ff // 2 if (ff // 2) % LANES == 0 else ff
    const = lambda shape: pl.BlockSpec(shape, lambda i: (0, 0), pipeline_mode=pl.Buffered(1))
    return pl.pallas_call(
        functools.partial(_ffn_kernel, ff_chunk=ff_chunk),
        grid=(n // tm,),
        in_specs=[pl.BlockSpec((tm, d), lambda i: (i, 0)),
                  pl.BlockSpec((1, d), lambda i: (0, 0)),
                  const((d, ff)), const((d, ff)), const((ff, d))],
        out_specs=pl.BlockSpec((tm, d), lambda i: (i, 0)),
        out_shape=jax.ShapeDtypeStruct((n, d), F32),
        compiler_params=_cparams(1),
        name="ffn",
    )(x2, g.reshape(1, d), wg, wu, wd)


def _router_kernel(x_ref, g_ref, wr_ref, meta_ref, gate_ref, cnt_ref, run_ref):
    i = pl.program_id(0)
    tm = x_ref.shape[0]

    @pl.when(i == 0)
    def _():
        run_ref[...] = jnp.zeros(run_ref.shape, F32)

    xn = _rms(x_ref[...], g_ref[...], NORM_EPS)
    xh = xn.astype(BF16)
    xl = (xn - xh.astype(F32)).astype(BF16)
    logits = _dot(xh, wr_ref[0]) + _dot(xl, wr_ref[0]) + _dot(xh, wr_ref[1])
    lane = lax.broadcasted_iota(I32, (tm, LANES), 1)
    lg = jnp.where(lane < N_EXPERTS, logits, NEG_INF)
    m1 = jnp.max(lg, axis=1, keepdims=True)
    e1 = jnp.min(jnp.where(lg == m1, lane, LANES), axis=1, keepdims=True)
    lg2 = jnp.where(lane == e1, NEG_INF, lg)
    m2 = jnp.max(lg2, axis=1, keepdims=True)
    e2 = jnp.min(jnp.where(lg2 == m2, lane, LANES), axis=1, keepdims=True)
    ex = jnp.exp(m2 - m1)
    g1 = 1.0 / (1.0 + ex)
    g2 = ex / (1.0 + ex)
    hit1 = lane == e1
    hit2 = lane == e2
    onehot = jnp.where(jnp.logical_or(hit1, hit2), 1.0, 0.0)
    earlier = (lax.broadcasted_iota(I32, (tm, tm), 0) > lax.broadcasted_iota(I32, (tm, tm), 1))
    before = _dot(jnp.where(earlier, 1.0, 0.0).astype(BF16), onehot.astype(BF16)) + run_ref[0:1, :]
    pos1 = jnp.sum(jnp.where(hit1, before, 0.0), axis=1, keepdims=True).astype(I32)
    pos2 = jnp.sum(jnp.where(hit2, before, 0.0), axis=1, keepdims=True).astype(I32)
    run_ref[...] = run_ref[...] + jnp.sum(onehot, axis=0, keepdims=True)
    meta_ref[...] = jnp.where(lane == 0, e1, jnp.where(lane == 1, e2, jnp.where(
        lane == 2, pos1, jnp.where(lane == 3, pos2, 0))))
    gate_ref[...] = jnp.where(lane == 0, g1, jnp.where(lane == 1, g2, 0.0))
    cnt_ref[...] = run_ref[...]


def router(x2, g, w_router, *, tm=512):
    n, d = x2.shape
    wr = jnp.zeros((d, LANES), F32).at[:, :N_EXPERTS].set(w_router)
    wr_hi = wr.astype(BF16)
    wr = jnp.stack([wr_hi, (wr - wr_hi.astype(F32)).astype(BF16)])
    row = lambda w: pl.BlockSpec((tm, w), lambda i: (i, 0))
    return pl.pallas_call(
        _router_kernel,
        grid=(n // tm,),
        in_specs=[row(d), pl.BlockSpec((1, d), lambda i: (0, 0)),
                  pl.BlockSpec((2, d, LANES), lambda i: (0, 0, 0))],
        out_specs=[row(LANES), row(LANES), pl.BlockSpec((8, LANES), lambda i: (0, 0))],
        out_shape=[jax.ShapeDtypeStruct((n, LANES), I32),
                   jax.ShapeDtypeStruct((n, LANES), F32), jax.ShapeDtypeStruct((8, LANES), F32)],
        scratch_shapes=[pltpu.VMEM((8, LANES), F32)],
        compiler_params=_cparams(1),
        name="router",
    )(x2, g.reshape(1, d), wr)


def _row_copy(src_ref, src_row, dst_ref, dst_row, sem):
    return pltpu.make_async_copy(src_ref.at[pl.ds(src_row, 1)], dst_ref.at[pl.ds(dst_row, 1)], sem)


def _dispatch_kernel(dest_ref, pend_ref, x_ref, g_ref, xb_ref, xn_ref, sem, *, tb):
    @pl.when(pl.program_id(0) == 0)
    def _():
        xn_ref[...] = jnp.zeros(xn_ref.shape, F32)

        def zero_block(row):
            return pltpu.make_async_copy(xn_ref, xb_ref.at[pl.ds(pl.multiple_of(row, tb), tb)], sem)

        def has_rows(e):
            return pend_ref[e] > (pend_ref[e - 1] if e > 0 else 0)

        total = pend_ref[N_EXPERTS - 1]
        n_rows = xb_ref.shape[0]
        for start in (True, False):
            for e in range(N_EXPERTS):
                @pl.when(has_rows(e))
                def _():
                    cp = zero_block(pend_ref[e] - tb)
                    cp.start() if start else cp.wait()

                @pl.when(total + e * tb < n_rows)
                def _():
                    cp = zero_block(total + e * tb)
                    cp.start() if start else cp.wait()

    xn_ref[...] = _rms(x_ref[...], g_ref[...], NORM_EPS)

    def issue(tt, carry):
        for k in range(2):
            _row_copy(xn_ref, tt, xb_ref, dest_ref[2 * tt + k], sem).start()
        return carry

    lax.fori_loop(0, tb, issue, 0, unroll=8)
    for k in range(2):
        pltpu.make_async_copy(xn_ref, xb_ref.at[pl.ds(0, tb)], sem).wait()


def dispatch(dest_flat, pad_end, x2, g, n_rows):
    n, d = x2.shape
    tb = MOE_ROWS
    return pl.pallas_call(
        functools.partial(_dispatch_kernel, tb=tb),
        grid=(n // tb,),
        in_specs=[pl.BlockSpec((2 * tb,), lambda i: (i,), memory_space=pltpu.SMEM),
                  pl.BlockSpec(memory_space=pltpu.SMEM),
                  pl.BlockSpec((tb, d), lambda i: (i, 0)),
                  pl.BlockSpec((1, d), lambda i: (0, 0))],
        out_specs=pl.BlockSpec(memory_space=pl.ANY),
        out_shape=jax.ShapeDtypeStruct((n_rows, d), F32),
        scratch_shapes=[pltpu.VMEM((tb, d), F32), pltpu.SemaphoreType.DMA(())],
        compiler_params=_cparams(1),
        name="moe_dispatch",
    )(dest_flat, pad_end.astype(I32), x2, g.reshape(1, d))


def _expert_kernel(be_ref, nu_ref, x_ref, wg_ref, wu_ref, wd_ref, y_ref, *, ff_chunk):
    i = pl.program_id(0)

    @pl.when(i < nu_ref[0])
    def _():
        xb = x_ref[...].astype(BF16)
        y = jnp.zeros(y_ref.shape, F32)
        for c in range(0, wg_ref.shape[2], ff_chunk):
            h = (_silu(_dot(xb, wg_ref[0, :, c:c + ff_chunk]))
                 * _dot(xb, wu_ref[0, :, c:c + ff_chunk]))
            y = y + _dot(h.astype(BF16), wd_ref[0, c:c + ff_chunk, :])
        y_ref[...] = y

    @pl.when(i >= nu_ref[0])
    def _():
        y_ref[...] = jnp.zeros(y_ref.shape, F32)


def experts(block_e, n_used, xb, wg, wu, wd, *, ff_chunk=512):
    rows, d = xb.shape
    ff = wg.shape[2]
    assert ff % ff_chunk == 0
    wspec = lambda shape: pl.BlockSpec(shape, lambda i, be, nu: (be[i], 0, 0),
                                       pipeline_mode=pl.Buffered(1))
    return pl.pallas_call(
        functools.partial(_expert_kernel, ff_chunk=ff_chunk),
        grid_spec=pltpu.PrefetchScalarGridSpec(
            num_scalar_prefetch=2,
            grid=(rows // MOE_ROWS,),
            in_specs=[pl.BlockSpec((MOE_ROWS, d), lambda i, be, nu: (i, 0)),
                      wspec((1, d, ff)), wspec((1, d, ff)), wspec((1, ff, d))],
            out_specs=pl.BlockSpec((MOE_ROWS, d), lambda i, be, nu: (i, 0)),
        ),
        out_shape=jax.ShapeDtypeStruct((rows, d), F32),
        compiler_params=_cparams(1),
        name="moe_experts",
    )(block_e, n_used, xb, wg, wu, wd)


def _combine_kernel(dest_ref, x_ref, gate_ref, fg_ref, yb_ref, out_ref, buf_ref, sem, *, tb, final):
    def issue(tt, carry):
        for k in range(2):
            _row_copy(yb_ref, dest_ref[2 * tt + k], buf_ref.at[k], tt, sem).start()
        return carry

    lax.fori_loop(0, tb, issue, 0, unroll=8)
    for k in range(2):
        pltpu.make_async_copy(yb_ref.at[pl.ds(0, tb)], buf_ref.at[k], sem).wait()
    gate = gate_ref[...]
    y = x_ref[...] + gate[:, 0:1] * buf_ref[0] + gate[:, 1:2] * buf_ref[1]
    if final:
        y = _rms(y, fg_ref[...], NORM_EPS)
    out_ref[...] = y


def combine(dest_flat, x2, gates, yb, final_g, *, final, tb=512):
    n, d = x2.shape
    return pl.pallas_call(
        functools.partial(_combine_kernel, tb=tb, final=final),
        grid=(n // tb,),
        in_specs=[pl.BlockSpec((2 * tb,), lambda i: (i,), memory_space=pltpu.SMEM),
                  pl.BlockSpec((tb, d), lambda i: (i, 0)),
                  pl.BlockSpec((tb, LANES), lambda i: (i, 0)),
                  pl.BlockSpec((1, d), lambda i: (0, 0)),
                  pl.BlockSpec(memory_space=pl.ANY)],
        out_specs=pl.BlockSpec((tb, d), lambda i: (i, 0)),
        out_shape=jax.ShapeDtypeStruct((n, d), F32),
        scratch_shapes=[pltpu.VMEM((2, tb, d), F32), pltpu.SemaphoreType.DMA(())],
        compiler_params=_cparams(1),
        name="moe_combine",
    )(dest_flat, x2, gates, final_g.reshape(1, d), yb)


def moe(x2, g, w_router, wg, wu, wd, final_g, *, final):
    n, d = x2.shape
    meta, gates, cnt = router(x2, g, w_router)
    counts = cnt[0, :N_EXPERTS].astype(I32)
    padded = (counts + MOE_ROWS - 1) // MOE_ROWS * MOE_ROWS
    pad_end = jnp.cumsum(padded)
    pad_start = pad_end - padded
    dest = (pad_start[meta[:, 0:2]] + meta[:, 2:4]).reshape(-1).astype(I32)
    n_blocks = -(-2 * n // MOE_ROWS) + N_EXPERTS
    starts = jnp.arange(n_blocks, dtype=I32) * MOE_ROWS
    block_e = jnp.minimum(jnp.sum((pad_end[None, :] <= starts[:, None]).astype(I32), axis=1),
                          N_EXPERTS - 1)
    n_used = (pad_end[-1:] // MOE_ROWS).astype(I32)
    xb = dispatch(dest, pad_end, x2, g, n_blocks * MOE_ROWS)
    yb = experts(block_e, n_used, xb, wg.astype(BF16), wu.astype(BF16), wd.astype(BF16))
    return combine(dest, x2, gates, yb, final_g, final=final)


def _bf16(w):
    return w.astype(BF16)


def mixer_a(x2, b, norm, w_qkv, w_o):
    d = x2.shape[1]
    n_groups = len(A_PATTERN)
    dils = tuple(dil for _, dil in A_PATTERN)
    gw = A_HEADS * HEAD_DIM
    w = w_qkv.reshape(d, 3, n_groups, gw)
    w = w * jnp.asarray([HEAD_DIM ** -0.5 * LOG2E, 1.0, 1.0], F32).reshape(1, 3, 1, 1)
    w = _bf16(w.transpose(0, 2, 1, 3).reshape(d, n_groups * 3 * gw))
    views = a_proj(x2, norm, w, dils)
    slopes = 2.0 ** (-8.0 * np.arange(1, n_groups * A_HEADS + 1) / (n_groups * A_HEADS))
    outs, lses = [], []
    for gi, (window, dil) in enumerate(A_PATTERN):
        assert window // dil == QB
        o, lse = attn_a_group(views[gi], b, dil, slopes[gi * A_HEADS:(gi + 1) * A_HEADS])
        outs.append(o)
        lses.append(lse)
    return mix_oproj_res(x2, outs, lses, _bf16(w_o), dils)


def mixer_b(x2, b, norm, w_qkv, lq1, lk1, lq2, lk2, subln, w_o, *, layer, t=512):
    n = x2.shape[0]
    n_q = w_qkv.shape[1] // 3
    w = jnp.concatenate([w_qkv[:, :n_q] * (HEAD_DIM ** -0.5 * LOG2E), w_qkv[:, n_q:]], axis=1)
    qkv = norm_proj(x2, norm, _bf16(w)).reshape(b, n // b, -1)
    o = attn_b(qkv, lq1, lk1, lq2, lk2, subln, layer, t=t)
    return oproj_res(x2, o.reshape(n, -1), _bf16(w_o))


def mixer_c(x2, b, norm, w_in, q_norm, w_q_up, kv_norm, w_kv_up, w_o, *, t=512):
    n = x2.shape[0]
    s = n // b
    q, kv, pe = c_proj(x2, s, norm, w_in, q_norm, w_q_up, kv_norm, w_kv_up)
    o = attn_c(q.reshape(b, s, -1), kv.reshape(b, s, -1), pe.reshape(b, s, -1), t=t)
    return oproj_res(x2, o.reshape(n, -1), _bf16(w_o))


def mixer_d(x2, b, norm, w_qkv, w_o, *, t=256):
    n = x2.shape[0]
    qkv = norm_proj(x2, norm, _bf16(w_qkv)).reshape(b, n // b, -1)
    o = attn_d(qkv, t=t)
    return oproj_res(x2, o.reshape(n, -1), _bf16(w_o))
def kernel(x, a_norm, a_w_qkv, a_w_o, ffn0_norm, ffn0_w_gate, ffn0_w_up, ffn0_w_down, b_norm, b_w_qkv, b_lambda_q1, b_lambda_k1, b_lambda_q2, b_lambda_k2, b_subln, b_w_o, moe1_norm, moe1_w_router, moe1_w_gate, moe1_w_up, moe1_w_down, c_norm, c_w_in, c_q_norm, c_w_q_up, c_kv_norm, c_w_kv_up, c_w_o, ffn2_norm, ffn2_w_gate, ffn2_w_up, ffn2_w_down, d_norm, d_w_qkv, d_w_o, moe3_norm, moe3_w_router, moe3_w_gate, moe3_w_up, moe3_w_down, final_norm):
    b, s, d = x.shape
    x2 = x.reshape(b * s, d)
    bf = _bf16
    x2 = mixer_a(x2, b, a_norm, a_w_qkv, a_w_o)
    x2 = ffn(x2, ffn0_norm, bf(ffn0_w_gate), bf(ffn0_w_up), bf(ffn0_w_down))
    x2 = mixer_b(x2, b, b_norm, b_w_qkv, b_lambda_q1, b_lambda_k1, b_lambda_q2, b_lambda_k2,
                 b_subln, b_w_o, layer=1)
    x2 = moe(x2, moe1_norm, moe1_w_router, moe1_w_gate, moe1_w_up, moe1_w_down, final_norm,
             final=False)
    x2 = mixer_c(x2, b, c_norm, c_w_in, c_q_norm, c_w_q_up, c_kv_norm, c_w_kv_up, c_w_o)
    x2 = ffn(x2, ffn2_norm, bf(ffn2_w_gate), bf(ffn2_w_up), bf(ffn2_w_down))
    x2 = mixer_d(x2, b, d_norm, d_w_qkv, d_w_o)
    x2 = moe(x2, moe3_norm, moe3_w_router, moe3_w_gate, moe3_w_up, moe3_w_down, final_norm,
             final=True)
    return x2.reshape(b, s, d)
```

```python
import functools
import math

import numpy as np
import jax
import jax.numpy as jnp
from jax import lax
from jax.experimental import pallas as pl
from jax.experimental.pallas import tpu as pltpu

F32 = jnp.float32
BF16 = jnp.bfloat16
I32 = jnp.int32

LANES = 128
MXU_DIM = 256
HEAD_DIM = 64
NORM_EPS = 1e-6
SUBLN_EPS = 1e-5
ROPE_THETA = 10000.0
A_PATTERN = ((128, 1), (512, 4), (2048, 16))
A_HEADS = 8
QB = 128
A_SUB = 2
N_EXPERTS = 8
MOE_ROWS = 512
VMEM_LIMIT = 56 * 1024 * 1024
NEG_INF = float("-inf")
EXP_UNDERFLOW = -105.0


def _cparams(n_axes, vmem=VMEM_LIMIT):
    return pltpu.CompilerParams(dimension_semantics=("arbitrary",) * n_axes,
                                vmem_limit_bytes=vmem)


def _dot(a, b):
    return jnp.dot(a, b, preferred_element_type=F32)


def _dot_nt(a, b):
    return lax.dot_general(a, b, (((1,), (1,)), ((), ())), preferred_element_type=F32)


def _rms(x, g, eps):
    return x * lax.rsqrt(jnp.mean(x * x, axis=-1, keepdims=True) + eps) * g


def _lane_tile(x, n):
    return x if n == 1 else jnp.concatenate([x] * n, axis=1)


def _norm_proj_kernel(x_ref, g_ref, w_ref, o_ref, *, col_chunk):
    xn = _rms(x_ref[...], g_ref[...], NORM_EPS).astype(BF16)
    n = o_ref.shape[1]
    for c in range(0, n, col_chunk):
        o_ref[:, c:c + col_chunk] = _dot(xn, w_ref[:, c:c + col_chunk]).astype(o_ref.dtype)


def norm_proj(x2, g, w, *, tm=512, col_chunk=512):
    n, d = x2.shape
    n_out = w.shape[1]
    assert n % tm == 0 and n_out % col_chunk == 0
    return pl.pallas_call(
        functools.partial(_norm_proj_kernel, col_chunk=col_chunk),
        grid=(n // tm,),
        in_specs=[pl.BlockSpec((tm, d), lambda i: (i, 0)),
                  pl.BlockSpec((1, d), lambda i: (0, 0)),
                  pl.BlockSpec((d, n_out), lambda i: (0, 0))],
        out_specs=pl.BlockSpec((tm, n_out), lambda i: (i, 0)),
        out_shape=jax.ShapeDtypeStruct((n, n_out), BF16),
        compiler_params=_cparams(1),
        name="norm_proj",
    )(x2, g.reshape(1, d), w)


def _oproj_kernel(x_ref, o_ref, w_ref, out_ref):
    out_ref[...] = x_ref[...] + _dot(o_ref[...], w_ref[...])


def oproj_res(x2, o2, w, *, tm=1024):
    n, d = x2.shape
    k = o2.shape[1]
    return pl.pallas_call(
        _oproj_kernel,
        grid=(n // tm,),
        in_specs=[pl.BlockSpec((tm, d), lambda i: (i, 0)),
                  pl.BlockSpec((tm, k), lambda i: (i, 0)),
                  pl.BlockSpec((k, d), lambda i: (0, 0))],
        out_specs=pl.BlockSpec((tm, d), lambda i: (i, 0)),
        out_shape=jax.ShapeDtypeStruct((n, d), F32),
        compiler_params=_cparams(1),
        name="oproj_res",
    )(x2, o2, w)


LOG2E = math.log2(math.e)
LN2 = math.log(2.0)


def _a_proj_kernel(x_ref, g_ref, w_ref, *rest, dils, gcols):
    outs, scr = rest[:len(dils)], rest[len(dils)]
    xn = _rms(x_ref[...], g_ref[...], NORM_EPS).astype(BF16)
    tm = x_ref.shape[0]
    for gi, dil in enumerate(dils):
        res = _dot(xn, w_ref[:, gi * gcols:(gi + 1) * gcols])
        if dil == 1:
            outs[gi][...] = res.astype(BF16)
        else:
            for c in range(gcols // LANES):
                scr[c] = res[:, c * LANES:(c + 1) * LANES]
            for r in range(dil):
                for c in range(gcols // LANES):
                    col = r * gcols + c * LANES
                    outs[gi][:, col:col + LANES] = (
                        scr[c, pl.ds(r, tm // dil, stride=dil), :].astype(BF16))


def a_proj(x2, g, w, dils, *, tm=512):
    n, d = x2.shape
    gcols = w.shape[1] // len(dils)
    return pl.pallas_call(
        functools.partial(_a_proj_kernel, dils=dils, gcols=gcols),
        grid=(n // tm,),
        in_specs=[pl.BlockSpec((tm, d), lambda i: (i, 0)),
                  pl.BlockSpec((1, d), lambda i: (0, 0)),
                  pl.BlockSpec((d, w.shape[1]), lambda i: (0, 0))],
        out_specs=[pl.BlockSpec((tm // dil, dil * gcols), lambda i: (i, 0)) for dil in dils],
        out_shape=[jax.ShapeDtypeStruct((n // dil, dil * gcols), BF16) for dil in dils],
        scratch_shapes=[pltpu.VMEM((gcols // LANES, tm, LANES), F32)],
        compiler_params=_cparams(1),
        name="a_proj",
    )(x2, g.reshape(1, d), w)


def _attn_a_kernel(q_ref, kc_ref, kp_ref, vc_ref, vp_ref, bc_ref, bp_ref, o_ref, lse_ref):
    first_step = pl.program_id(2) == 0
    hw = MXU_DIM
    nh = hw // HEAD_DIM
    head_of_lane = lax.broadcasted_iota(I32, (QB, hw), 1) // HEAD_DIM
    for sub in range(q_ref.shape[1] // QB):
        own = slice(sub * QB, (sub + 1) * QB)
        before = slice((sub - 1) * QB, sub * QB)
        for half in range(q_ref.shape[2] // hw):
            sl = slice(half * hw, (half + 1) * hw)
            tab = slice(half * nh * QB, (half + 1) * nh * QB)
            q = q_ref[0, own, sl]
            kp, vp = ((kp_ref[0, :, sl], vp_ref[0, :, sl]) if sub == 0
                      else (kc_ref[0, before, sl], vc_ref[0, before, sl]))
            zero = jnp.zeros_like(q)
            q_all = jnp.concatenate([jnp.where(head_of_lane == h, q, zero) for h in range(nh)],
                                    axis=0)
            sc = _dot_nt(q_all, kc_ref[0, own, sl]) + bc_ref[tab, :]
            sp = _dot_nt(q_all, kp) + bp_ref[tab, :]
            if sub == 0:
                sp = jnp.where(first_step, NEG_INF, sp)
            m = jnp.maximum(jnp.max(sc, axis=1, keepdims=True), jnp.max(sp, axis=1, keepdims=True))
            pc = jnp.exp2(sc - m)
            pp = jnp.exp2(sp - m)
            l = jnp.sum(pc, axis=1, keepdims=True) + jnp.sum(pp, axis=1, keepdims=True)
            acc = _dot(pc.astype(BF16), vc_ref[0, own, sl]) + _dot(pp.astype(BF16), vp)
            lse = (m + jnp.log2(l)) * LN2
            o = jnp.zeros((QB, hw), F32)
            ls = jnp.zeros((QB, hw), F32)
            for h in range(nh):
                rows = slice(h * QB, (h + 1) * QB)
                mine = head_of_lane == h
                o = jnp.where(mine, acc[rows] / l[rows], o)
                ls = jnp.where(mine, lse[rows], ls)
            o_ref[0, own, sl] = o.astype(o_ref.dtype)
            lse_ref[0, own, sl] = ls


def _a_bias_tables(dil, slopes):
    rel_c = np.arange(QB)[:, None] - np.arange(QB)[None, :]
    rel_p = rel_c + QB
    sl = np.asarray(slopes, np.float64)[:, None, None] * dil * LOG2E
    bc = np.where(rel_c >= 0, -sl * rel_c, -np.inf).reshape(-1, QB)
    bp = np.where(rel_p <= QB, -sl * rel_p, -np.inf).reshape(-1, QB)
    return jnp.asarray(bc, F32), jnp.asarray(bp, F32)


def attn_a_group(view2, b, dil, slopes):
    gw = A_HEADS * HEAD_DIM
    l = view2.shape[0] // b
    tq = A_SUB * QB
    assert l % tq == 0 and view2.shape[1] == dil * 3 * gw
    view = view2.reshape(b, l, dil * 3 * gw)
    bc, bp = _a_bias_tables(dil, slopes)

    def spec(which, prev):
        if prev:
            return pl.BlockSpec((1, QB, gw), lambda bi, r, j: (
                bi, jnp.maximum(j * A_SUB - 1, 0), r * 3 + which))
        return pl.BlockSpec((1, tq, gw), lambda bi, r, j: (bi, j, r * 3 + which))

    tab = pl.BlockSpec((A_HEADS * QB, QB), lambda bi, r, j: (0, 0))
    o, lse = pl.pallas_call(
        _attn_a_kernel,
        grid=(b, dil, l // tq),
        in_specs=[spec(0, False), spec(1, False), spec(1, True), spec(2, False), spec(2, True),
                  tab, tab],
        out_specs=[pl.BlockSpec((1, tq, gw), lambda bi, r, j: (bi, j, r)),
                   pl.BlockSpec((1, tq, gw), lambda bi, r, j: (bi, j, r))],
        out_shape=[jax.ShapeDtypeStruct((b, l, dil * gw), BF16),
                   jax.ShapeDtypeStruct((b, l, dil * gw), F32)],
        compiler_params=_cparams(3),
        name=f"attn_a_d{dil}",
    )(view, view, view, view, view, bc, bp)
    return o.reshape(b * l, dil * gw), lse.reshape(b * l, dil * gw)


def _mix_oproj_kernel(x_ref, *rest, dils, gw):
    ng = len(dils)
    o_refs, l_refs = rest[:ng], rest[ng:2 * ng]
    w_ref, out_ref, o_scr, l_scr = rest[2 * ng:]
    tm = x_ref.shape[0]
    os_, ls = [], []
    for gi, dil in enumerate(dils):
        if dil == 1:
            os_.append(o_refs[gi][...].astype(F32))
            ls.append(l_refs[gi][...])
        else:
            nct = gw // LANES
            for r in range(dil):
                rows = pl.ds(r, tm // dil, stride=dil)
                for c in range(nct):
                    cols = slice(r * gw + c * LANES, r * gw + (c + 1) * LANES)
                    o_scr[gi, c, rows, :] = o_refs[gi][:, cols].astype(F32)
                    l_scr[gi, c, rows, :] = l_refs[gi][:, cols]
            os_.append(jnp.concatenate([o_scr[gi, c] for c in range(nct)], axis=1))
            ls.append(jnp.concatenate([l_scr[gi, c] for c in range(nct)], axis=1))
    m = functools.reduce(jnp.maximum, ls)
    es = [jnp.exp(v - m) for v in ls]
    den = functools.reduce(lambda a, c: a + c, es)
    mixed = functools.reduce(lambda a, c: a + c, [(e / den) * o for e, o in zip(es, os_)])
    out_ref[...] = x_ref[...] + _dot(mixed.astype(BF16), w_ref[...])


def mix_oproj_res(x2, outs, lses, w, dils, *, tm=1024):
    n, d = x2.shape
    gw = w.shape[0]
    ng = len(dils)
    views = [pl.BlockSpec((tm // dil, dil * gw), lambda i: (i, 0)) for dil in dils]
    return pl.pallas_call(
        functools.partial(_mix_oproj_kernel, dils=dils, gw=gw),
        grid=(n // tm,),
        in_specs=[pl.BlockSpec((tm, d), lambda i: (i, 0))] + views + views
                 + [pl.BlockSpec((gw, d), lambda i: (0, 0))],
        out_specs=pl.BlockSpec((tm, d), lambda i: (i, 0)),
        out_shape=jax.ShapeDtypeStruct((n, d), F32),
        scratch_shapes=[pltpu.VMEM((ng, gw // LANES, tm, LANES), F32)] * 2,
        compiler_params=_cparams(1),
        name="mix_oproj_res",
    )(x2, *outs, *lses, w)


def _softmax_update(s, v, m_ref, acc_ref, idx):
    tk = s.shape[1]
    m_prev = m_ref[idx]
    m_next = jnp.maximum(m_prev, jnp.max(s, axis=1, keepdims=True))
    p = jnp.exp2(s - _lane_tile(m_next, tk // LANES))
    alpha = jnp.exp2(m_prev - m_next)
    acc_ref[idx] = (_lane_tile(alpha, acc_ref.shape[-1] // LANES) * acc_ref[idx]
                    + _dot(p.astype(BF16), v))
    m_ref[idx] = m_next


def _init_softmax_state(m_ref, acc_ref):
    m_ref[...] = jnp.full(m_ref.shape, NEG_INF, F32)
    acc_ref[...] = jnp.zeros(acc_ref.shape, F32)


def _causal_sweep(qi, t, n_chain, scores, update, s_ref):
    def qk(kb, buf):
        for c in range(n_chain):
            s_ref[buf, c] = scores(kb, c)

    def upd(kb, buf, masked):
        for c in range(n_chain):
            s = s_ref[buf, c]
            if masked:
                s = jnp.where(_causal_mask(s.shape[0], t), s, NEG_INF)
            update(s, kb, c)

    qk(0, 0)

    def body(i, carry):
        kb = 2 * i
        qk(kb + 1, 1)
        upd(kb, 0, False)
        qk(kb + 2, 0)
        upd(kb + 1, 1, False)
        return carry

    lax.fori_loop(0, lax.shift_right_logical(qi, 1), body, 0)
    odd = jnp.bitwise_and(qi, 1) == 1

    @pl.when(odd)
    def _():
        qk(qi, 1)
        upd(qi - 1, 0, False)
        upd(qi, 1, True)

    @pl.when(jnp.logical_not(odd))
    def _():
        upd(qi, 0, True)


def _split3(x):
    hi = x.astype(BF16).astype(F32)
    mid = (x - hi).astype(BF16).astype(F32)
    lo = (x - hi - mid).astype(BF16).astype(F32)
    return hi, mid, lo


def _causal_mask(n_rows, t):
    row = lax.broadcasted_iota(I32, (n_rows, t), 0)
    for _ in range(n_rows // t - 1):
        row = jnp.where(row >= t, row - t, row)
    return row >= lax.broadcasted_iota(I32, (n_rows, t), 1)


def _attn_b_kernel(q_ref, k_ref, v_ref, lq1, lk1, lq2, lk2, g_ref, o_ref, kb_ref, vb_ref, s_ref,
                   m_ref, acc_ref, *, t, slopes, lam_init):
    h = pl.program_id(1)
    qi = pl.program_id(2)
    n_bias = 3

    @pl.when(qi == 0)
    def _():
        slope = jnp.float32(0.0)
        for hh, sv in enumerate(slopes):
            slope = jnp.where(h == hh, jnp.float32(sv), slope)
        shape = (k_ref.shape[1], LANES)
        lane = lax.broadcasted_iota(I32, shape, 1)
        parts = _split3(slope * lax.broadcasted_iota(I32, shape, 0).astype(F32))
        bias = jnp.zeros(shape, F32)
        for j, part in enumerate(parts):
            bias = jnp.where(lane == j, part, bias)
        kb_ref[:, :LANES] = k_ref[0]
        kb_ref[:, LANES:] = bias.astype(BF16)
        vb_ref[:, :LANES] = v_ref[0]
        vb_ref[:, LANES:] = jnp.ones(shape, BF16)

    q = q_ref[0].astype(F32)
    lane = lax.broadcasted_iota(I32, (t, LANES), 1)
    ones = jnp.where(lane < n_bias, 1.0, 0.0)
    q_all = jnp.concatenate(
        [jnp.concatenate([jnp.where(lane < HEAD_DIM, q, 0.0), ones], axis=1),
         jnp.concatenate([jnp.where(lane >= HEAD_DIM, q, 0.0), ones], axis=1)], axis=0).astype(BF16)
    _init_softmax_state(m_ref, acc_ref)

    def rows(kb):
        return pl.ds(pl.multiple_of(kb * t, t), t)

    def scores(kb, c):
        return _dot_nt(q_all, kb_ref[rows(kb), :])

    def update(s, kb, c):
        _softmax_update(s, vb_ref[rows(kb), :], m_ref, acc_ref, c)

    _causal_sweep(qi, t, 1, scores, update, s_ref)

    lam = (jnp.exp(jnp.sum(lq1[...] * lk1[...], axis=1, keepdims=True))
           - jnp.exp(jnp.sum(lq2[...] * lk2[...], axis=1, keepdims=True)) + lam_init)
    acc = acc_ref[0]
    o = acc[:t, :LANES] / acc[:t, LANES:] - lam * (acc[t:, :LANES] / acc[t:, LANES:])
    o = _rms(o, g_ref[...], SUBLN_EPS) * (1.0 - lam_init)
    o_ref[0] = o.astype(o_ref.dtype)


def attn_b(qkv3, lq1, lk1, lq2, lk2, subln, layer, *, t=512):
    b, s, width = qkv3.shape
    nh = width // (3 * LANES)
    slopes = tuple(float(2.0 ** (-8.0 * (i + 1) / nh)) * LOG2E for i in range(nh))
    lam_init = 0.8 - 0.6 * math.exp(-0.3 * layer)
    vec = pl.BlockSpec((1, HEAD_DIM), lambda bi, h, qi: (0, 0))
    return pl.pallas_call(
        functools.partial(_attn_b_kernel, t=t, slopes=slopes, lam_init=lam_init),
        grid=(b, nh, s // t),
        in_specs=[pl.BlockSpec((1, t, LANES), lambda bi, h, qi: (bi, qi, h)),
                  pl.BlockSpec((1, s, LANES), lambda bi, h, qi: (bi, 0, nh + h)),
                  pl.BlockSpec((1, s, LANES), lambda bi, h, qi: (bi, 0, 2 * nh + h)),
                  vec, vec, vec, vec,
                  pl.BlockSpec((1, LANES), lambda bi, h, qi: (0, 0))],
        out_specs=pl.BlockSpec((1, t, LANES), lambda bi, h, qi: (bi, qi, h)),
        out_shape=jax.ShapeDtypeStruct((b, s, nh * LANES), BF16),
        scratch_shapes=[pltpu.VMEM((s, 2 * LANES), BF16)] * 2 + [pltpu.VMEM((2, 1, 2 * t, t), F32),
                        pltpu.VMEM((1, 2 * t, LANES), F32), pltpu.VMEM((1, 2 * t, 2 * LANES), F32)],
        compiler_params=_cparams(3),
        name="attn_b",
    )(qkv3, qkv3, qkv3, lq1.reshape(1, -1), lk1.reshape(1, -1), lq2.reshape(1, -1),
      lk2.reshape(1, -1), subln.reshape(1, -1))


MLA_HEADS = 16
MLA_Q_RANK = 384
MLA_KV_RANK = 256
MLA_ROPE = 32


def _c_proj_kernel(x_ref, g_ref, wq_ref, wkv_ref, wpe_ref, qn_ref, kvn_ref, wqu_ref, wkvu_ref,
                   cos_ref, sm_ref, sp_ref, q_out, kv_out, pe_out):
    xn = _rms(x_ref[...], g_ref[...], NORM_EPS).astype(BF16)
    q_lat = _dot(xn, wq_ref[...])
    kv_lat = _dot(xn, wkv_ref[...])
    cos, sm, sp = cos_ref[...], sm_ref[...], sp_ref[...]
    pw = cos.shape[1]

    def rope(v):
        return v * cos + pltpu.roll(v, pw - MLA_ROPE // 2, 1) * sm + pltpu.roll(v, MLA_ROPE // 2, 1) * sp

    pe_out[...] = rope(_dot(xn, wpe_ref[...])).astype(pe_out.dtype)
    qn = _rms(q_lat, qn_ref[...], NORM_EPS).astype(BF16)
    for p in range(q_out.shape[1] // pw):
        sl = slice(p * pw, (p + 1) * pw)
        q_out[:, sl] = rope(_dot(qn, wqu_ref[:, sl])).astype(q_out.dtype)
    kvn = _rms(kv_lat, kvn_ref[...], NORM_EPS).astype(BF16)
    for p in range(kv_out.shape[1] // pw):
        sl = slice(p * pw, (p + 1) * pw)
        kv_out[:, sl] = _dot(kvn, wkvu_ref[:, sl]).astype(kv_out.dtype)


def _mla_layout():
    qcols = -np.ones(MLA_HEADS * LANES, np.int64)
    kvcols = np.zeros(MLA_HEADS * LANES, np.int64)
    for h in range(MLA_HEADS):
        qsrc = h * (HEAD_DIM + MLA_ROPE)
        ksrc = h * 2 * HEAD_DIM
        base = h * LANES
        if h % 2 == 0:
            qcols[base:base + MLA_ROPE] = qsrc + HEAD_DIM + np.arange(MLA_ROPE)
            qcols[base + HEAD_DIM:base + LANES] = qsrc + np.arange(HEAD_DIM)
            kvcols[base:base + HEAD_DIM] = ksrc + HEAD_DIM + np.arange(HEAD_DIM)
            kvcols[base + HEAD_DIM:base + LANES] = ksrc + np.arange(HEAD_DIM)
        else:
            qcols[base:base + HEAD_DIM] = qsrc + np.arange(HEAD_DIM)
            qcols[base + HEAD_DIM:base + HEAD_DIM + MLA_ROPE] = qsrc + HEAD_DIM + np.arange(MLA_ROPE)
            kvcols[base:base + HEAD_DIM] = ksrc + np.arange(HEAD_DIM)
            kvcols[base + HEAD_DIM:base + LANES] = ksrc + HEAD_DIM + np.arange(HEAD_DIM)
    pe_offsets = (0, LANES + HEAD_DIM)
    return qcols, kvcols, pe_offsets


def _rope_tables(s, pe_offsets):
    half = MLA_ROPE // 2
    inv_freq = ROPE_THETA ** (-jnp.arange(half, dtype=F32) / half)
    ang = jnp.arange(s, dtype=F32)[:, None] * inv_freq[None, :]
    cos, sin = jnp.cos(ang), jnp.sin(ang)
    c = jnp.ones((s, 2 * LANES), F32)
    sm = jnp.zeros((s, 2 * LANES), F32)
    sp = jnp.zeros((s, 2 * LANES), F32)
    for off in pe_offsets:
        c = c.at[:, off:off + half].set(cos).at[:, off + half:off + 2 * half].set(cos)
        sm = sm.at[:, off:off + half].set(-sin)
        sp = sp.at[:, off + half:off + 2 * half].set(sin)
    return c, sm, sp


def c_proj(x2, s, g, w_in, q_norm, w_q_up, kv_norm, w_kv_up, *, tm=512):
    n, d = x2.shape
    qcols, kvcols, pe_offsets = _mla_layout()
    wq = w_in[:, :MLA_Q_RANK].astype(BF16)
    wkv = w_in[:, MLA_Q_RANK:MLA_Q_RANK + MLA_KV_RANK].astype(BF16)
    w_pe = w_in[:, MLA_Q_RANK + MLA_KV_RANK:]
    wpe = jnp.zeros((d, 2 * LANES), F32)
    for off in pe_offsets:
        wpe = wpe.at[:, off:off + MLA_ROPE].set(w_pe)
    wpe = wpe.astype(BF16)
    q_scale = (HEAD_DIM + MLA_ROPE) ** -0.5 * LOG2E
    wqu = jnp.where(jnp.asarray(qcols >= 0)[None, :],
                    w_q_up[:, np.maximum(qcols, 0)] * q_scale, 0.0).astype(BF16)
    wkvu = w_kv_up[:, kvcols].astype(BF16)
    cos, sm, sp = _rope_tables(s, pe_offsets)
    width = MLA_HEADS * LANES
    assert s % tm == 0
    nsb = s // tm
    full = lambda shape: pl.BlockSpec(shape, lambda i: (0, 0))
    tab = pl.BlockSpec((tm, 2 * LANES), lambda i: (i % nsb, 0))
    return pl.pallas_call(
        _c_proj_kernel,
        grid=(n // tm,),
        in_specs=[pl.BlockSpec((tm, d), lambda i: (i, 0)), full((1, d)),
                  full((d, MLA_Q_RANK)), full((d, MLA_KV_RANK)), full((d, 2 * LANES)),
                  full((1, MLA_Q_RANK)), full((1, MLA_KV_RANK)),
                  full((MLA_Q_RANK, width)), full((MLA_KV_RANK, width)), tab, tab, tab],
        out_specs=[pl.BlockSpec((tm, width), lambda i: (i, 0)),
                   pl.BlockSpec((tm, width), lambda i: (i, 0)),
                   pl.BlockSpec((tm, 2 * LANES), lambda i: (i, 0))],
        out_shape=[jax.ShapeDtypeStruct((n, width), BF16),
                   jax.ShapeDtypeStruct((n, width), BF16),
                   jax.ShapeDtypeStruct((n, 2 * LANES), BF16)],
        compiler_params=_cparams(1),
        name="c_proj",
    )(x2, g.reshape(1, d), wq, wkv, wpe, q_norm.reshape(1, -1), kv_norm.reshape(1, -1),
      wqu, wkvu, cos, sm, sp)


def _attn_c_kernel(q_ref, kv_ref, pe_ref, o_ref, kcat_ref, vcat_ref, s_ref, m_ref, acc_ref, *, t):
    qi = pl.program_id(2)
    pw = 2 * LANES

    @pl.when(qi == 0)
    def _():
        lane = lax.broadcasted_iota(I32, kcat_ref.shape, 1)
        is_v = jnp.logical_or(lane < HEAD_DIM, lane >= pw - HEAD_DIM)
        kv = kv_ref[0]
        kcat_ref[...] = jnp.where(is_v, pe_ref[0], kv)
        vcat_ref[...] = jnp.where(is_v, kv, jnp.ones_like(kv))

    _init_softmax_state(m_ref, acc_ref)
    q = q_ref[0]
    first = lax.broadcasted_iota(I32, q.shape, 1) < LANES
    zero = jnp.zeros_like(q)
    q_all = jnp.concatenate([jnp.where(first, q, zero), jnp.where(first, zero, q)], axis=0)

    def rows(kb):
        return pl.ds(pl.multiple_of(kb * t, t), t)

    def scores(kb, c):
        return _dot_nt(q_all, kcat_ref[rows(kb), :])

    def update(s, kb, c):
        _softmax_update(s, vcat_ref[rows(kb), :], m_ref, acc_ref, c)

    _causal_sweep(qi, t, 1, scores, update, s_ref)
    lo = lax.broadcasted_iota(I32, (t, LANES), 1) < HEAD_DIM
    top = acc_ref[0, :t, :LANES]
    bot = acc_ref[0, t:, LANES:]
    o_ref[0] = jnp.where(lo, top / pltpu.roll(top, HEAD_DIM, 1),
                         bot / pltpu.roll(bot, HEAD_DIM, 1)).astype(o_ref.dtype)


def attn_c(q3, kv3, pe3, *, t=512):
    b, s, width = q3.shape
    npair = width // (2 * LANES)
    return pl.pallas_call(
        functools.partial(_attn_c_kernel, t=t),
        grid=(b, npair, s // t),
        in_specs=[pl.BlockSpec((1, t, 2 * LANES), lambda bi, p, qi: (bi, qi, p)),
                  pl.BlockSpec((1, s, 2 * LANES), lambda bi, p, qi: (bi, 0, p)),
                  pl.BlockSpec((1, s, 2 * LANES), lambda bi, p, qi: (bi, 0, 0))],
        out_specs=pl.BlockSpec((1, t, LANES), lambda bi, p, qi: (bi, qi, p)),
        out_shape=jax.ShapeDtypeStruct((b, s, npair * LANES), BF16),
        scratch_shapes=[pltpu.VMEM((s, 2 * LANES), BF16)] * 2 + [pltpu.VMEM((2, 1, 2 * t, t), F32),
                        pltpu.VMEM((1, 2 * t, LANES), F32), pltpu.VMEM((1, 2 * t, 2 * LANES), F32)],
        compiler_params=_cparams(3),
        name="attn_c",
    )(q3, kv3, pe3)


def _attn_d_kernel(q_ref, k_ref, v_ref, o_ref, r_ref, acc_ref, *, t, scale):
    qi = pl.program_id(2)
    q = q_ref[0] * scale
    hw = q.shape[1]
    nh = hw // HEAD_DIM
    head_of_lane = lax.broadcasted_iota(I32, (t, hw), 1) // HEAD_DIM
    zero = jnp.zeros_like(q)
    q_all = jnp.concatenate([jnp.where(head_of_lane == h, q, zero) for h in range(nh)], axis=0)
    r_ref[...] = jnp.zeros(r_ref.shape, F32)
    acc_ref[...] = jnp.zeros(acc_ref.shape, F32)

    def row_in_block(n_rows):
        row = lax.broadcasted_iota(I32, (n_rows, t), 0)
        for _ in range(n_rows // t - 1):
            row = jnp.where(row >= t, row - t, row)
        return row

    ones_ge = jnp.where(row_in_block(2 * t) >= lax.broadcasted_iota(I32, (2 * t, t), 1),
                        1.0, 0.0).astype(BF16)
    strict = row_in_block(nh * t) > lax.broadcasted_iota(I32, (nh * t, t), 1)

    def block(kb, masked):
        off = pl.multiple_of(kb * t, t)
        z = _dot_nt(q_all, k_ref[0, pl.ds(off, t), :])
        lg = jnp.log(1.0 + jnp.exp2(jnp.abs(z) * (-LOG2E)))
        ls = jnp.minimum(z, 0.0) - lg
        lk = ls - z
        if masked:
            lk = jnp.where(strict, lk, 0.0)
        hi = lk.astype(BF16)
        lw = (lk - hi.astype(F32)).astype(BF16)
        csum = _dot(jnp.concatenate([hi, lw], axis=1), ones_ge)
        r_prev = r_ref[...]
        a = jnp.exp(ls + (_lane_tile(r_prev, t // LANES) + (csum - lk)))
        if masked:
            a = jnp.where(strict, a, 0.0)
        acc_ref[...] = acc_ref[...] + _dot(a.astype(BF16), v_ref[0, pl.ds(off, t), :])
        r_ref[...] = r_prev + csum[:, 0:1]

    def live():
        return jnp.max(r_ref[...]) >= EXP_UNDERFLOW

    def cond(carry):
        it, alive = carry
        return jnp.logical_and(it < qi, alive)

    def body(carry):
        it, _ = carry
        block(qi - 1 - it, False)
        return it + 1, live()

    @pl.when(qi == 0)
    def _():
        block(qi, True)

    @pl.when(qi > 0)
    def _():
        block(qi, True)
        block(qi - 1, False)
        lax.while_loop(cond, body, (jnp.int32(1), live()))

    o = acc_ref[:t]
    for h in range(1, nh):
        o = jnp.where(head_of_lane == h, acc_ref[h * t:(h + 1) * t], o)
    o_ref[0] = o.astype(o_ref.dtype)


def attn_d(qkv3, *, t=256):
    b, s, width = qkv3.shape
    hw = MXU_DIM
    ngrp = width // (3 * hw)
    nh = hw // HEAD_DIM
    return pl.pallas_call(
        functools.partial(_attn_d_kernel, t=t, scale=HEAD_DIM ** -0.5),
        grid=(b, ngrp, s // t),
        in_specs=[pl.BlockSpec((1, t, hw), lambda bi, p, qi: (bi, qi, p)),
                  pl.BlockSpec((1, s, hw), lambda bi, p, qi: (bi, 0, ngrp + p)),
                  pl.BlockSpec((1, s, hw), lambda bi, p, qi: (bi, 0, 2 * ngrp + p))],
        out_specs=pl.BlockSpec((1, t, hw), lambda bi, p, qi: (bi, qi, p)),
        out_shape=jax.ShapeDtypeStruct((b, s, ngrp * hw), BF16),
        scratch_shapes=[pltpu.VMEM((nh * t, LANES), F32), pltpu.VMEM((nh * t, hw), F32)],
        compiler_params=_cparams(3),
        name="attn_d",
    )(qkv3, qkv3, qkv3)


def _silu(x):
    return x / (1.0 + jnp.exp(-x))


def _ffn_kernel(x_ref, g_ref, wg_ref, wu_ref, wd_ref, out_ref, *, ff_chunk):
    x = x_ref[...]
    xn = _rms(x, g_ref[...], NORM_EPS).astype(BF16)
    y = x
    for c in range(0, wg_ref.shape[1], ff_chunk):
        h = _silu(_dot(xn, wg_ref[:, c:c + ff_chunk])) * _dot(xn, wu_ref[:, c:c + ff_chunk])
        y = y + _dot(h.astype(BF16), wd_ref[c:c + ff_chunk, :])
    out_ref[...] = y


def ffn(x2, g, wg, wu, wd, *, tm=512):
    n, d = x2.shape
    ff = wg.shape[1]
    ff_chunk = ff // 2 if (ff // 2) % LANES == 0 else ff
    const = lambda shape: pl.BlockSpec(shape, lambda i: (0, 0), pipeline_mode=pl.Buffered(1))
    return pl.pallas_call(
        functools.partial(_ffn_kernel, ff_chunk=ff_chunk),
        grid=(n // tm,),
        in_specs=[pl.BlockSpec((tm, d), lambda i: (i, 0)),
                  pl.BlockSpec((1, d), lambda i: (0, 0)),
                  const((d, ff)), const((d, ff)), const((ff, d))],
        out_specs=pl.BlockSpec((tm, d), lambda i: (i, 0)),
        out_shape=jax.ShapeDtypeStruct((n, d), F32),
        compiler_params=_cparams(1),
        name="ffn",
    )(x2, g.reshape(1, d), wg, wu, wd)


def _router_kernel(x_ref, g_ref, wr_ref, meta_ref, gate_ref, cnt_ref, run_ref):
    i = pl.program_id(0)
    tm = x_ref.shape[0]

    @pl.when(i == 0)
    def _():
        run_ref[...] = jnp.zeros(run_ref.shape, F32)

    xn = _rms(x_ref[...], g_ref[...], NORM_EPS)
    xh = xn.astype(BF16)
    xl = (xn - xh.astype(F32)).astype(BF16)
    logits = _dot(xh, wr_ref[0]) + _dot(xl, wr_ref[0]) + _dot(xh, wr_ref[1])
    lane = lax.broadcasted_iota(I32, (tm, LANES), 1)
    lg = jnp.where(lane < N_EXPERTS, logits, NEG_INF)
    m1 = jnp.max(lg, axis=1, keepdims=True)
    e1 = jnp.min(jnp.where(lg == m1, lane, LANES), axis=1, keepdims=True)
    lg2 = jnp.where(lane == e1, NEG_INF, lg)
    m2 = jnp.max(lg2, axis=1, keepdims=True)
    e2 = jnp.min(jnp.where(lg2 == m2, lane, LANES), axis=1, keepdims=True)
    ex = jnp.exp(m2 - m1)
    g1 = 1.0 / (1.0 + ex)
    g2 = ex / (1.0 + ex)
    hit1 = lane == e1
    hit2 = lane == e2
    onehot = jnp.where(jnp.logical_or(hit1, hit2), 1.0, 0.0)
    earlier = (lax.broadcasted_iota(I32, (tm, tm), 0) > lax.broadcasted_iota(I32, (tm, tm), 1))
    before = _dot(jnp.where(earlier, 1.0, 0.0).astype(BF16), onehot.astype(BF16)) + run_ref[0:1, :]
    pos1 = jnp.sum(jnp.where(hit1, before, 0.0), axis=1, keepdims=True).astype(I32)
    pos2 = jnp.sum(jnp.where(hit2, before, 0.0), axis=1, keepdims=True).astype(I32)
    run_ref[...] = run_ref[...] + jnp.sum(onehot, axis=0, keepdims=True)
    meta_ref[...] = jnp.where(lane == 0, e1, jnp.where(lane == 1, e2, jnp.where(
        lane == 2, pos1, jnp.where(lane == 3, pos2, 0))))
    gate_ref[...] = jnp.where(lane == 0, g1, jnp.where(lane == 1, g2, 0.0))
    cnt_ref[...] = run_ref[...]


def router(x2, g, w_router, *, tm=512):
    n, d = x2.shape
    wr = jnp.zeros((d, LANES), F32).at[:, :N_EXPERTS].set(w_router)
    wr_hi = wr.astype(BF16)
    wr = jnp.stack([wr_hi, (wr - wr_hi.astype(F32)).astype(BF16)])
    row = lambda w: pl.BlockSpec((tm, w), lambda i: (i, 0))
    return pl.pallas_call(
        _router_kernel,
        grid=(n // tm,),
        in_specs=[row(d), pl.BlockSpec((1, d), lambda i: (0, 0)),
                  pl.BlockSpec((2, d, LANES), lambda i: (0, 0, 0))],
        out_specs=[row(LANES), row(LANES), pl.BlockSpec((8, LANES), lambda i: (0, 0))],
        out_shape=[jax.ShapeDtypeStruct((n, LANES), I32),
                   jax.ShapeDtypeStruct((n, LANES), F32), jax.ShapeDtypeStruct((8, LANES), F32)],
        scratch_shapes=[pltpu.VMEM((8, LANES), F32)],
        compiler_params=_cparams(1),
        name="router",
    )(x2, g.reshape(1, d), wr)


def _row_copy(src_ref, src_row, dst_ref, dst_row, sem):
    return pltpu.make_async_copy(src_ref.at[pl.ds(src_row, 1)], dst_ref.at[pl.ds(dst_row, 1)], sem)


def _dispatch_kernel(dest_ref, pend_ref, x_ref, g_ref, xb_ref, xn_ref, sem, *, tb, n_steps):
    i = pl.program_id(0)
    slot = lax.rem(i, 2)

    def drain(s):
        for k in range(2):
            pltpu.make_async_copy(xn_ref.at[s], xb_ref.at[pl.ds(0, tb)], sem.at[s]).wait()

    @pl.when(i == 0)
    def _():
        xn_ref[0] = jnp.zeros(xn_ref.shape[1:], F32)

        def zero_block(row):
            return pltpu.make_async_copy(
                xn_ref.at[0], xb_ref.at[pl.ds(pl.multiple_of(row, tb), tb)], sem.at[0])

        def has_rows(e):
            return pend_ref[e] > (pend_ref[e - 1] if e > 0 else 0)

        total = pend_ref[N_EXPERTS - 1]
        n_rows = xb_ref.shape[0]
        for start in (True, False):
            for e in range(N_EXPERTS):
                @pl.when(has_rows(e))
                def _():
                    cp = zero_block(pend_ref[e] - tb)
                    cp.start() if start else cp.wait()

                @pl.when(total + e * tb < n_rows)
                def _():
                    cp = zero_block(total + e * tb)
                    cp.start() if start else cp.wait()

    @pl.when(i >= 2)
    def _():
        drain(slot)

    xn_ref[slot] = _rms(x_ref[...], g_ref[...], NORM_EPS)

    def issue(tt, carry):
        for k in range(2):
            _row_copy(xn_ref.at[slot], tt, xb_ref, dest_ref[2 * tt + k], sem.at[slot]).start()
        return carry

    lax.fori_loop(0, tb, issue, 0, unroll=8)

    @pl.when(i == n_steps - 1)
    def _():
        drain(slot)
        if n_steps > 1:
            drain(1 - slot)


def dispatch(dest_flat, pad_end, x2, g, n_rows):
    n, d = x2.shape
    tb = MOE_ROWS
    return pl.pallas_call(
        functools.partial(_dispatch_kernel, tb=tb, n_steps=n // tb),
        grid=(n // tb,),
        in_specs=[pl.BlockSpec((2 * tb,), lambda i: (i,), memory_space=pltpu.SMEM),
                  pl.BlockSpec(memory_space=pltpu.SMEM),
                  pl.BlockSpec((tb, d), lambda i: (i, 0)),
                  pl.BlockSpec((1, d), lambda i: (0, 0))],
        out_specs=pl.BlockSpec(memory_space=pl.ANY),
        out_shape=jax.ShapeDtypeStruct((n_rows, d), F32),
        scratch_shapes=[pltpu.VMEM((2, tb, d), F32), pltpu.SemaphoreType.DMA((2,))],
        compiler_params=_cparams(1),
        name="moe_dispatch",
    )(dest_flat, pad_end.astype(I32), x2, g.reshape(1, d))


def _expert_kernel(be_ref, nu_ref, x_ref, wg_ref, wu_ref, wd_ref, y_ref, *, ff_chunk):
    i = pl.program_id(0)

    @pl.when(i < nu_ref[0])
    def _():
        xb = x_ref[...].astype(BF16)
        y = jnp.zeros(y_ref.shape, F32)
        for c in range(0, wg_ref.shape[2], ff_chunk):
            h = (_silu(_dot(xb, wg_ref[0, :, c:c + ff_chunk]))
                 * _dot(xb, wu_ref[0, :, c:c + ff_chunk]))
            y = y + _dot(h.astype(BF16), wd_ref[0, c:c + ff_chunk, :])
        y_ref[...] = y

    @pl.when(i >= nu_ref[0])
    def _():
        y_ref[...] = jnp.zeros(y_ref.shape, F32)


def experts(block_e, n_used, xb, wg, wu, wd, *, ff_chunk=512):
    rows, d = xb.shape
    ff = wg.shape[2]
    assert ff % ff_chunk == 0
    wspec = lambda shape: pl.BlockSpec(shape, lambda i, be, nu: (be[i], 0, 0),
                                       pipeline_mode=pl.Buffered(1))
    return pl.pallas_call(
        functools.partial(_expert_kernel, ff_chunk=ff_chunk),
        grid_spec=pltpu.PrefetchScalarGridSpec(
            num_scalar_prefetch=2,
            grid=(rows // MOE_ROWS,),
            in_specs=[pl.BlockSpec((MOE_ROWS, d), lambda i, be, nu: (i, 0)),
                      wspec((1, d, ff)), wspec((1, d, ff)), wspec((1, ff, d))],
            out_specs=pl.BlockSpec((MOE_ROWS, d), lambda i, be, nu: (i, 0)),
        ),
        out_shape=jax.ShapeDtypeStruct((rows, d), F32),
        compiler_params=_cparams(1),
        name="moe_experts",
    )(block_e, n_used, xb, wg, wu, wd)


def _combine_kernel(dest_ref, dest_next_ref, x_ref, gate_ref, fg_ref, yb_ref, out_ref, buf_ref, sem,
                    *, tb, final, n_steps):
    i = pl.program_id(0)
    slot = lax.rem(i, 2)

    def gather(d_ref, s):
        def issue(tt, carry):
            for k in range(2):
                _row_copy(yb_ref, d_ref[2 * tt + k], buf_ref.at[s, k], tt, sem.at[s]).start()
            return carry

        lax.fori_loop(0, tb, issue, 0, unroll=8)

    @pl.when(i == 0)
    def _():
        gather(dest_ref, slot)

    @pl.when(i + 1 < n_steps)
    def _():
        gather(dest_next_ref, 1 - slot)

    for k in range(2):
        pltpu.make_async_copy(yb_ref.at[pl.ds(0, tb)], buf_ref.at[slot, k], sem.at[slot]).wait()
    gate = gate_ref[...]
    y = x_ref[...] + gate[:, 0:1] * buf_ref[slot, 0] + gate[:, 1:2] * buf_ref[slot, 1]
    if final:
        y = _rms(y, fg_ref[...], NORM_EPS)
    out_ref[...] = y


def combine(dest_flat, x2, gates, yb, final_g, *, final, tb=512):
    n, d = x2.shape
    n_steps = n // tb
    return pl.pallas_call(
        functools.partial(_combine_kernel, tb=tb, final=final, n_steps=n_steps),
        grid=(n_steps,),
        in_specs=[pl.BlockSpec((2 * tb,), lambda i: (i,), memory_space=pltpu.SMEM),
                  pl.BlockSpec((2 * tb,), lambda i: (jnp.minimum(i + 1, n_steps - 1),),
                               memory_space=pltpu.SMEM),
                  pl.BlockSpec((tb, d), lambda i: (i, 0)),
                  pl.BlockSpec((tb, LANES), lambda i: (i, 0)),
                  pl.BlockSpec((1, d), lambda i: (0, 0)),
                  pl.BlockSpec(memory_space=pl.ANY)],
        out_specs=pl.BlockSpec((tb, d), lambda i: (i, 0)),
        out_shape=jax.ShapeDtypeStruct((n, d), F32),
        scratch_shapes=[pltpu.VMEM((2, 2, tb, d), F32), pltpu.SemaphoreType.DMA((2,))],
        compiler_params=_cparams(1),
        name="moe_combine",
    )(dest_flat, dest_flat, x2, gates, final_g.reshape(1, d), yb)


def moe(x2, g, w_router, wg, wu, wd, final_g, *, final):
    n, d = x2.shape
    meta, gates, cnt = router(x2, g, w_router)
    counts = cnt[0, :N_EXPERTS].astype(I32)
    padded = (counts + MOE_ROWS - 1) // MOE_ROWS * MOE_ROWS
    pad_end = jnp.cumsum(padded)
    pad_start = pad_end - padded
    dest = (pad_start[meta[:, 0:2]] + meta[:, 2:4]).reshape(-1).astype(I32)
    n_blocks = -(-2 * n // MOE_ROWS) + N_EXPERTS
    starts = jnp.arange(n_blocks, dtype=I32) * MOE_ROWS
    block_e = jnp.minimum(jnp.sum((pad_end[None, :] <= starts[:, None]).astype(I32), axis=1),
                          N_EXPERTS - 1)
    n_used = (pad_end[-1:] // MOE_ROWS).astype(I32)
    xb = dispatch(dest, pad_end, x2, g, n_blocks * MOE_ROWS)
    yb = experts(block_e, n_used, xb, wg.astype(BF16), wu.astype(BF16), wd.astype(BF16))
    return combine(dest, x2, gates, yb, final_g, final=final)


def _bf16(w):
    return w.astype(BF16)


def mixer_a(x2, b, norm, w_qkv, w_o):
    d = x2.shape[1]
    n_groups = len(A_PATTERN)
    dils = tuple(dil for _, dil in A_PATTERN)
    gw = A_HEADS * HEAD_DIM
    w = w_qkv.reshape(d, 3, n_groups, gw)
    w = w * jnp.asarray([HEAD_DIM ** -0.5 * LOG2E, 1.0, 1.0], F32).reshape(1, 3, 1, 1)
    w = _bf16(w.transpose(0, 2, 1, 3).reshape(d, n_groups * 3 * gw))
    views = a_proj(x2, norm, w, dils)
    slopes = 2.0 ** (-8.0 * np.arange(1, n_groups * A_HEADS + 1) / (n_groups * A_HEADS))
    outs, lses = [], []
    for gi, (window, dil) in enumerate(A_PATTERN):
        assert window // dil == QB
        o, lse = attn_a_group(views[gi], b, dil, slopes[gi * A_HEADS:(gi + 1) * A_HEADS])
        outs.append(o)
        lses.append(lse)
    return mix_oproj_res(x2, outs, lses, _bf16(w_o), dils)


def mixer_b(x2, b, norm, w_qkv, lq1, lk1, lq2, lk2, subln, w_o, *, layer, t=512):
    n = x2.shape[0]
    n_q = w_qkv.shape[1] // 3
    w = jnp.concatenate([w_qkv[:, :n_q] * (HEAD_DIM ** -0.5 * LOG2E), w_qkv[:, n_q:]], axis=1)
    qkv = norm_proj(x2, norm, _bf16(w)).reshape(b, n // b, -1)
    o = attn_b(qkv, lq1, lk1, lq2, lk2, subln, layer, t=t)
    return oproj_res(x2, o.reshape(n, -1), _bf16(w_o))


def mixer_c(x2, b, norm, w_in, q_norm, w_q_up, kv_norm, w_kv_up, w_o, *, t=512):
    n = x2.shape[0]
    s = n // b
    q, kv, pe = c_proj(x2, s, norm, w_in, q_norm, w_q_up, kv_norm, w_kv_up)
    o = attn_c(q.reshape(b, s, -1), kv.reshape(b, s, -1), pe.reshape(b, s, -1), t=t)
    return oproj_res(x2, o.reshape(n, -1), _bf16(w_o))


def mixer_d(x2, b, norm, w_qkv, w_o, *, t=256):
    n = x2.shape[0]
    qkv = norm_proj(x2, norm, _bf16(w_qkv)).reshape(b, n // b, -1)
    o = attn_d(qkv, t=t)
    return oproj_res(x2, o.reshape(n, -1), _bf16(w_o))
def kernel(x, a_norm, a_w_qkv, a_w_o, ffn0_norm, ffn0_w_gate, ffn0_w_up, ffn0_w_down, b_norm, b_w_qkv, b_lambda_q1, b_lambda_k1, b_lambda_q2, b_lambda_k2, b_subln, b_w_o, moe1_norm, moe1_w_router, moe1_w_gate, moe1_w_up, moe1_w_down, c_norm, c_w_in, c_q_norm, c_w_q_up, c_kv_norm, c_w_kv_up, c_w_o, ffn2_norm, ffn2_w_gate, ffn2_w_up, ffn2_w_down, d_norm, d_w_qkv, d_w_o, moe3_norm, moe3_w_router, moe3_w_gate, moe3_w_up, moe3_w_down, final_norm):
    b, s, d = x.shape
    x2 = x.reshape(b * s, d)
    bf = _bf16
    x2 = mixer_a(x2, b, a_norm, a_w_qkv, a_w_o)
    x2 = ffn(x2, ffn0_norm, bf(ffn0_w_gate), bf(ffn0_w_up), bf(ffn0_w_down))
    x2 = mixer_b(x2, b, b_norm, b_w_qkv, b_lambda_q1, b_lambda_k1, b_lambda_q2, b_lambda_k2,
                 b_subln, b_w_o, layer=1)
    x2 = moe(x2, moe1_norm, moe1_w_router, moe1_w_gate, moe1_w_up, moe1_w_down, final_norm,
             final=False)
    x2 = mixer_c(x2, b, c_norm, c_w_in, c_q_norm, c_w_q_up, c_kv_norm, c_w_kv_up, c_w_o)
    x2 = ffn(x2, ffn2_norm, bf(ffn2_w_gate), bf(ffn2_w_up), bf(ffn2_w_down))
    x2 = mixer_d(x2, b, d_norm, d_w_qkv, d_w_o)
    x2 = moe(x2, moe3_norm, moe3_w_router, moe3_w_gate, moe3_w_up, moe3_w_down, final_norm,
             final=True)
    return x2.reshape(b, s, d)
```

```python
import functools
import math

import numpy as np
import jax
import jax.numpy as jnp
from jax import lax
from jax.experimental import pallas as pl
from jax.experimental.pallas import tpu as pltpu

F32 = jnp.float32
BF16 = jnp.bfloat16
I32 = jnp.int32

LANES = 128
MXU_DIM = 256
HEAD_DIM = 64
NORM_EPS = 1e-6
SUBLN_EPS = 1e-5
ROPE_THETA = 10000.0
A_PATTERN = ((128, 1), (512, 4), (2048, 16))
A_HEADS = 8
QB = 128
A_SUB = 2
N_EXPERTS = 8
MOE_ROWS = 512
VMEM_LIMIT = 56 * 1024 * 1024
NEG_INF = float("-inf")
EXP_UNDERFLOW = -105.0


def _cparams(n_axes, vmem=VMEM_LIMIT):
    return pltpu.CompilerParams(dimension_semantics=("arbitrary",) * n_axes,
                                vmem_limit_bytes=vmem)


def _dot(a, b):
    return jnp.dot(a, b, preferred_element_type=F32)


def _dot_nt(a, b):
    return lax.dot_general(a, b, (((1,), (1,)), ((), ())), preferred_element_type=F32)


def _rms(x, g, eps):
    return x * lax.rsqrt(jnp.mean(x * x, axis=-1, keepdims=True) + eps) * g


def _lane_tile(x, n):
    return x if n == 1 else jnp.concatenate([x] * n, axis=1)


def _norm_proj_kernel(x_ref, g_ref, w_ref, o_ref, *, col_chunk):
    xn = _rms(x_ref[...], g_ref[...], NORM_EPS).astype(BF16)
    n = o_ref.shape[1]
    for c in range(0, n, col_chunk):
        o_ref[:, c:c + col_chunk] = _dot(xn, w_ref[:, c:c + col_chunk]).astype(o_ref.dtype)


def norm_proj(x2, g, w, *, tm=512, col_chunk=512):
    n, d = x2.shape
    n_out = w.shape[1]
    assert n % tm == 0 and n_out % col_chunk == 0
    return pl.pallas_call(
        functools.partial(_norm_proj_kernel, col_chunk=col_chunk),
        grid=(n // tm,),
        in_specs=[pl.BlockSpec((tm, d), lambda i: (i, 0)),
                  pl.BlockSpec((1, d), lambda i: (0, 0)),
                  pl.BlockSpec((d, n_out), lambda i: (0, 0))],
        out_specs=pl.BlockSpec((tm, n_out), lambda i: (i, 0)),
        out_shape=jax.ShapeDtypeStruct((n, n_out), BF16),
        compiler_params=_cparams(1),
        name="norm_proj",
    )(x2, g.reshape(1, d), w)


def _oproj_kernel(x_ref, o_ref, w_ref, out_ref):
    out_ref[...] = x_ref[...] + _dot(o_ref[...], w_ref[...])


def oproj_res(x2, o2, w, *, tm=1024):
    n, d = x2.shape
    k = o2.shape[1]
    return pl.pallas_call(
        _oproj_kernel,
        grid=(n // tm,),
        in_specs=[pl.BlockSpec((tm, d), lambda i: (i, 0)),
                  pl.BlockSpec((tm, k), lambda i: (i, 0)),
                  pl.BlockSpec((k, d), lambda i: (0, 0))],
        out_specs=pl.BlockSpec((tm, d), lambda i: (i, 0)),
        out_shape=jax.ShapeDtypeStruct((n, d), F32),
        compiler_params=_cparams(1),
        name="oproj_res",
    )(x2, o2, w)


LOG2E = math.log2(math.e)
LN2 = math.log(2.0)


def _a_proj_kernel(x_ref, g_ref, w_ref, *rest, dils, gcols):
    outs, scr = rest[:len(dils)], rest[len(dils)]
    xn = _rms(x_ref[...], g_ref[...], NORM_EPS).astype(BF16)
    tm = x_ref.shape[0]
    for gi, dil in enumerate(dils):
        res = _dot(xn, w_ref[:, gi * gcols:(gi + 1) * gcols])
        if dil == 1:
            outs[gi][...] = res.astype(BF16)
        else:
            for c in range(gcols // LANES):
                scr[c] = res[:, c * LANES:(c + 1) * LANES]
            for r in range(dil):
                for c in range(gcols // LANES):
                    col = r * gcols + c * LANES
                    outs[gi][:, col:col + LANES] = (
                        scr[c, pl.ds(r, tm // dil, stride=dil), :].astype(BF16))


def a_proj(x2, g, w, dils, *, tm=512):
    n, d = x2.shape
    gcols = w.shape[1] // len(dils)
    return pl.pallas_call(
        functools.partial(_a_proj_kernel, dils=dils, gcols=gcols),
        grid=(n // tm,),
        in_specs=[pl.BlockSpec((tm, d), lambda i: (i, 0)),
                  pl.BlockSpec((1, d), lambda i: (0, 0)),
                  pl.BlockSpec((d, w.shape[1]), lambda i: (0, 0))],
        out_specs=[pl.BlockSpec((tm // dil, dil * gcols), lambda i: (i, 0)) for dil in dils],
        out_shape=[jax.ShapeDtypeStruct((n // dil, dil * gcols), BF16) for dil in dils],
        scratch_shapes=[pltpu.VMEM((gcols // LANES, tm, LANES), F32)],
        compiler_params=_cparams(1),
        name="a_proj",
    )(x2, g.reshape(1, d), w)


def _attn_a_kernel(q_ref, kc_ref, kp_ref, vc_ref, vp_ref, bc_ref, bp_ref, o_ref, lse_ref):
    first_step = pl.program_id(2) == 0
    hw = MXU_DIM
    nh = hw // HEAD_DIM
    head_of_lane = lax.broadcasted_iota(I32, (QB, hw), 1) // HEAD_DIM
    for sub in range(q_ref.shape[1] // QB):
        own = slice(sub * QB, (sub + 1) * QB)
        before = slice((sub - 1) * QB, sub * QB)
        for half in range(q_ref.shape[2] // hw):
            sl = slice(half * hw, (half + 1) * hw)
            tab = slice(half * nh * QB, (half + 1) * nh * QB)
            q = q_ref[0, own, sl]
            kp, vp = ((kp_ref[0, :, sl], vp_ref[0, :, sl]) if sub == 0
                      else (kc_ref[0, before, sl], vc_ref[0, before, sl]))
            zero = jnp.zeros_like(q)
            q_all = jnp.concatenate([jnp.where(head_of_lane == h, q, zero) for h in range(nh)],
                                    axis=0)
            sc = _dot_nt(q_all, kc_ref[0, own, sl]) + bc_ref[tab, :]
            sp = _dot_nt(q_all, kp) + bp_ref[tab, :]
            if sub == 0:
                sp = jnp.where(first_step, NEG_INF, sp)
            m = jnp.maximum(jnp.max(sc, axis=1, keepdims=True), jnp.max(sp, axis=1, keepdims=True))
            pc = jnp.exp2(sc - m)
            pp = jnp.exp2(sp - m)
            l = jnp.sum(pc, axis=1, keepdims=True) + jnp.sum(pp, axis=1, keepdims=True)
            acc = _dot(pc.astype(BF16), vc_ref[0, own, sl]) + _dot(pp.astype(BF16), vp)
            lse = (m + jnp.log2(l)) * LN2
            o = jnp.zeros((QB, hw), F32)
            ls = jnp.zeros((QB, hw), F32)
            for h in range(nh):
                rows = slice(h * QB, (h + 1) * QB)
                mine = head_of_lane == h
                o = jnp.where(mine, acc[rows] / l[rows], o)
                ls = jnp.where(mine, lse[rows], ls)
            o_ref[0, own, sl] = o.astype(o_ref.dtype)
            lse_ref[0, own, sl] = ls


def _a_bias_tables(dil, slopes):
    rel_c = np.arange(QB)[:, None] - np.arange(QB)[None, :]
    rel_p = rel_c + QB
    sl = np.asarray(slopes, np.float64)[:, None, None] * dil * LOG2E
    bc = np.where(rel_c >= 0, -sl * rel_c, -np.inf).reshape(-1, QB)
    bp = np.where(rel_p <= QB, -sl * rel_p, -np.inf).reshape(-1, QB)
    return jnp.asarray(bc, F32), jnp.asarray(bp, F32)


def attn_a_group(view2, b, dil, slopes):
    gw = A_HEADS * HEAD_DIM
    l = view2.shape[0] // b
    tq = A_SUB * QB
    assert l % tq == 0 and view2.shape[1] == dil * 3 * gw
    view = view2.reshape(b, l, dil * 3 * gw)
    bc, bp = _a_bias_tables(dil, slopes)

    def spec(which, prev):
        if prev:
            return pl.BlockSpec((1, QB, gw), lambda bi, r, j: (
                bi, jnp.maximum(j * A_SUB - 1, 0), r * 3 + which))
        return pl.BlockSpec((1, tq, gw), lambda bi, r, j: (bi, j, r * 3 + which))

    tab = pl.BlockSpec((A_HEADS * QB, QB), lambda bi, r, j: (0, 0))
    o, lse = pl.pallas_call(
        _attn_a_kernel,
        grid=(b, dil, l // tq),
        in_specs=[spec(0, False), spec(1, False), spec(1, True), spec(2, False), spec(2, True),
                  tab, tab],
        out_specs=[pl.BlockSpec((1, tq, gw), lambda bi, r, j: (bi, j, r)),
                   pl.BlockSpec((1, tq, gw), lambda bi, r, j: (bi, j, r))],
        out_shape=[jax.ShapeDtypeStruct((b, l, dil * gw), BF16),
                   jax.ShapeDtypeStruct((b, l, dil * gw), F32)],
        compiler_params=_cparams(3),
        name=f"attn_a_d{dil}",
    )(view, view, view, view, view, bc, bp)
    return o.reshape(b * l, dil * gw), lse.reshape(b * l, dil * gw)


def _mix_oproj_kernel(x_ref, *rest, dils, gw):
    ng = len(dils)
    o_refs, l_refs = rest[:ng], rest[ng:2 * ng]
    w_ref, out_ref, o_scr, l_scr = rest[2 * ng:]
    tm = x_ref.shape[0]
    os_, ls = [], []
    for gi, dil in enumerate(dils):
        if dil == 1:
            os_.append(o_refs[gi][...].astype(F32))
            ls.append(l_refs[gi][...])
        else:
            nct = gw // LANES
            for r in range(dil):
                rows = pl.ds(r, tm // dil, stride=dil)
                for c in range(nct):
                    cols = slice(r * gw + c * LANES, r * gw + (c + 1) * LANES)
                    o_scr[gi, c, rows, :] = o_refs[gi][:, cols].astype(F32)
                    l_scr[gi, c, rows, :] = l_refs[gi][:, cols]
            os_.append(jnp.concatenate([o_scr[gi, c] for c in range(nct)], axis=1))
            ls.append(jnp.concatenate([l_scr[gi, c] for c in range(nct)], axis=1))
    m = functools.reduce(jnp.maximum, ls)
    es = [jnp.exp(v - m) for v in ls]
    den = functools.reduce(lambda a, c: a + c, es)
    mixed = functools.reduce(lambda a, c: a + c, [(e / den) * o for e, o in zip(es, os_)])
    out_ref[...] = x_ref[...] + _dot(mixed.astype(BF16), w_ref[...])


def mix_oproj_res(x2, outs, lses, w, dils, *, tm=1024):
    n, d = x2.shape
    gw = w.shape[0]
    ng = len(dils)
    views = [pl.BlockSpec((tm // dil, dil * gw), lambda i: (i, 0)) for dil in dils]
    return pl.pallas_call(
        functools.partial(_mix_oproj_kernel, dils=dils, gw=gw),
        grid=(n // tm,),
        in_specs=[pl.BlockSpec((tm, d), lambda i: (i, 0))] + views + views
                 + [pl.BlockSpec((gw, d), lambda i: (0, 0))],
        out_specs=pl.BlockSpec((tm, d), lambda i: (i, 0)),
        out_shape=jax.ShapeDtypeStruct((n, d), F32),
        scratch_shapes=[pltpu.VMEM((ng, gw // LANES, tm, LANES), F32)] * 2,
        compiler_params=_cparams(1),
        name="mix_oproj_res",
    )(x2, *outs, *lses, w)


def _softmax_update(s, v, m_ref, acc_ref, idx):
    tk = s.shape[1]
    m_prev = m_ref[idx]
    m_next = jnp.maximum(m_prev, jnp.max(s, axis=1, keepdims=True))
    p = jnp.exp2(s - _lane_tile(m_next, tk // LANES))
    alpha = jnp.exp2(m_prev - m_next)
    acc_ref[idx] = (_lane_tile(alpha, acc_ref.shape[-1] // LANES) * acc_ref[idx]
                    + _dot(p.astype(BF16), v))
    m_ref[idx] = m_next


def _init_softmax_state(m_ref, acc_ref):
    m_ref[...] = jnp.full(m_ref.shape, NEG_INF, F32)
    acc_ref[...] = jnp.zeros(acc_ref.shape, F32)


def _causal_sweep(qi, t, n_chain, scores, update, s_ref):
    def qk(kb, buf):
        for c in range(n_chain):
            s_ref[buf, c] = scores(kb, c)

    def upd(kb, buf, masked):
        for c in range(n_chain):
            s = s_ref[buf, c]
            if masked:
                s = jnp.where(_causal_mask(s.shape[0], t), s, NEG_INF)
            update(s, kb, c)

    qk(0, 0)

    def body(i, carry):
        kb = 2 * i
        qk(kb + 1, 1)
        upd(kb, 0, False)
        qk(kb + 2, 0)
        upd(kb + 1, 1, False)
        return carry

    lax.fori_loop(0, lax.shift_right_logical(qi, 1), body, 0)
    odd = jnp.bitwise_and(qi, 1) == 1

    @pl.when(odd)
    def _():
        qk(qi, 1)
        upd(qi - 1, 0, False)
        upd(qi, 1, True)

    @pl.when(jnp.logical_not(odd))
    def _():
        upd(qi, 0, True)


def _split3(x):
    hi = x.astype(BF16).astype(F32)
    mid = (x - hi).astype(BF16).astype(F32)
    lo = (x - hi - mid).astype(BF16).astype(F32)
    return hi, mid, lo


def _causal_mask(n_rows, t):
    row = lax.broadcasted_iota(I32, (n_rows, t), 0)
    for _ in range(n_rows // t - 1):
        row = jnp.where(row >= t, row - t, row)
    return row >= lax.broadcasted_iota(I32, (n_rows, t), 1)


def _attn_b_kernel(q_ref, k_ref, v_ref, lq1, lk1, lq2, lk2, g_ref, o_ref, kb_ref, vb_ref, s_ref,
                   m_ref, acc_ref, *, t, slopes, lam_init):
    h = pl.program_id(1)
    qi = pl.program_id(2)
    n_bias = 3

    @pl.when(qi == 0)
    def _():
        slope = jnp.float32(0.0)
        for hh, sv in enumerate(slopes):
            slope = jnp.where(h == hh, jnp.float32(sv), slope)
        shape = (k_ref.shape[1], LANES)
        lane = lax.broadcasted_iota(I32, shape, 1)
        parts = _split3(slope * lax.broadcasted_iota(I32, shape, 0).astype(F32))
        bias = jnp.zeros(shape, F32)
        for j, part in enumerate(parts):
            bias = jnp.where(lane == j, part, bias)
        kb_ref[:, :LANES] = k_ref[0]
        kb_ref[:, LANES:] = bias.astype(BF16)
        vb_ref[:, :LANES] = v_ref[0]
        vb_ref[:, LANES:] = jnp.ones(shape, BF16)

    q = q_ref[0].astype(F32)
    lane = lax.broadcasted_iota(I32, (t, LANES), 1)
    ones = jnp.where(lane < n_bias, 1.0, 0.0)
    q_all = jnp.concatenate(
        [jnp.concatenate([jnp.where(lane < HEAD_DIM, q, 0.0), ones], axis=1),
         jnp.concatenate([jnp.where(lane >= HEAD_DIM, q, 0.0), ones], axis=1)], axis=0).astype(BF16)
    _init_softmax_state(m_ref, acc_ref)

    def rows(kb):
        return pl.ds(pl.multiple_of(kb * t, t), t)

    def scores(kb, c):
        return _dot_nt(q_all, kb_ref[rows(kb), :])

    def update(s, kb, c):
        _softmax_update(s, vb_ref[rows(kb), :], m_ref, acc_ref, c)

    _causal_sweep(qi, t, 1, scores, update, s_ref)

    lam = (jnp.exp(jnp.sum(lq1[...] * lk1[...], axis=1, keepdims=True))
           - jnp.exp(jnp.sum(lq2[...] * lk2[...], axis=1, keepdims=True)) + lam_init)
    acc = acc_ref[0]
    o = acc[:t, :LANES] / acc[:t, LANES:] - lam * (acc[t:, :LANES] / acc[t:, LANES:])
    o = _rms(o, g_ref[...], SUBLN_EPS) * (1.0 - lam_init)
    o_ref[0] = o.astype(o_ref.dtype)


def attn_b(qkv3, lq1, lk1, lq2, lk2, subln, layer, *, t=512):
    b, s, width = qkv3.shape
    nh = width // (3 * LANES)
    slopes = tuple(float(2.0 ** (-8.0 * (i + 1) / nh)) * LOG2E for i in range(nh))
    lam_init = 0.8 - 0.6 * math.exp(-0.3 * layer)
    vec = pl.BlockSpec((1, HEAD_DIM), lambda bi, h, qi: (0, 0))
    return pl.pallas_call(
        functools.partial(_attn_b_kernel, t=t, slopes=slopes, lam_init=lam_init),
        grid=(b, nh, s // t),
        in_specs=[pl.BlockSpec((1, t, LANES), lambda bi, h, qi: (bi, qi, h)),
                  pl.BlockSpec((1, s, LANES), lambda bi, h, qi: (bi, 0, nh + h)),
                  pl.BlockSpec((1, s, LANES), lambda bi, h, qi: (bi, 0, 2 * nh + h)),
                  vec, vec, vec, vec,
                  pl.BlockSpec((1, LANES), lambda bi, h, qi: (0, 0))],
        out_specs=pl.BlockSpec((1, t, LANES), lambda bi, h, qi: (bi, qi, h)),
        out_shape=jax.ShapeDtypeStruct((b, s, nh * LANES), BF16),
        scratch_shapes=[pltpu.VMEM((s, 2 * LANES), BF16)] * 2 + [pltpu.VMEM((2, 1, 2 * t, t), F32),
                        pltpu.VMEM((1, 2 * t, LANES), F32), pltpu.VMEM((1, 2 * t, 2 * LANES), F32)],
        compiler_params=_cparams(3),
        name="attn_b",
    )(qkv3, qkv3, qkv3, lq1.reshape(1, -1), lk1.reshape(1, -1), lq2.reshape(1, -1),
      lk2.reshape(1, -1), subln.reshape(1, -1))


MLA_HEADS = 16
MLA_Q_RANK = 384
MLA_KV_RANK = 256
MLA_ROPE = 32


def _c_proj_kernel(x_ref, g_ref, wq_ref, wkv_ref, wpe_ref, qn_ref, kvn_ref, wqu_ref, wkvu_ref,
                   cos_ref, sm_ref, sp_ref, q_out, kv_out, pe_out):
    xn = _rms(x_ref[...], g_ref[...], NORM_EPS).astype(BF16)
    q_lat = _dot(xn, wq_ref[...])
    kv_lat = _dot(xn, wkv_ref[...])
    cos, sm, sp = cos_ref[...], sm_ref[...], sp_ref[...]
    pw = cos.shape[1]

    def rope(v):
        return v * cos + pltpu.roll(v, pw - MLA_ROPE // 2, 1) * sm + pltpu.roll(v, MLA_ROPE // 2, 1) * sp

    pe_out[...] = rope(_dot(xn, wpe_ref[...])).astype(pe_out.dtype)
    qn = _rms(q_lat, qn_ref[...], NORM_EPS).astype(BF16)
    for p in range(q_out.shape[1] // pw):
        sl = slice(p * pw, (p + 1) * pw)
        q_out[:, sl] = rope(_dot(qn, wqu_ref[:, sl])).astype(q_out.dtype)
    kvn = _rms(kv_lat, kvn_ref[...], NORM_EPS).astype(BF16)
    for p in range(kv_out.shape[1] // pw):
        sl = slice(p * pw, (p + 1) * pw)
        kv_out[:, sl] = _dot(kvn, wkvu_ref[:, sl]).astype(kv_out.dtype)


def _mla_layout():
    qcols = -np.ones(MLA_HEADS * LANES, np.int64)
    kvcols = np.zeros(MLA_HEADS * LANES, np.int64)
    for h in range(MLA_HEADS):
        qsrc = h * (HEAD_DIM + MLA_ROPE)
        ksrc = h * 2 * HEAD_DIM
        base = h * LANES
        if h % 2 == 0:
            qcols[base:base + MLA_ROPE] = qsrc + HEAD_DIM + np.arange(MLA_ROPE)
            qcols[base + HEAD_DIM:base + LANES] = qsrc + np.arange(HEAD_DIM)
            kvcols[base:base + HEAD_DIM] = ksrc + HEAD_DIM + np.arange(HEAD_DIM)
            kvcols[base + HEAD_DIM:base + LANES] = ksrc + np.arange(HEAD_DIM)
        else:
            qcols[base:base + HEAD_DIM] = qsrc + np.arange(HEAD_DIM)
            qcols[base + HEAD_DIM:base + HEAD_DIM + MLA_ROPE] = qsrc + HEAD_DIM + np.arange(MLA_ROPE)
            kvcols[base:base + HEAD_DIM] = ksrc + np.arange(HEAD_DIM)
            kvcols[base + HEAD_DIM:base + LANES] = ksrc + HEAD_DIM + np.arange(HEAD_DIM)
    pe_offsets = (0, LANES + HEAD_DIM)
    return qcols, kvcols, pe_offsets


def _rope_tables(s, pe_offsets):
    half = MLA_ROPE // 2
    inv_freq = np.float32(ROPE_THETA) ** (-np.arange(half, dtype=np.float32) / np.float32(half))
    ang = np.arange(s, dtype=np.float32)[:, None] * inv_freq[None, :].astype(np.float32)
    cos, sin = np.cos(ang), np.sin(ang)
    c = np.ones((s, 2 * LANES), np.float32)
    sm = np.zeros((s, 2 * LANES), np.float32)
    sp = np.zeros((s, 2 * LANES), np.float32)
    for off in pe_offsets:
        c[:, off:off + half] = cos
        c[:, off + half:off + 2 * half] = cos
        sm[:, off:off + half] = -sin
        sp[:, off + half:off + 2 * half] = sin
    return jnp.asarray(c), jnp.asarray(sm), jnp.asarray(sp)


def c_proj(x2, s, g, w_in, q_norm, w_q_up, kv_norm, w_kv_up, *, tm=512):
    n, d = x2.shape
    qcols, kvcols, pe_offsets = _mla_layout()
    wq = w_in[:, :MLA_Q_RANK].astype(BF16)
    wkv = w_in[:, MLA_Q_RANK:MLA_Q_RANK + MLA_KV_RANK].astype(BF16)
    w_pe = w_in[:, MLA_Q_RANK + MLA_KV_RANK:]
    wpe = jnp.zeros((d, 2 * LANES), F32)
    for off in pe_offsets:
        wpe = wpe.at[:, off:off + MLA_ROPE].set(w_pe)
    wpe = wpe.astype(BF16)
    q_scale = (HEAD_DIM + MLA_ROPE) ** -0.5 * LOG2E
    wqu = jnp.where(jnp.asarray(qcols >= 0)[None, :],
                    w_q_up[:, np.maximum(qcols, 0)] * q_scale, 0.0).astype(BF16)
    wkvu = w_kv_up[:, kvcols].astype(BF16)
    cos, sm, sp = _rope_tables(s, pe_offsets)
    width = MLA_HEADS * LANES
    assert s % tm == 0
    nsb = s // tm
    full = lambda shape: pl.BlockSpec(shape, lambda i: (0, 0))
    tab = pl.BlockSpec((tm, 2 * LANES), lambda i: (i % nsb, 0))
    return pl.pallas_call(
        _c_proj_kernel,
        grid=(n // tm,),
        in_specs=[pl.BlockSpec((tm, d), lambda i: (i, 0)), full((1, d)),
                  full((d, MLA_Q_RANK)), full((d, MLA_KV_RANK)), full((d, 2 * LANES)),
                  full((1, MLA_Q_RANK)), full((1, MLA_KV_RANK)),
                  full((MLA_Q_RANK, width)), full((MLA_KV_RANK, width)), tab, tab, tab],
        out_specs=[pl.BlockSpec((tm, width), lambda i: (i, 0)),
                   pl.BlockSpec((tm, width), lambda i: (i, 0)),
                   pl.BlockSpec((tm, 2 * LANES), lambda i: (i, 0))],
        out_shape=[jax.ShapeDtypeStruct((n, width), BF16),
                   jax.ShapeDtypeStruct((n, width), BF16),
                   jax.ShapeDtypeStruct((n, 2 * LANES), BF16)],
        compiler_params=_cparams(1),
        name="c_proj",
    )(x2, g.reshape(1, d), wq, wkv, wpe, q_norm.reshape(1, -1), kv_norm.reshape(1, -1),
      wqu, wkvu, cos, sm, sp)


def _attn_c_kernel(q_ref, kv_ref, pe_ref, o_ref, kcat_ref, vcat_ref, s_ref, m_ref, acc_ref, *, t):
    qi = pl.program_id(2)
    pw = 2 * LANES

    @pl.when(qi == 0)
    def _():
        lane = lax.broadcasted_iota(I32, kcat_ref.shape, 1)
        is_v = jnp.logical_or(lane < HEAD_DIM, lane >= pw - HEAD_DIM)
        kv = kv_ref[0]
        kcat_ref[...] = jnp.where(is_v, pe_ref[0], kv)
        vcat_ref[...] = jnp.where(is_v, kv, jnp.ones_like(kv))

    _init_softmax_state(m_ref, acc_ref)
    q = q_ref[0]
    first = lax.broadcasted_iota(I32, q.shape, 1) < LANES
    zero = jnp.zeros_like(q)
    q_all = jnp.concatenate([jnp.where(first, q, zero), jnp.where(first, zero, q)], axis=0)

    def rows(kb):
        return pl.ds(pl.multiple_of(kb * t, t), t)

    def scores(kb, c):
        return _dot_nt(q_all, kcat_ref[rows(kb), :])

    def update(s, kb, c):
        _softmax_update(s, vcat_ref[rows(kb), :], m_ref, acc_ref, c)

    _causal_sweep(qi, t, 1, scores, update, s_ref)
    lo = lax.broadcasted_iota(I32, (t, LANES), 1) < HEAD_DIM
    top = acc_ref[0, :t, :LANES]
    bot = acc_ref[0, t:, LANES:]
    o_ref[0] = jnp.where(lo, top / pltpu.roll(top, HEAD_DIM, 1),
                         bot / pltpu.roll(bot, HEAD_DIM, 1)).astype(o_ref.dtype)


def attn_c(q3, kv3, pe3, *, t=512):
    b, s, width = q3.shape
    npair = width // (2 * LANES)
    return pl.pallas_call(
        functools.partial(_attn_c_kernel, t=t),
        grid=(b, npair, s // t),
        in_specs=[pl.BlockSpec((1, t, 2 * LANES), lambda bi, p, qi: (bi, qi, p)),
                  pl.BlockSpec((1, s, 2 * LANES), lambda bi, p, qi: (bi, 0, p)),
                  pl.BlockSpec((1, s, 2 * LANES), lambda bi, p, qi: (bi, 0, 0))],
        out_specs=pl.BlockSpec((1, t, LANES), lambda bi, p, qi: (bi, qi, p)),
        out_shape=jax.ShapeDtypeStruct((b, s, npair * LANES), BF16),
        scratch_shapes=[pltpu.VMEM((s, 2 * LANES), BF16)] * 2 + [pltpu.VMEM((2, 1, 2 * t, t), F32),
                        pltpu.VMEM((1, 2 * t, LANES), F32), pltpu.VMEM((1, 2 * t, 2 * LANES), F32)],
        compiler_params=_cparams(3),
        name="attn_c",
    )(q3, kv3, pe3)


def _attn_d_kernel(q_ref, k_ref, v_ref, o_ref, r_ref, acc_ref, *, t, scale):
    qi = pl.program_id(2)
    q = q_ref[0] * scale
    hw = q.shape[1]
    nh = hw // HEAD_DIM
    head_of_lane = lax.broadcasted_iota(I32, (t, hw), 1) // HEAD_DIM
    zero = jnp.zeros_like(q)
    q_all = jnp.concatenate([jnp.where(head_of_lane == h, q, zero) for h in range(nh)], axis=0)
    r_ref[...] = jnp.zeros(r_ref.shape, F32)
    acc_ref[...] = jnp.zeros(acc_ref.shape, F32)

    def row_in_block(n_rows):
        row = lax.broadcasted_iota(I32, (n_rows, t), 0)
        for _ in range(n_rows // t - 1):
            row = jnp.where(row >= t, row - t, row)
        return row

    ones_ge = jnp.where(row_in_block(2 * t) >= lax.broadcasted_iota(I32, (2 * t, t), 1),
                        1.0, 0.0).astype(BF16)
    strict = row_in_block(nh * t) > lax.broadcasted_iota(I32, (nh * t, t), 1)

    def block(kb, masked):
        off = pl.multiple_of(kb * t, t)
        z = _dot_nt(q_all, k_ref[0, pl.ds(off, t), :])
        lg = jnp.log(1.0 + jnp.exp2(jnp.abs(z) * (-LOG2E)))
        ls = jnp.minimum(z, 0.0) - lg
        lk = ls - z
        if masked:
            lk = jnp.where(strict, lk, 0.0)
        hi = lk.astype(BF16)
        lw = (lk - hi.astype(F32)).astype(BF16)
        csum = _dot(jnp.concatenate([hi, lw], axis=1), ones_ge)
        r_prev = r_ref[...]
        a = jnp.exp(ls + (_lane_tile(r_prev, t // LANES) + (csum - lk)))
        if masked:
            a = jnp.where(strict, a, 0.0)
        acc_ref[...] = acc_ref[...] + _dot(a.astype(BF16), v_ref[0, pl.ds(off, t), :])
        r_ref[...] = r_prev + csum[:, 0:1]

    def live():
        return jnp.max(r_ref[...]) >= EXP_UNDERFLOW

    def cond(carry):
        it, alive = carry
        return jnp.logical_and(it < qi, alive)

    def body(carry):
        it, _ = carry
        block(qi - 1 - it, False)
        return it + 1, live()

    @pl.when(qi == 0)
    def _():
        block(qi, True)

    @pl.when(qi > 0)
    def _():
        block(qi, True)
        block(qi - 1, False)
        lax.while_loop(cond, body, (jnp.int32(1), live()))

    o = acc_ref[:t]
    for h in range(1, nh):
        o = jnp.where(head_of_lane == h, acc_ref[h * t:(h + 1) * t], o)
    o_ref[0] = o.astype(o_ref.dtype)


def attn_d(qkv3, *, t=256):
    b, s, width = qkv3.shape
    hw = MXU_DIM
    ngrp = width // (3 * hw)
    nh = hw // HEAD_DIM
    return pl.pallas_call(
        functools.partial(_attn_d_kernel, t=t, scale=HEAD_DIM ** -0.5),
        grid=(b, ngrp, s // t),
        in_specs=[pl.BlockSpec((1, t, hw), lambda bi, p, qi: (bi, qi, p)),
                  pl.BlockSpec((1, s, hw), lambda bi, p, qi: (bi, 0, ngrp + p)),
                  pl.BlockSpec((1, s, hw), lambda bi, p, qi: (bi, 0, 2 * ngrp + p))],
        out_specs=pl.BlockSpec((1, t, hw), lambda bi, p, qi: (bi, qi, p)),
        out_shape=jax.ShapeDtypeStruct((b, s, ngrp * hw), BF16),
        scratch_shapes=[pltpu.VMEM((nh * t, LANES), F32), pltpu.VMEM((nh * t, hw), F32)],
        compiler_params=_cparams(3),
        name="attn_d",
    )(qkv3, qkv3, qkv3)


def _silu(x):
    return x / (1.0 + jnp.exp(-x))


def _ffn_kernel(x_ref, g_ref, wg_ref, wu_ref, wd_ref, out_ref, *, ff_chunk):
    x = x_ref[...]
    xn = _rms(x, g_ref[...], NORM_EPS).astype(BF16)
    y = x
    for c in range(0, wg_ref.shape[1], ff_chunk):
        h = _silu(_dot(xn, wg_ref[:, c:c + ff_chunk])) * _dot(xn, wu_ref[:, c:c + ff_chunk])
        y = y + _dot(h.astype(BF16), wd_ref[c:c + ff_chunk, :])
    out_ref[...] = y


def ffn(x2, g, wg, wu, wd, *, tm=512):
    n, d = x2.shape
    ff = wg.shape[1]
    ff_chunk = ff // 2 if (ff // 2) % LANES == 0 else ff
    const = lambda shape: pl.BlockSpec(shape, lambda i: (0, 0), pipeline_mode=pl.Buffered(1))
    return pl.pallas_call(
        functools.partial(_ffn_kernel, ff_chunk=ff_chunk),
        grid=(n // tm,),
        in_specs=[pl.BlockSpec((tm, d), lambda i: (i, 0)),
                  pl.BlockSpec((1, d), lambda i: (0, 0)),
                  const((d, ff)), const((d, ff)), const((ff, d))],
        out_specs=pl.BlockSpec((tm, d), lambda i: (i, 0)),
        out_shape=jax.ShapeDtypeStruct((n, d), F32),
        compiler_params=_cparams(1),
        name="ffn",
    )(x2, g.reshape(1, d), wg, wu, wd)


def _router_kernel(x_ref, g_ref, wr_ref, meta_ref, gate_ref, cnt_ref, run_ref):
    i = pl.program_id(0)
    tm = x_ref.shape[0]

    @pl.when(i == 0)
    def _():
        run_ref[...] = jnp.zeros(run_ref.shape, F32)

    xn = _rms(x_ref[...], g_ref[...], NORM_EPS)
    xh = xn.astype(BF16)
    xl = (xn - xh.astype(F32)).astype(BF16)
    logits = _dot(xh, wr_ref[0]) + _dot(xl, wr_ref[0]) + _dot(xh, wr_ref[1])
    lane = lax.broadcasted_iota(I32, (tm, LANES), 1)
    lg = jnp.where(lane < N_EXPERTS, logits, NEG_INF)
    m1 = jnp.max(lg, axis=1, keepdims=True)
    e1 = jnp.min(jnp.where(lg == m1, lane, LANES), axis=1, keepdims=True)
    lg2 = jnp.where(lane == e1, NEG_INF, lg)
    m2 = jnp.max(lg2, axis=1, keepdims=True)
    e2 = jnp.min(jnp.where(lg2 == m2, lane, LANES), axis=1, keepdims=True)
    ex = jnp.exp(m2 - m1)
    g1 = 1.0 / (1.0 + ex)
    g2 = ex / (1.0 + ex)
    hit1 = lane == e1
    hit2 = lane == e2
    onehot = jnp.where(jnp.logical_or(hit1, hit2), 1.0, 0.0)
    earlier = (lax.broadcasted_iota(I32, (tm, tm), 0) > lax.broadcasted_iota(I32, (tm, tm), 1))
    before = _dot(jnp.where(earlier, 1.0, 0.0).astype(BF16), onehot.astype(BF16)) + run_ref[0:1, :]
    pos1 = jnp.sum(jnp.where(hit1, before, 0.0), axis=1, keepdims=True).astype(I32)
    pos2 = jnp.sum(jnp.where(hit2, before, 0.0), axis=1, keepdims=True).astype(I32)
    run_ref[...] = run_ref[...] + jnp.sum(onehot, axis=0, keepdims=True)
    meta_ref[...] = jnp.where(lane == 0, e1, jnp.where(lane == 1, e2, jnp.where(
        lane == 2, pos1, jnp.where(lane == 3, pos2, 0))))
    gate_ref[...] = jnp.where(lane == 0, g1, jnp.where(lane == 1, g2, 0.0))
    cnt_ref[...] = run_ref[...]


def router(x2, g, w_router, *, tm=512):
    n, d = x2.shape
    wr = jnp.zeros((d, LANES), F32).at[:, :N_EXPERTS].set(w_router)
    wr_hi = wr.astype(BF16)
    wr = jnp.stack([wr_hi, (wr - wr_hi.astype(F32)).astype(BF16)])
    row = lambda w: pl.BlockSpec((tm, w), lambda i: (i, 0))
    return pl.pallas_call(
        _router_kernel,
        grid=(n // tm,),
        in_specs=[row(d), pl.BlockSpec((1, d), lambda i: (0, 0)),
                  pl.BlockSpec((2, d, LANES), lambda i: (0, 0, 0))],
        out_specs=[row(LANES), row(LANES), pl.BlockSpec((8, LANES), lambda i: (0, 0))],
        out_shape=[jax.ShapeDtypeStruct((n, LANES), I32),
                   jax.ShapeDtypeStruct((n, LANES), F32), jax.ShapeDtypeStruct((8, LANES), F32)],
        scratch_shapes=[pltpu.VMEM((8, LANES), F32)],
        compiler_params=_cparams(1),
        name="router",
    )(x2, g.reshape(1, d), wr)


ROW_TILE = 8


def _to_row_tiles(ref, x):
    rows = x.shape[0]
    for c in range(ROW_TILE):
        ref[pl.ds(c, rows, stride=ROW_TILE), :] = x[:, c * LANES:(c + 1) * LANES]


def _from_row_tiles(ref):
    rows = ref.shape[0] // ROW_TILE
    return jnp.concatenate([ref[pl.ds(c, rows, stride=ROW_TILE), :] for c in range(ROW_TILE)], axis=1)


def _row_copy(src_ref, src_row, dst_ref, dst_row, sem):
    def tile(ref, row):
        return ref.at[pl.ds(pl.multiple_of(row * ROW_TILE, ROW_TILE), ROW_TILE)]
    return pltpu.make_async_copy(tile(src_ref, src_row), tile(dst_ref, dst_row), sem)


def _dispatch_kernel(dest_ref, pend_ref, x_ref, g_ref, xb_ref, xn_ref, sem, *, tb, n_steps):
    i = pl.program_id(0)
    slot = lax.rem(i, 2)

    def drain(s):
        for k in range(2):
            pltpu.make_async_copy(xn_ref.at[s], xb_ref.at[pl.ds(0, tb * ROW_TILE)], sem.at[s]).wait()

    @pl.when(i == 0)
    def _():
        xn_ref[0] = jnp.zeros(xn_ref.shape[1:], F32)

        def zero_block(row):
            first = pl.multiple_of(row * ROW_TILE, tb * ROW_TILE)
            return pltpu.make_async_copy(
                xn_ref.at[0], xb_ref.at[pl.ds(first, tb * ROW_TILE)], sem.at[0])

        def has_rows(e):
            return pend_ref[e] > (pend_ref[e - 1] if e > 0 else 0)

        total = pend_ref[N_EXPERTS - 1]
        n_rows = xb_ref.shape[0] // ROW_TILE
        for start in (True, False):
            for e in range(N_EXPERTS):
                @pl.when(has_rows(e))
                def _():
                    cp = zero_block(pend_ref[e] - tb)
                    cp.start() if start else cp.wait()

                @pl.when(total + e * tb < n_rows)
                def _():
                    cp = zero_block(total + e * tb)
                    cp.start() if start else cp.wait()

    @pl.when(i >= 2)
    def _():
        drain(slot)

    _to_row_tiles(xn_ref.at[slot], _rms(x_ref[...], g_ref[...], NORM_EPS))

    def issue(tt, carry):
        for k in range(2):
            _row_copy(xn_ref.at[slot], tt, xb_ref, dest_ref[2 * tt + k], sem.at[slot]).start()
        return carry

    lax.fori_loop(0, tb, issue, 0, unroll=8)

    @pl.when(i == n_steps - 1)
    def _():
        drain(slot)
        if n_steps > 1:
            drain(1 - slot)


def dispatch(dest_flat, pad_end, x2, g, n_rows):
    n, d = x2.shape
    assert d == ROW_TILE * LANES
    tb = MOE_ROWS
    return pl.pallas_call(
        functools.partial(_dispatch_kernel, tb=tb, n_steps=n // tb),
        grid=(n // tb,),
        in_specs=[pl.BlockSpec((2 * tb,), lambda i: (i,), memory_space=pltpu.SMEM),
                  pl.BlockSpec(memory_space=pltpu.SMEM),
                  pl.BlockSpec((tb, d), lambda i: (i, 0)),
                  pl.BlockSpec((1, d), lambda i: (0, 0))],
        out_specs=pl.BlockSpec(memory_space=pl.ANY),
        out_shape=jax.ShapeDtypeStruct((n_rows * ROW_TILE, LANES), F32),
        scratch_shapes=[pltpu.VMEM((2, tb * ROW_TILE, LANES), F32), pltpu.SemaphoreType.DMA((2,))],
        compiler_params=_cparams(1),
        name="moe_dispatch",
    )(dest_flat, pad_end.astype(I32), x2, g.reshape(1, d))


def _expert_kernel(be_ref, nu_ref, x_ref, wg_ref, wu_ref, wd_ref, y_ref, *, ff_chunk):
    i = pl.program_id(0)

    @pl.when(i < nu_ref[0])
    def _():
        xb = _from_row_tiles(x_ref).astype(BF16)
        y = jnp.zeros(xb.shape, F32)
        for c in range(0, wg_ref.shape[2], ff_chunk):
            h = (_silu(_dot(xb, wg_ref[0, :, c:c + ff_chunk]))
                 * _dot(xb, wu_ref[0, :, c:c + ff_chunk]))
            y = y + _dot(h.astype(BF16), wd_ref[0, c:c + ff_chunk, :])
        _to_row_tiles(y_ref, y)

    @pl.when(i >= nu_ref[0])
    def _():
        y_ref[...] = jnp.zeros(y_ref.shape, F32)


def experts(block_e, n_used, xb, wg, wu, wd, *, ff_chunk=512):
    d, ff = wg.shape[1], wg.shape[2]
    assert ff % ff_chunk == 0
    wspec = lambda shape: pl.BlockSpec(shape, lambda i, be, nu: (be[i], 0, 0),
                                       pipeline_mode=pl.Buffered(1))
    blk = pl.BlockSpec((MOE_ROWS * ROW_TILE, LANES), lambda i, be, nu: (i, 0))
    return pl.pallas_call(
        functools.partial(_expert_kernel, ff_chunk=ff_chunk),
        grid_spec=pltpu.PrefetchScalarGridSpec(
            num_scalar_prefetch=2,
            grid=(xb.shape[0] // (MOE_ROWS * ROW_TILE),),
            in_specs=[blk, wspec((1, d, ff)), wspec((1, d, ff)), wspec((1, ff, d))],
            out_specs=blk,
        ),
        out_shape=jax.ShapeDtypeStruct(xb.shape, F32),
        compiler_params=_cparams(1),
        name="moe_experts",
    )(block_e, n_used, xb, wg, wu, wd)


def _combine_kernel(dest_ref, dest_next_ref, x_ref, gate_ref, fg_ref, yb_ref, out_ref, buf_ref, sem,
                    *, tb, final, n_steps):
    i = pl.program_id(0)
    slot = lax.rem(i, 2)

    def gather(d_ref, s):
        def issue(tt, carry):
            for k in range(2):
                _row_copy(yb_ref, d_ref[2 * tt + k], buf_ref.at[s, k], tt, sem.at[s]).start()
            return carry

        lax.fori_loop(0, tb, issue, 0, unroll=8)

    @pl.when(i == 0)
    def _():
        gather(dest_ref, slot)

    @pl.when(i + 1 < n_steps)
    def _():
        gather(dest_next_ref, 1 - slot)

    for k in range(2):
        pltpu.make_async_copy(yb_ref.at[pl.ds(0, tb * ROW_TILE)], buf_ref.at[slot, k],
                              sem.at[slot]).wait()
    gate = gate_ref[...]
    y = (x_ref[...] + gate[:, 0:1] * _from_row_tiles(buf_ref.at[slot, 0])
         + gate[:, 1:2] * _from_row_tiles(buf_ref.at[slot, 1]))
    if final:
        y = _rms(y, fg_ref[...], NORM_EPS)
    out_ref[...] = y


def combine(dest_flat, x2, gates, yb, final_g, *, final, tb=512):
    n, d = x2.shape
    n_steps = n // tb
    return pl.pallas_call(
        functools.partial(_combine_kernel, tb=tb, final=final, n_steps=n_steps),
        grid=(n_steps,),
        in_specs=[pl.BlockSpec((2 * tb,), lambda i: (i,), memory_space=pltpu.SMEM),
                  pl.BlockSpec((2 * tb,), lambda i: (jnp.minimum(i + 1, n_steps - 1),),
                               memory_space=pltpu.SMEM),
                  pl.BlockSpec((tb, d), lambda i: (i, 0)),
                  pl.BlockSpec((tb, LANES), lambda i: (i, 0)),
                  pl.BlockSpec((1, d), lambda i: (0, 0)),
                  pl.BlockSpec(memory_space=pl.ANY)],
        out_specs=pl.BlockSpec((tb, d), lambda i: (i, 0)),
        out_shape=jax.ShapeDtypeStruct((n, d), F32),
        scratch_shapes=[pltpu.VMEM((2, 2, tb * ROW_TILE, LANES), F32),
                        pltpu.SemaphoreType.DMA((2,))],
        compiler_params=_cparams(1),
        name="moe_combine",
    )(dest_flat, dest_flat, x2, gates, final_g.reshape(1, d), yb)


def moe(x2, g, w_router, wg, wu, wd, final_g, *, final):
    n, d = x2.shape
    meta, gates, cnt = router(x2, g, w_router)
    counts = cnt[0, :N_EXPERTS].astype(I32)
    padded = (counts + MOE_ROWS - 1) // MOE_ROWS * MOE_ROWS
    pad_end = jnp.cumsum(padded)
    pad_start = pad_end - padded
    dest = (pad_start[meta[:, 0:2]] + meta[:, 2:4]).reshape(-1).astype(I32)
    n_blocks = -(-2 * n // MOE_ROWS) + N_EXPERTS
    starts = jnp.arange(n_blocks, dtype=I32) * MOE_ROWS
    block_e = jnp.minimum(jnp.sum((pad_end[None, :] <= starts[:, None]).astype(I32), axis=1),
                          N_EXPERTS - 1)
    n_used = (pad_end[-1:] // MOE_ROWS).astype(I32)
    xb = dispatch(dest, pad_end, x2, g, n_blocks * MOE_ROWS)
    yb = experts(block_e, n_used, xb, wg.astype(BF16), wu.astype(BF16), wd.astype(BF16))
    return combine(dest, x2, gates, yb, final_g, final=final)


def _bf16(w):
    return w.astype(BF16)


def mixer_a(x2, b, norm, w_qkv, w_o):
    d = x2.shape[1]
    n_groups = len(A_PATTERN)
    dils = tuple(dil for _, dil in A_PATTERN)
    gw = A_HEADS * HEAD_DIM
    w = w_qkv.reshape(d, 3, n_groups, gw)
    w = w * jnp.asarray([HEAD_DIM ** -0.5 * LOG2E, 1.0, 1.0], F32).reshape(1, 3, 1, 1)
    w = _bf16(w.transpose(0, 2, 1, 3).reshape(d, n_groups * 3 * gw))
    views = a_proj(x2, norm, w, dils)
    slopes = 2.0 ** (-8.0 * np.arange(1, n_groups * A_HEADS + 1) / (n_groups * A_HEADS))
    outs, lses = [], []
    for gi, (window, dil) in enumerate(A_PATTERN):
        assert window // dil == QB
        o, lse = attn_a_group(views[gi], b, dil, slopes[gi * A_HEADS:(gi + 1) * A_HEADS])
        outs.append(o)
        lses.append(lse)
    return mix_oproj_res(x2, outs, lses, _bf16(w_o), dils)


def mixer_b(x2, b, norm, w_qkv, lq1, lk1, lq2, lk2, subln, w_o, *, layer, t=512):
    n = x2.shape[0]
    n_q = w_qkv.shape[1] // 3
    w = jnp.concatenate([w_qkv[:, :n_q] * (HEAD_DIM ** -0.5 * LOG2E), w_qkv[:, n_q:]], axis=1)
    qkv = norm_proj(x2, norm, _bf16(w)).reshape(b, n // b, -1)
    o = attn_b(qkv, lq1, lk1, lq2, lk2, subln, layer, t=t)
    return oproj_res(x2, o.reshape(n, -1), _bf16(w_o))


def mixer_c(x2, b, norm, w_in, q_norm, w_q_up, kv_norm, w_kv_up, w_o, *, t=512):
    n = x2.shape[0]
    s = n // b
    q, kv, pe = c_proj(x2, s, norm, w_in, q_norm, w_q_up, kv_norm, w_kv_up)
    o = attn_c(q.reshape(b, s, -1), kv.reshape(b, s, -1), pe.reshape(b, s, -1), t=t)
    return oproj_res(x2, o.reshape(n, -1), _bf16(w_o))


def mixer_d(x2, b, norm, w_qkv, w_o, *, t=256):
    n = x2.shape[0]
    qkv = norm_proj(x2, norm, _bf16(w_qkv)).reshape(b, n // b, -1)
    o = attn_d(qkv, t=t)
    return oproj_res(x2, o.reshape(n, -1), _bf16(w_o))
def kernel(x, a_norm, a_w_qkv, a_w_o, ffn0_norm, ffn0_w_gate, ffn0_w_up, ffn0_w_down, b_norm, b_w_qkv, b_lambda_q1, b_lambda_k1, b_lambda_q2, b_lambda_k2, b_subln, b_w_o, moe1_norm, moe1_w_router, moe1_w_gate, moe1_w_up, moe1_w_down, c_norm, c_w_in, c_q_norm, c_w_q_up, c_kv_norm, c_w_kv_up, c_w_o, ffn2_norm, ffn2_w_gate, ffn2_w_up, ffn2_w_down, d_norm, d_w_qkv, d_w_o, moe3_norm, moe3_w_router, moe3_w_gate, moe3_w_up, moe3_w_down, final_norm):
    b, s, d = x.shape
    x2 = x.reshape(b * s, d)
    bf = _bf16
    x2 = mixer_a(x2, b, a_norm, a_w_qkv, a_w_o)
    x2 = ffn(x2, ffn0_norm, bf(ffn0_w_gate), bf(ffn0_w_up), bf(ffn0_w_down))
    x2 = mixer_b(x2, b, b_norm, b_w_qkv, b_lambda_q1, b_lambda_k1, b_lambda_q2, b_lambda_k2,
                 b_subln, b_w_o, layer=1)
    x2 = moe(x2, moe1_norm, moe1_w_router, moe1_w_gate, moe1_w_up, moe1_w_down, final_norm,
             final=False)
    x2 = mixer_c(x2, b, c_norm, c_w_in, c_q_norm, c_w_q_up, c_kv_norm, c_w_kv_up, c_w_o)
    x2 = ffn(x2, ffn2_norm, bf(ffn2_w_gate), bf(ffn2_w_up), bf(ffn2_w_down))
    x2 = mixer_d(x2, b, d_norm, d_w_qkv, d_w_o)
    x2 = moe(x2, moe3_norm, moe3_w_router, moe3_w_gate, moe3_w_up, moe3_w_down, final_norm,
             final=True)
    return x2.reshape(b, s, d)
```

```python
import functools
import math

import numpy as np
import jax
import jax.numpy as jnp
from jax import lax
from jax.experimental import pallas as pl
from jax.experimental.pallas import tpu as pltpu

F32 = jnp.float32
BF16 = jnp.bfloat16
I32 = jnp.int32

LANES = 128
MXU_DIM = 256
HEAD_DIM = 64
NORM_EPS = 1e-6
SUBLN_EPS = 1e-5
ROPE_THETA = 10000.0
A_PATTERN = ((128, 1), (512, 4), (2048, 16))
A_HEADS = 8
QB = 128
A_SUB = 4
N_EXPERTS = 8
MOE_ROWS = 512
VMEM_LIMIT = 56 * 1024 * 1024
NEG_INF = float("-inf")
EXP_UNDERFLOW = -105.0


def _cparams(n_axes, vmem=VMEM_LIMIT):
    return pltpu.CompilerParams(dimension_semantics=("arbitrary",) * n_axes,
                                vmem_limit_bytes=vmem)


def _dot(a, b):
    return jnp.dot(a, b, preferred_element_type=F32)


def _dot_nt(a, b):
    return lax.dot_general(a, b, (((1,), (1,)), ((), ())), preferred_element_type=F32)


def _rms(x, g, eps):
    return x * lax.rsqrt(jnp.mean(x * x, axis=-1, keepdims=True) + eps) * g


def _lane_tile(x, n):
    return x if n == 1 else jnp.concatenate([x] * n, axis=1)


def _norm_proj_kernel(x_ref, g_ref, w_ref, o_ref, *, col_chunk):
    xn = _rms(x_ref[...], g_ref[...], NORM_EPS).astype(BF16)
    n = o_ref.shape[1]
    for c in range(0, n, col_chunk):
        o_ref[:, c:c + col_chunk] = _dot(xn, w_ref[:, c:c + col_chunk]).astype(o_ref.dtype)


def norm_proj(x2, g, w, *, tm=512, col_chunk=512):
    n, d = x2.shape
    n_out = w.shape[1]
    assert n % tm == 0 and n_out % col_chunk == 0
    return pl.pallas_call(
        functools.partial(_norm_proj_kernel, col_chunk=col_chunk),
        grid=(n // tm,),
        in_specs=[pl.BlockSpec((tm, d), lambda i: (i, 0)),
                  pl.BlockSpec((1, d), lambda i: (0, 0)),
                  pl.BlockSpec((d, n_out), lambda i: (0, 0))],
        out_specs=pl.BlockSpec((tm, n_out), lambda i: (i, 0)),
        out_shape=jax.ShapeDtypeStruct((n, n_out), BF16),
        compiler_params=_cparams(1),
        name="norm_proj",
    )(x2, g.reshape(1, d), w)


def _oproj_kernel(x_ref, o_ref, w_ref, out_ref):
    out_ref[...] = x_ref[...] + _dot(o_ref[...], w_ref[...])


def oproj_res(x2, o2, w, *, tm=1024):
    n, d = x2.shape
    k = o2.shape[1]
    return pl.pallas_call(
        _oproj_kernel,
        grid=(n // tm,),
        in_specs=[pl.BlockSpec((tm, d), lambda i: (i, 0)),
                  pl.BlockSpec((tm, k), lambda i: (i, 0)),
                  pl.BlockSpec((k, d), lambda i: (0, 0))],
        out_specs=pl.BlockSpec((tm, d), lambda i: (i, 0)),
        out_shape=jax.ShapeDtypeStruct((n, d), F32),
        compiler_params=_cparams(1),
        name="oproj_res",
    )(x2, o2, w)


LOG2E = math.log2(math.e)
LN2 = math.log(2.0)


def _a_proj_kernel(x_ref, g_ref, w_ref, *rest, dils, gcols):
    outs, scr = rest[:len(dils)], rest[len(dils)]
    xn = _rms(x_ref[...], g_ref[...], NORM_EPS).astype(BF16)
    tm = x_ref.shape[0]
    for gi, dil in enumerate(dils):
        res = _dot(xn, w_ref[:, gi * gcols:(gi + 1) * gcols])
        if dil == 1:
            outs[gi][...] = res.astype(BF16)
        else:
            for c in range(gcols // LANES):
                scr[c] = res[:, c * LANES:(c + 1) * LANES]
            for r in range(dil):
                for c in range(gcols // LANES):
                    col = r * gcols + c * LANES
                    outs[gi][:, col:col + LANES] = (
                        scr[c, pl.ds(r, tm // dil, stride=dil), :].astype(BF16))


def a_proj(x2, g, w, dils, *, tm=512):
    n, d = x2.shape
    gcols = w.shape[1] // len(dils)
    return pl.pallas_call(
        functools.partial(_a_proj_kernel, dils=dils, gcols=gcols),
        grid=(n // tm,),
        in_specs=[pl.BlockSpec((tm, d), lambda i: (i, 0)),
                  pl.BlockSpec((1, d), lambda i: (0, 0)),
                  pl.BlockSpec((d, w.shape[1]), lambda i: (0, 0))],
        out_specs=[pl.BlockSpec((tm // dil, dil * gcols), lambda i: (i, 0)) for dil in dils],
        out_shape=[jax.ShapeDtypeStruct((n // dil, dil * gcols), BF16) for dil in dils],
        scratch_shapes=[pltpu.VMEM((gcols // LANES, tm, LANES), F32)],
        compiler_params=_cparams(1),
        name="a_proj",
    )(x2, g.reshape(1, d), w)


def _attn_a_kernel(q_ref, kc_ref, kp_ref, vc_ref, vp_ref, bc_ref, bp_ref, o_ref, lse_ref):
    first_step = pl.program_id(2) == 0
    hw = MXU_DIM
    nh = hw // HEAD_DIM
    head_of_lane = lax.broadcasted_iota(I32, (QB, hw), 1) // HEAD_DIM
    for sub in range(q_ref.shape[1] // QB):
        own = slice(sub * QB, (sub + 1) * QB)
        before = slice((sub - 1) * QB, sub * QB)
        for half in range(q_ref.shape[2] // hw):
            sl = slice(half * hw, (half + 1) * hw)
            tab = slice(half * nh * QB, (half + 1) * nh * QB)
            q = q_ref[0, own, sl]
            kp, vp = ((kp_ref[0, :, sl], vp_ref[0, :, sl]) if sub == 0
                      else (kc_ref[0, before, sl], vc_ref[0, before, sl]))
            zero = jnp.zeros_like(q)
            q_all = jnp.concatenate([jnp.where(head_of_lane == h, q, zero) for h in range(nh)],
                                    axis=0)
            sc = _dot_nt(q_all, kc_ref[0, own, sl]) + bc_ref[tab, :]
            sp = _dot_nt(q_all, kp) + bp_ref[tab, :]
            if sub == 0:
                sp = jnp.where(first_step, NEG_INF, sp)
            m = jnp.maximum(jnp.max(sc, axis=1, keepdims=True), jnp.max(sp, axis=1, keepdims=True))
            pc = jnp.exp2(sc - m)
            pp = jnp.exp2(sp - m)
            l = jnp.sum(pc, axis=1, keepdims=True) + jnp.sum(pp, axis=1, keepdims=True)
            acc = _dot(pc.astype(BF16), vc_ref[0, own, sl]) + _dot(pp.astype(BF16), vp)
            lse = (m + jnp.log2(l)) * LN2
            o = jnp.zeros((QB, hw), F32)
            ls = jnp.zeros((QB, hw), F32)
            for h in range(nh):
                rows = slice(h * QB, (h + 1) * QB)
                mine = head_of_lane == h
                o = jnp.where(mine, acc[rows] / l[rows], o)
                ls = jnp.where(mine, lse[rows], ls)
            o_ref[0, own, sl] = o.astype(o_ref.dtype)
            lse_ref[0, own, sl] = ls


def _a_bias_tables(dil, slopes):
    rel_c = np.arange(QB)[:, None] - np.arange(QB)[None, :]
    rel_p = rel_c + QB
    sl = np.asarray(slopes, np.float64)[:, None, None] * dil * LOG2E
    bc = np.where(rel_c >= 0, -sl * rel_c, -np.inf).reshape(-1, QB)
    bp = np.where(rel_p <= QB, -sl * rel_p, -np.inf).reshape(-1, QB)
    return jnp.asarray(bc, F32), jnp.asarray(bp, F32)


def attn_a_group(view2, b, dil, slopes):
    gw = A_HEADS * HEAD_DIM
    l = view2.shape[0] // b
    n_sub = min(A_SUB, l // QB)
    tq = n_sub * QB
    assert l % tq == 0 and view2.shape[1] == dil * 3 * gw
    view = view2.reshape(b, l, dil * 3 * gw)
    bc, bp = _a_bias_tables(dil, slopes)

    def spec(which, prev):
        if prev:
            return pl.BlockSpec((1, QB, gw), lambda bi, r, j: (
                bi, jnp.maximum(j * n_sub - 1, 0), r * 3 + which))
        return pl.BlockSpec((1, tq, gw), lambda bi, r, j: (bi, j, r * 3 + which))

    tab = pl.BlockSpec((A_HEADS * QB, QB), lambda bi, r, j: (0, 0))
    o, lse = pl.pallas_call(
        _attn_a_kernel,
        grid=(b, dil, l // tq),
        in_specs=[spec(0, False), spec(1, False), spec(1, True), spec(2, False), spec(2, True),
                  tab, tab],
        out_specs=[pl.BlockSpec((1, tq, gw), lambda bi, r, j: (bi, j, r)),
                   pl.BlockSpec((1, tq, gw), lambda bi, r, j: (bi, j, r))],
        out_shape=[jax.ShapeDtypeStruct((b, l, dil * gw), BF16),
                   jax.ShapeDtypeStruct((b, l, dil * gw), F32)],
        compiler_params=_cparams(3),
        name=f"attn_a_d{dil}",
    )(view, view, view, view, view, bc, bp)
    return o.reshape(b * l, dil * gw), lse.reshape(b * l, dil * gw)


def _mix_oproj_kernel(x_ref, *rest, dils, gw):
    ng = len(dils)
    o_refs, l_refs = rest[:ng], rest[ng:2 * ng]
    w_ref, out_ref, o_scr, l_scr = rest[2 * ng:]
    tm = x_ref.shape[0]
    os_, ls = [], []
    for gi, dil in enumerate(dils):
        if dil == 1:
            os_.append(o_refs[gi][...].astype(F32))
            ls.append(l_refs[gi][...])
        else:
            nct = gw // LANES
            for r in range(dil):
                rows = pl.ds(r, tm // dil, stride=dil)
                for c in range(nct):
                    cols = slice(r * gw + c * LANES, r * gw + (c + 1) * LANES)
                    o_scr[gi, c, rows, :] = o_refs[gi][:, cols].astype(F32)
                    l_scr[gi, c, rows, :] = l_refs[gi][:, cols]
            os_.append(jnp.concatenate([o_scr[gi, c] for c in range(nct)], axis=1))
            ls.append(jnp.concatenate([l_scr[gi, c] for c in range(nct)], axis=1))
    m = functools.reduce(jnp.maximum, ls)
    es = [jnp.exp(v - m) for v in ls]
    den = functools.reduce(lambda a, c: a + c, es)
    mixed = functools.reduce(lambda a, c: a + c, [(e / den) * o for e, o in zip(es, os_)])
    out_ref[...] = x_ref[...] + _dot(mixed.astype(BF16), w_ref[...])


def mix_oproj_res(x2, outs, lses, w, dils, *, tm=1024):
    n, d = x2.shape
    gw = w.shape[0]
    ng = len(dils)
    views = [pl.BlockSpec((tm // dil, dil * gw), lambda i: (i, 0)) for dil in dils]
    return pl.pallas_call(
        functools.partial(_mix_oproj_kernel, dils=dils, gw=gw),
        grid=(n // tm,),
        in_specs=[pl.BlockSpec((tm, d), lambda i: (i, 0))] + views + views
                 + [pl.BlockSpec((gw, d), lambda i: (0, 0))],
        out_specs=pl.BlockSpec((tm, d), lambda i: (i, 0)),
        out_shape=jax.ShapeDtypeStruct((n, d), F32),
        scratch_shapes=[pltpu.VMEM((ng, gw // LANES, tm, LANES), F32)] * 2,
        compiler_params=_cparams(1),
        name="mix_oproj_res",
    )(x2, *outs, *lses, w)


def _softmax_update(s, v, m_ref, acc_ref, idx):
    tk = s.shape[1]
    m_prev = m_ref[idx]
    m_next = jnp.maximum(m_prev, jnp.max(s, axis=1, keepdims=True))
    p = jnp.exp2(s - _lane_tile(m_next, tk // LANES))
    alpha = jnp.exp2(m_prev - m_next)
    acc_ref[idx] = (_lane_tile(alpha, acc_ref.shape[-1] // LANES) * acc_ref[idx]
                    + _dot(p.astype(BF16), v))
    m_ref[idx] = m_next


def _init_softmax_state(m_ref, acc_ref):
    m_ref[...] = jnp.full(m_ref.shape, NEG_INF, F32)
    acc_ref[...] = jnp.zeros(acc_ref.shape, F32)


def _causal_sweep(qi, t, n_chain, scores, update, s_ref):
    def qk(kb, buf):
        for c in range(n_chain):
            s_ref[buf, c] = scores(kb, c)

    def upd(kb, buf, masked):
        for c in range(n_chain):
            s = s_ref[buf, c]
            if masked:
                s = jnp.where(_causal_mask(s.shape[0], t), s, NEG_INF)
            update(s, kb, c)

    qk(0, 0)

    def body(i, carry):
        kb = 2 * i
        qk(kb + 1, 1)
        upd(kb, 0, False)
        qk(kb + 2, 0)
        upd(kb + 1, 1, False)
        return carry

    lax.fori_loop(0, lax.shift_right_logical(qi, 1), body, 0)
    odd = jnp.bitwise_and(qi, 1) == 1

    @pl.when(odd)
    def _():
        qk(qi, 1)
        upd(qi - 1, 0, False)
        upd(qi, 1, True)

    @pl.when(jnp.logical_not(odd))
    def _():
        upd(qi, 0, True)


def _split3(x):
    hi = x.astype(BF16).astype(F32)
    mid = (x - hi).astype(BF16).astype(F32)
    lo = (x - hi - mid).astype(BF16).astype(F32)
    return hi, mid, lo


def _causal_mask(n_rows, t):
    row = lax.broadcasted_iota(I32, (n_rows, t), 0)
    for _ in range(n_rows // t - 1):
        row = jnp.where(row >= t, row - t, row)
    return row >= lax.broadcasted_iota(I32, (n_rows, t), 1)


def _attn_b_kernel(q_ref, k_ref, v_ref, lq1, lk1, lq2, lk2, g_ref, o_ref, kb_ref, vb_ref, s_ref,
                   m_ref, acc_ref, *, t, slopes, lam_init):
    h = pl.program_id(1)
    qi = pl.program_id(2)
    n_bias = 3

    @pl.when(qi == 0)
    def _():
        slope = jnp.float32(0.0)
        for hh, sv in enumerate(slopes):
            slope = jnp.where(h == hh, jnp.float32(sv), slope)
        shape = (k_ref.shape[1], LANES)
        lane = lax.broadcasted_iota(I32, shape, 1)
        parts = _split3(slope * lax.broadcasted_iota(I32, shape, 0).astype(F32))
        bias = jnp.zeros(shape, F32)
        for j, part in enumerate(parts):
            bias = jnp.where(lane == j, part, bias)
        kb_ref[:, :LANES] = k_ref[0]
        kb_ref[:, LANES:] = bias.astype(BF16)
        vb_ref[:, :LANES] = v_ref[0]
        vb_ref[:, LANES:] = jnp.ones(shape, BF16)

    q = q_ref[0].astype(F32)
    lane = lax.broadcasted_iota(I32, (t, LANES), 1)
    ones = jnp.where(lane < n_bias, 1.0, 0.0)
    q_all = jnp.concatenate(
        [jnp.concatenate([jnp.where(lane < HEAD_DIM, q, 0.0), ones], axis=1),
         jnp.concatenate([jnp.where(lane >= HEAD_DIM, q, 0.0), ones], axis=1)], axis=0).astype(BF16)
    _init_softmax_state(m_ref, acc_ref)

    def rows(kb):
        return pl.ds(pl.multiple_of(kb * t, t), t)

    def scores(kb, c):
        return _dot_nt(q_all, kb_ref[rows(kb), :])

    def update(s, kb, c):
        _softmax_update(s, vb_ref[rows(kb), :], m_ref, acc_ref, c)

    _causal_sweep(qi, t, 1, scores, update, s_ref)

    lam = (jnp.exp(jnp.sum(lq1[...] * lk1[...], axis=1, keepdims=True))
           - jnp.exp(jnp.sum(lq2[...] * lk2[...], axis=1, keepdims=True)) + lam_init)
    acc = acc_ref[0]
    o = acc[:t, :LANES] / acc[:t, LANES:] - lam * (acc[t:, :LANES] / acc[t:, LANES:])
    o = _rms(o, g_ref[...], SUBLN_EPS) * (1.0 - lam_init)
    o_ref[0] = o.astype(o_ref.dtype)


def attn_b(qkv3, lq1, lk1, lq2, lk2, subln, layer, *, t=512):
    b, s, width = qkv3.shape
    nh = width // (3 * LANES)
    slopes = tuple(float(2.0 ** (-8.0 * (i + 1) / nh)) * LOG2E for i in range(nh))
    lam_init = 0.8 - 0.6 * math.exp(-0.3 * layer)
    vec = pl.BlockSpec((1, HEAD_DIM), lambda bi, h, qi: (0, 0))
    return pl.pallas_call(
        functools.partial(_attn_b_kernel, t=t, slopes=slopes, lam_init=lam_init),
        grid=(b, nh, s // t),
        in_specs=[pl.BlockSpec((1, t, LANES), lambda bi, h, qi: (bi, qi, h)),
                  pl.BlockSpec((1, s, LANES), lambda bi, h, qi: (bi, 0, nh + h)),
                  pl.BlockSpec((1, s, LANES), lambda bi, h, qi: (bi, 0, 2 * nh + h)),
                  vec, vec, vec, vec,
                  pl.BlockSpec((1, LANES), lambda bi, h, qi: (0, 0))],
        out_specs=pl.BlockSpec((1, t, LANES), lambda bi, h, qi: (bi, qi, h)),
        out_shape=jax.ShapeDtypeStruct((b, s, nh * LANES), BF16),
        scratch_shapes=[pltpu.VMEM((s, 2 * LANES), BF16)] * 2 + [pltpu.VMEM((2, 1, 2 * t, t), F32),
                        pltpu.VMEM((1, 2 * t, LANES), F32), pltpu.VMEM((1, 2 * t, 2 * LANES), F32)],
        compiler_params=_cparams(3),
        name="attn_b",
    )(qkv3, qkv3, qkv3, lq1.reshape(1, -1), lk1.reshape(1, -1), lq2.reshape(1, -1),
      lk2.reshape(1, -1), subln.reshape(1, -1))


MLA_HEADS = 16
MLA_Q_RANK = 384
MLA_KV_RANK = 256
MLA_ROPE = 32


def _c_proj_kernel(x_ref, g_ref, wq_ref, wkv_ref, wpe_ref, qn_ref, kvn_ref, wqu_ref, wkvu_ref,
                   cos_ref, sm_ref, sp_ref, q_out, kv_out, pe_out):
    xn = _rms(x_ref[...], g_ref[...], NORM_EPS).astype(BF16)
    q_lat = _dot(xn, wq_ref[...])
    kv_lat = _dot(xn, wkv_ref[...])
    cos, sm, sp = cos_ref[...], sm_ref[...], sp_ref[...]
    pw = cos.shape[1]

    def rope(v):
        return v * cos + pltpu.roll(v, pw - MLA_ROPE // 2, 1) * sm + pltpu.roll(v, MLA_ROPE // 2, 1) * sp

    pe_out[...] = rope(_dot(xn, wpe_ref[...])).astype(pe_out.dtype)
    qn = _rms(q_lat, qn_ref[...], NORM_EPS).astype(BF16)
    for p in range(q_out.shape[1] // pw):
        sl = slice(p * pw, (p + 1) * pw)
        q_out[:, sl] = rope(_dot(qn, wqu_ref[:, sl])).astype(q_out.dtype)
    kvn = _rms(kv_lat, kvn_ref[...], NORM_EPS).astype(BF16)
    for p in range(kv_out.shape[1] // pw):
        sl = slice(p * pw, (p + 1) * pw)
        kv_out[:, sl] = _dot(kvn, wkvu_ref[:, sl]).astype(kv_out.dtype)


def _mla_layout():
    qcols = -np.ones(MLA_HEADS * LANES, np.int64)
    kvcols = np.zeros(MLA_HEADS * LANES, np.int64)
    for h in range(MLA_HEADS):
        qsrc = h * (HEAD_DIM + MLA_ROPE)
        ksrc = h * 2 * HEAD_DIM
        base = h * LANES
        if h % 2 == 0:
            qcols[base:base + MLA_ROPE] = qsrc + HEAD_DIM + np.arange(MLA_ROPE)
            qcols[base + HEAD_DIM:base + LANES] = qsrc + np.arange(HEAD_DIM)
            kvcols[base:base + HEAD_DIM] = ksrc + HEAD_DIM + np.arange(HEAD_DIM)
            kvcols[base + HEAD_DIM:base + LANES] = ksrc + np.arange(HEAD_DIM)
        else:
            qcols[base:base + HEAD_DIM] = qsrc + np.arange(HEAD_DIM)
            qcols[base + HEAD_DIM:base + HEAD_DIM + MLA_ROPE] = qsrc + HEAD_DIM + np.arange(MLA_ROPE)
            kvcols[base:base + HEAD_DIM] = ksrc + np.arange(HEAD_DIM)
            kvcols[base + HEAD_DIM:base + LANES] = ksrc + HEAD_DIM + np.arange(HEAD_DIM)
    pe_offsets = (0, LANES + HEAD_DIM)
    return qcols, kvcols, pe_offsets


def _rope_tables(s, pe_offsets):
    half = MLA_ROPE // 2
    inv_freq = np.float32(ROPE_THETA) ** (-np.arange(half, dtype=np.float32) / np.float32(half))
    ang = np.arange(s, dtype=np.float32)[:, None] * inv_freq[None, :].astype(np.float32)
    cos, sin = np.cos(ang), np.sin(ang)
    c = np.ones((s, 2 * LANES), np.float32)
    sm = np.zeros((s, 2 * LANES), np.float32)
    sp = np.zeros((s, 2 * LANES), np.float32)
    for off in pe_offsets:
        c[:, off:off + half] = cos
        c[:, off + half:off + 2 * half] = cos
        sm[:, off:off + half] = -sin
        sp[:, off + half:off + 2 * half] = sin
    return jnp.asarray(c), jnp.asarray(sm), jnp.asarray(sp)


def c_proj(x2, s, g, w_in, q_norm, w_q_up, kv_norm, w_kv_up, *, tm=512):
    n, d = x2.shape
    qcols, kvcols, pe_offsets = _mla_layout()
    wq = w_in[:, :MLA_Q_RANK].astype(BF16)
    wkv = w_in[:, MLA_Q_RANK:MLA_Q_RANK + MLA_KV_RANK].astype(BF16)
    w_pe = w_in[:, MLA_Q_RANK + MLA_KV_RANK:]
    wpe = jnp.zeros((d, 2 * LANES), F32)
    for off in pe_offsets:
        wpe = wpe.at[:, off:off + MLA_ROPE].set(w_pe)
    wpe = wpe.astype(BF16)
    q_scale = (HEAD_DIM + MLA_ROPE) ** -0.5 * LOG2E
    wqu = jnp.where(jnp.asarray(qcols >= 0)[None, :],
                    w_q_up[:, np.maximum(qcols, 0)] * q_scale, 0.0).astype(BF16)
    wkvu = w_kv_up[:, kvcols].astype(BF16)
    cos, sm, sp = _rope_tables(s, pe_offsets)
    width = MLA_HEADS * LANES
    assert s % tm == 0
    nsb = s // tm
    full = lambda shape: pl.BlockSpec(shape, lambda i: (0, 0))
    tab = pl.BlockSpec((tm, 2 * LANES), lambda i: (i % nsb, 0))
    return pl.pallas_call(
        _c_proj_kernel,
        grid=(n // tm,),
        in_specs=[pl.BlockSpec((tm, d), lambda i: (i, 0)), full((1, d)),
                  full((d, MLA_Q_RANK)), full((d, MLA_KV_RANK)), full((d, 2 * LANES)),
                  full((1, MLA_Q_RANK)), full((1, MLA_KV_RANK)),
                  full((MLA_Q_RANK, width)), full((MLA_KV_RANK, width)), tab, tab, tab],
        out_specs=[pl.BlockSpec((tm, width), lambda i: (i, 0)),
                   pl.BlockSpec((tm, width), lambda i: (i, 0)),
                   pl.BlockSpec((tm, 2 * LANES), lambda i: (i, 0))],
        out_shape=[jax.ShapeDtypeStruct((n, width), BF16),
                   jax.ShapeDtypeStruct((n, width), BF16),
                   jax.ShapeDtypeStruct((n, 2 * LANES), BF16)],
        compiler_params=_cparams(1),
        name="c_proj",
    )(x2, g.reshape(1, d), wq, wkv, wpe, q_norm.reshape(1, -1), kv_norm.reshape(1, -1),
      wqu, wkvu, cos, sm, sp)


def _attn_c_kernel(q_ref, kv_ref, pe_ref, o_ref, kcat_ref, vcat_ref, s_ref, m_ref, acc_ref, *, t):
    qi = pl.program_id(2)
    pw = 2 * LANES

    @pl.when(qi == 0)
    def _():
        lane = lax.broadcasted_iota(I32, kcat_ref.shape, 1)
        is_v = jnp.logical_or(lane < HEAD_DIM, lane >= pw - HEAD_DIM)
        kv = kv_ref[0]
        kcat_ref[...] = jnp.where(is_v, pe_ref[0], kv)
        vcat_ref[...] = jnp.where(is_v, kv, jnp.ones_like(kv))

    _init_softmax_state(m_ref, acc_ref)
    q = q_ref[0]
    first = lax.broadcasted_iota(I32, q.shape, 1) < LANES
    zero = jnp.zeros_like(q)
    q_all = jnp.concatenate([jnp.where(first, q, zero), jnp.where(first, zero, q)], axis=0)

    def rows(kb):
        return pl.ds(pl.multiple_of(kb * t, t), t)

    def scores(kb, c):
        return _dot_nt(q_all, kcat_ref[rows(kb), :])

    def update(s, kb, c):
        _softmax_update(s, vcat_ref[rows(kb), :], m_ref, acc_ref, c)

    _causal_sweep(qi, t, 1, scores, update, s_ref)
    lo = lax.broadcasted_iota(I32, (t, LANES), 1) < HEAD_DIM
    top = acc_ref[0, :t, :LANES]
    bot = acc_ref[0, t:, LANES:]
    o_ref[0] = jnp.where(lo, top / pltpu.roll(top, HEAD_DIM, 1),
                         bot / pltpu.roll(bot, HEAD_DIM, 1)).astype(o_ref.dtype)


def attn_c(q3, kv3, pe3, *, t=512):
    b, s, width = q3.shape
    npair = width // (2 * LANES)
    return pl.pallas_call(
        functools.partial(_attn_c_kernel, t=t),
        grid=(b, npair, s // t),
        in_specs=[pl.BlockSpec((1, t, 2 * LANES), lambda bi, p, qi: (bi, qi, p)),
                  pl.BlockSpec((1, s, 2 * LANES), lambda bi, p, qi: (bi, 0, p)),
                  pl.BlockSpec((1, s, 2 * LANES), lambda bi, p, qi: (bi, 0, 0))],
        out_specs=pl.BlockSpec((1, t, LANES), lambda bi, p, qi: (bi, qi, p)),
        out_shape=jax.ShapeDtypeStruct((b, s, npair * LANES), BF16),
        scratch_shapes=[pltpu.VMEM((s, 2 * LANES), BF16)] * 2 + [pltpu.VMEM((2, 1, 2 * t, t), F32),
                        pltpu.VMEM((1, 2 * t, LANES), F32), pltpu.VMEM((1, 2 * t, 2 * LANES), F32)],
        compiler_params=_cparams(3),
        name="attn_c",
    )(q3, kv3, pe3)


def _attn_d_kernel(q_ref, k_ref, v_ref, o_ref, r_ref, acc_ref, *, t, scale):
    qi = pl.program_id(2)
    q = q_ref[0] * scale
    hw = q.shape[1]
    nh = hw // HEAD_DIM
    head_of_lane = lax.broadcasted_iota(I32, (t, hw), 1) // HEAD_DIM
    zero = jnp.zeros_like(q)
    q_all = jnp.concatenate([jnp.where(head_of_lane == h, q, zero) for h in range(nh)], axis=0)
    r_ref[...] = jnp.zeros(r_ref.shape, F32)
    acc_ref[...] = jnp.zeros(acc_ref.shape, F32)

    def row_in_block(n_rows):
        row = lax.broadcasted_iota(I32, (n_rows, t), 0)
        for _ in range(n_rows // t - 1):
            row = jnp.where(row >= t, row - t, row)
        return row

    ones_ge = jnp.where(row_in_block(2 * t) >= lax.broadcasted_iota(I32, (2 * t, t), 1),
                        1.0, 0.0).astype(BF16)
    strict = row_in_block(nh * t) > lax.broadcasted_iota(I32, (nh * t, t), 1)

    def block(kb, masked):
        off = pl.multiple_of(kb * t, t)
        z = _dot_nt(q_all, k_ref[0, pl.ds(off, t), :])
        lg = jnp.log(1.0 + jnp.exp2(jnp.abs(z) * (-LOG2E)))
        ls = jnp.minimum(z, 0.0) - lg
        lk = ls - z
        if masked:
            lk = jnp.where(strict, lk, 0.0)
        hi = lk.astype(BF16)
        lw = (lk - hi.astype(F32)).astype(BF16)
        csum = _dot(jnp.concatenate([hi, lw], axis=1), ones_ge)
        r_prev = r_ref[...]
        a = jnp.exp(ls + (_lane_tile(r_prev, t // LANES) + (csum - lk)))
        if masked:
            a = jnp.where(strict, a, 0.0)
        acc_ref[...] = acc_ref[...] + _dot(a.astype(BF16), v_ref[0, pl.ds(off, t), :])
        r_ref[...] = r_prev + csum[:, 0:1]

    def live():
        return jnp.max(r_ref[...]) >= EXP_UNDERFLOW

    def cond(carry):
        it, alive = carry
        return jnp.logical_and(it < qi, alive)

    def body(carry):
        it, _ = carry
        block(qi - 1 - it, False)
        return it + 1, live()

    @pl.when(qi == 0)
    def _():
        block(qi, True)

    @pl.when(qi > 0)
    def _():
        block(qi, True)
        block(qi - 1, False)
        lax.while_loop(cond, body, (jnp.int32(1), live()))

    o = acc_ref[:t]
    for h in range(1, nh):
        o = jnp.where(head_of_lane == h, acc_ref[h * t:(h + 1) * t], o)
    o_ref[0] = o.astype(o_ref.dtype)


def attn_d(qkv3, *, t=256):
    b, s, width = qkv3.shape
    hw = MXU_DIM
    ngrp = width // (3 * hw)
    nh = hw // HEAD_DIM
    return pl.pallas_call(
        functools.partial(_attn_d_kernel, t=t, scale=HEAD_DIM ** -0.5),
        grid=(b, ngrp, s // t),
        in_specs=[pl.BlockSpec((1, t, hw), lambda bi, p, qi: (bi, qi, p)),
                  pl.BlockSpec((1, s, hw), lambda bi, p, qi: (bi, 0, ngrp + p)),
                  pl.BlockSpec((1, s, hw), lambda bi, p, qi: (bi, 0, 2 * ngrp + p))],
        out_specs=pl.BlockSpec((1, t, hw), lambda bi, p, qi: (bi, qi, p)),
        out_shape=jax.ShapeDtypeStruct((b, s, ngrp * hw), BF16),
        scratch_shapes=[pltpu.VMEM((nh * t, LANES), F32), pltpu.VMEM((nh * t, hw), F32)],
        compiler_params=_cparams(3),
        name="attn_d",
    )(qkv3, qkv3, qkv3)


def _silu(x):
    return x / (1.0 + jnp.exp(-x))


def _ffn_kernel(x_ref, g_ref, wg_ref, wu_ref, wd_ref, out_ref, *, ff_chunk):
    x = x_ref[...]
    xn = _rms(x, g_ref[...], NORM_EPS).astype(BF16)
    y = x
    for c in range(0, wg_ref.shape[1], ff_chunk):
        h = _silu(_dot(xn, wg_ref[:, c:c + ff_chunk])) * _dot(xn, wu_ref[:, c:c + ff_chunk])
        y = y + _dot(h.astype(BF16), wd_ref[c:c + ff_chunk, :])
    out_ref[...] = y


def ffn(x2, g, wg, wu, wd, *, tm=512):
    n, d = x2.shape
    ff = wg.shape[1]
    ff_chunk = ff // 2 if (ff // 2) % LANES == 0 else ff
    const = lambda shape: pl.BlockSpec(shape, lambda i: (0, 0), pipeline_mode=pl.Buffered(1))
    return pl.pallas_call(
        functools.partial(_ffn_kernel, ff_chunk=ff_chunk),
        grid=(n // tm,),
        in_specs=[pl.BlockSpec((tm, d), lambda i: (i, 0)),
                  pl.BlockSpec((1, d), lambda i: (0, 0)),
                  const((d, ff)), const((d, ff)), const((ff, d))],
        out_specs=pl.BlockSpec((tm, d), lambda i: (i, 0)),
        out_shape=jax.ShapeDtypeStruct((n, d), F32),
        compiler_params=_cparams(1),
        name="ffn",
    )(x2, g.reshape(1, d), wg, wu, wd)


def _router_kernel(x_ref, g_ref, wr_ref, meta_ref, gate_ref, cnt_ref, run_ref):
    i = pl.program_id(0)
    tm = x_ref.shape[0]

    @pl.when(i == 0)
    def _():
        run_ref[...] = jnp.zeros(run_ref.shape, F32)

    xn = _rms(x_ref[...], g_ref[...], NORM_EPS)
    xh = xn.astype(BF16)
    xl = (xn - xh.astype(F32)).astype(BF16)
    logits = _dot(xh, wr_ref[0]) + _dot(xl, wr_ref[0]) + _dot(xh, wr_ref[1])
    lane = lax.broadcasted_iota(I32, (tm, LANES), 1)
    lg = jnp.where(lane < N_EXPERTS, logits, NEG_INF)
    m1 = jnp.max(lg, axis=1, keepdims=True)
    e1 = jnp.min(jnp.where(lg == m1, lane, LANES), axis=1, keepdims=True)
    lg2 = jnp.where(lane == e1, NEG_INF, lg)
    m2 = jnp.max(lg2, axis=1, keepdims=True)
    e2 = jnp.min(jnp.where(lg2 == m2, lane, LANES), axis=1, keepdims=True)
    ex = jnp.exp(m2 - m1)
    g1 = 1.0 / (1.0 + ex)
    g2 = ex / (1.0 + ex)
    hit1 = lane == e1
    hit2 = lane == e2
    onehot = jnp.where(jnp.logical_or(hit1, hit2), 1.0, 0.0)
    earlier = (lax.broadcasted_iota(I32, (tm, tm), 0) > lax.broadcasted_iota(I32, (tm, tm), 1))
    before = _dot(jnp.where(earlier, 1.0, 0.0).astype(BF16), onehot.astype(BF16)) + run_ref[0:1, :]
    pos1 = jnp.sum(jnp.where(hit1, before, 0.0), axis=1, keepdims=True).astype(I32)
    pos2 = jnp.sum(jnp.where(hit2, before, 0.0), axis=1, keepdims=True).astype(I32)
    run_ref[...] = run_ref[...] + jnp.sum(onehot, axis=0, keepdims=True)
    meta_ref[...] = jnp.where(lane == 0, e1, jnp.where(lane == 1, e2, jnp.where(
        lane == 2, pos1, jnp.where(lane == 3, pos2, 0))))
    gate_ref[...] = jnp.where(lane == 0, g1, jnp.where(lane == 1, g2, 0.0))
    cnt_ref[...] = run_ref[...]


def router(x2, g, w_router, *, tm=512):
    n, d = x2.shape
    wr = jnp.zeros((d, LANES), F32).at[:, :N_EXPERTS].set(w_router)
    wr_hi = wr.astype(BF16)
    wr = jnp.stack([wr_hi, (wr - wr_hi.astype(F32)).astype(BF16)])
    row = lambda w: pl.BlockSpec((tm, w), lambda i: (i, 0))
    return pl.pallas_call(
        _router_kernel,
        grid=(n // tm,),
        in_specs=[row(d), pl.BlockSpec((1, d), lambda i: (0, 0)),
                  pl.BlockSpec((2, d, LANES), lambda i: (0, 0, 0))],
        out_specs=[row(LANES), row(LANES), pl.BlockSpec((8, LANES), lambda i: (0, 0))],
        out_shape=[jax.ShapeDtypeStruct((n, LANES), I32),
                   jax.ShapeDtypeStruct((n, LANES), F32), jax.ShapeDtypeStruct((8, LANES), F32)],
        scratch_shapes=[pltpu.VMEM((8, LANES), F32)],
        compiler_params=_cparams(1),
        name="router",
    )(x2, g.reshape(1, d), wr)


ROW_TILE = 8


def _to_row_tiles(ref, x):
    rows = x.shape[0]
    for c in range(ROW_TILE):
        ref[pl.ds(c, rows, stride=ROW_TILE), :] = x[:, c * LANES:(c + 1) * LANES]


def _from_row_tiles(ref):
    rows = ref.shape[0] // ROW_TILE
    return jnp.concatenate([ref[pl.ds(c, rows, stride=ROW_TILE), :] for c in range(ROW_TILE)], axis=1)


def _row_copy(src_ref, src_row, dst_ref, dst_row, sem):
    def tile(ref, row):
        return ref.at[pl.ds(pl.multiple_of(row * ROW_TILE, ROW_TILE), ROW_TILE)]
    return pltpu.make_async_copy(tile(src_ref, src_row), tile(dst_ref, dst_row), sem)


def _dispatch_kernel(dest_ref, pend_ref, x_ref, g_ref, xb_ref, xn_ref, sem, *, tb, n_steps):
    i = pl.program_id(0)
    slot = lax.rem(i, 2)

    def drain(s):
        for k in range(2):
            pltpu.make_async_copy(xn_ref.at[s], xb_ref.at[pl.ds(0, tb * ROW_TILE)], sem.at[s]).wait()

    @pl.when(i == 0)
    def _():
        xn_ref[0] = jnp.zeros(xn_ref.shape[1:], F32)

        def zero_block(row):
            first = pl.multiple_of(row * ROW_TILE, tb * ROW_TILE)
            return pltpu.make_async_copy(
                xn_ref.at[0], xb_ref.at[pl.ds(first, tb * ROW_TILE)], sem.at[0])

        def has_rows(e):
            return pend_ref[e] > (pend_ref[e - 1] if e > 0 else 0)

        total = pend_ref[N_EXPERTS - 1]
        n_rows = xb_ref.shape[0] // ROW_TILE
        for start in (True, False):
            for e in range(N_EXPERTS):
                @pl.when(has_rows(e))
                def _():
                    cp = zero_block(pend_ref[e] - tb)
                    cp.start() if start else cp.wait()

                @pl.when(total + e * tb < n_rows)
                def _():
                    cp = zero_block(total + e * tb)
                    cp.start() if start else cp.wait()

    @pl.when(i >= 2)
    def _():
        drain(slot)

    _to_row_tiles(xn_ref.at[slot], _rms(x_ref[...], g_ref[...], NORM_EPS))

    def issue(tt, carry):
        for k in range(2):
            _row_copy(xn_ref.at[slot], tt, xb_ref, dest_ref[2 * tt + k],
                      sem.at[slot]).start(priority=k)
        return carry

    lax.fori_loop(0, tb, issue, 0, unroll=8)

    @pl.when(i == n_steps - 1)
    def _():
        drain(slot)
        if n_steps > 1:
            drain(1 - slot)


def dispatch(dest_flat, pad_end, x2, g, n_rows):
    n, d = x2.shape
    assert d == ROW_TILE * LANES
    tb = MOE_ROWS
    return pl.pallas_call(
        functools.partial(_dispatch_kernel, tb=tb, n_steps=n // tb),
        grid=(n // tb,),
        in_specs=[pl.BlockSpec((2 * tb,), lambda i: (i,), memory_space=pltpu.SMEM),
                  pl.BlockSpec(memory_space=pltpu.SMEM),
                  pl.BlockSpec((tb, d), lambda i: (i, 0)),
                  pl.BlockSpec((1, d), lambda i: (0, 0))],
        out_specs=pl.BlockSpec(memory_space=pl.ANY),
        out_shape=jax.ShapeDtypeStruct((n_rows * ROW_TILE, LANES), F32),
        scratch_shapes=[pltpu.VMEM((2, tb * ROW_TILE, LANES), F32), pltpu.SemaphoreType.DMA((2,))],
        compiler_params=_cparams(1),
        name="moe_dispatch",
    )(dest_flat, pad_end.astype(I32), x2, g.reshape(1, d))


def _expert_kernel(be_ref, nu_ref, x_ref, wg_ref, wu_ref, wd_ref, y_ref, *, ff_chunk):
    i = pl.program_id(0)

    @pl.when(i < nu_ref[0])
    def _():
        xb = _from_row_tiles(x_ref).astype(BF16)
        y = jnp.zeros(xb.shape, F32)
        for c in range(0, wg_ref.shape[2], ff_chunk):
            h = (_silu(_dot(xb, wg_ref[0, :, c:c + ff_chunk]))
                 * _dot(xb, wu_ref[0, :, c:c + ff_chunk]))
            y = y + _dot(h.astype(BF16), wd_ref[0, c:c + ff_chunk, :])
        _to_row_tiles(y_ref, y)

    @pl.when(i >= nu_ref[0])
    def _():
        y_ref[...] = jnp.zeros(y_ref.shape, F32)


def experts(block_e, n_used, xb, wg, wu, wd, *, ff_chunk=512):
    d, ff = wg.shape[1], wg.shape[2]
    assert ff % ff_chunk == 0
    wspec = lambda shape: pl.BlockSpec(shape, lambda i, be, nu: (be[i], 0, 0),
                                       pipeline_mode=pl.Buffered(1))
    blk = pl.BlockSpec((MOE_ROWS * ROW_TILE, LANES), lambda i, be, nu: (i, 0))
    return pl.pallas_call(
        functools.partial(_expert_kernel, ff_chunk=ff_chunk),
        grid_spec=pltpu.PrefetchScalarGridSpec(
            num_scalar_prefetch=2,
            grid=(xb.shape[0] // (MOE_ROWS * ROW_TILE),),
            in_specs=[blk, wspec((1, d, ff)), wspec((1, d, ff)), wspec((1, ff, d))],
            out_specs=blk,
        ),
        out_shape=jax.ShapeDtypeStruct(xb.shape, F32),
        compiler_params=_cparams(1),
        name="moe_experts",
    )(block_e, n_used, xb, wg, wu, wd)


def _combine_kernel(dest_ref, dest_next_ref, x_ref, gate_ref, fg_ref, yb_ref, out_ref, buf_ref, sem,
                    *, tb, final, n_steps):
    i = pl.program_id(0)
    slot = lax.rem(i, 2)

    def gather(d_ref, s):
        def issue(tt, carry):
            for k in range(2):
                _row_copy(yb_ref, d_ref[2 * tt + k], buf_ref.at[s, k], tt,
                          sem.at[s]).start(priority=k)
            return carry

        lax.fori_loop(0, tb, issue, 0, unroll=8)

    @pl.when(i == 0)
    def _():
        gather(dest_ref, slot)

    @pl.when(i + 1 < n_steps)
    def _():
        gather(dest_next_ref, 1 - slot)

    for k in range(2):
        pltpu.make_async_copy(yb_ref.at[pl.ds(0, tb * ROW_TILE)], buf_ref.at[slot, k],
                              sem.at[slot]).wait()
    gate = gate_ref[...]
    y = (x_ref[...] + gate[:, 0:1] * _from_row_tiles(buf_ref.at[slot, 0])
         + gate[:, 1:2] * _from_row_tiles(buf_ref.at[slot, 1]))
    if final:
        y = _rms(y, fg_ref[...], NORM_EPS)
    out_ref[...] = y


def combine(dest_flat, x2, gates, yb, final_g, *, final, tb=512):
    n, d = x2.shape
    n_steps = n // tb
    return pl.pallas_call(
        functools.partial(_combine_kernel, tb=tb, final=final, n_steps=n_steps),
        grid=(n_steps,),
        in_specs=[pl.BlockSpec((2 * tb,), lambda i: (i,), memory_space=pltpu.SMEM),
                  pl.BlockSpec((2 * tb,), lambda i: (jnp.minimum(i + 1, n_steps - 1),),
                               memory_space=pltpu.SMEM),
                  pl.BlockSpec((tb, d), lambda i: (i, 0)),
                  pl.BlockSpec((tb, LANES), lambda i: (i, 0)),
                  pl.BlockSpec((1, d), lambda i: (0, 0)),
                  pl.BlockSpec(memory_space=pl.ANY)],
        out_specs=pl.BlockSpec((tb, d), lambda i: (i, 0)),
        out_shape=jax.ShapeDtypeStruct((n, d), F32),
        scratch_shapes=[pltpu.VMEM((2, 2, tb * ROW_TILE, LANES), F32),
                        pltpu.SemaphoreType.DMA((2,))],
        compiler_params=_cparams(1),
        name="moe_combine",
    )(dest_flat, dest_flat, x2, gates, final_g.reshape(1, d), yb)


def moe(x2, g, w_router, wg, wu, wd, final_g, *, final):
    n, d = x2.shape
    meta, gates, cnt = router(x2, g, w_router)
    counts = cnt[0, :N_EXPERTS].astype(I32)
    padded = (counts + MOE_ROWS - 1) // MOE_ROWS * MOE_ROWS
    pad_end = jnp.cumsum(padded)
    pad_start = pad_end - padded
    dest = (pad_start[meta[:, 0:2]] + meta[:, 2:4]).reshape(-1).astype(I32)
    n_blocks = -(-2 * n // MOE_ROWS) + N_EXPERTS
    starts = jnp.arange(n_blocks, dtype=I32) * MOE_ROWS
    block_e = jnp.minimum(jnp.sum((pad_end[None, :] <= starts[:, None]).astype(I32), axis=1),
                          N_EXPERTS - 1)
    n_used = (pad_end[-1:] // MOE_ROWS).astype(I32)
    xb = dispatch(dest, pad_end, x2, g, n_blocks * MOE_ROWS)
    yb = experts(block_e, n_used, xb, wg.astype(BF16), wu.astype(BF16), wd.astype(BF16))
    return combine(dest, x2, gates, yb, final_g, final=final)


def _bf16(w):
    return w.astype(BF16)


def mixer_a(x2, b, norm, w_qkv, w_o):
    d = x2.shape[1]
    n_groups = len(A_PATTERN)
    dils = tuple(dil for _, dil in A_PATTERN)
    gw = A_HEADS * HEAD_DIM
    w = w_qkv.reshape(d, 3, n_groups, gw)
    w = w * jnp.asarray([HEAD_DIM ** -0.5 * LOG2E, 1.0, 1.0], F32).reshape(1, 3, 1, 1)
    w = _bf16(w.transpose(0, 2, 1, 3).reshape(d, n_groups * 3 * gw))
    views = a_proj(x2, norm, w, dils)
    slopes = 2.0 ** (-8.0 * np.arange(1, n_groups * A_HEADS + 1) / (n_groups * A_HEADS))
    outs, lses = [], []
    for gi, (window, dil) in enumerate(A_PATTERN):
        assert window // dil == QB
        o, lse = attn_a_group(views[gi], b, dil, slopes[gi * A_HEADS:(gi + 1) * A_HEADS])
        outs.append(o)
        lses.append(lse)
    return mix_oproj_res(x2, outs, lses, _bf16(w_o), dils)


def mixer_b(x2, b, norm, w_qkv, lq1, lk1, lq2, lk2, subln, w_o, *, layer, t=512):
    n = x2.shape[0]
    n_q = w_qkv.shape[1] // 3
    w = jnp.concatenate([w_qkv[:, :n_q] * (HEAD_DIM ** -0.5 * LOG2E), w_qkv[:, n_q:]], axis=1)
    qkv = norm_proj(x2, norm, _bf16(w)).reshape(b, n // b, -1)
    o = attn_b(qkv, lq1, lk1, lq2, lk2, subln, layer, t=t)
    return oproj_res(x2, o.reshape(n, -1), _bf16(w_o))


def mixer_c(x2, b, norm, w_in, q_norm, w_q_up, kv_norm, w_kv_up, w_o, *, t=512):
    n = x2.shape[0]
    s = n // b
    q, kv, pe = c_proj(x2, s, norm, w_in, q_norm, w_q_up, kv_norm, w_kv_up)
    o = attn_c(q.reshape(b, s, -1), kv.reshape(b, s, -1), pe.reshape(b, s, -1), t=t)
    return oproj_res(x2, o.reshape(n, -1), _bf16(w_o))


def mixer_d(x2, b, norm, w_qkv, w_o, *, t=256):
    n = x2.shape[0]
    qkv = norm_proj(x2, norm, _bf16(w_qkv)).reshape(b, n // b, -1)
    o = attn_d(qkv, t=t)
    return oproj_res(x2, o.reshape(n, -1), _bf16(w_o))
def kernel(x, a_norm, a_w_qkv, a_w_o, ffn0_norm, ffn0_w_gate, ffn0_w_up, ffn0_w_down, b_norm, b_w_qkv, b_lambda_q1, b_lambda_k1, b_lambda_q2, b_lambda_k2, b_subln, b_w_o, moe1_norm, moe1_w_router, moe1_w_gate, moe1_w_up, moe1_w_down, c_norm, c_w_in, c_q_norm, c_w_q_up, c_kv_norm, c_w_kv_up, c_w_o, ffn2_norm, ffn2_w_gate, ffn2_w_up, ffn2_w_down, d_norm, d_w_qkv, d_w_o, moe3_norm, moe3_w_router, moe3_w_gate, moe3_w_up, moe3_w_down, final_norm):
    b, s, d = x.shape
    x2 = x.reshape(b * s, d)
    bf = _bf16
    x2 = mixer_a(x2, b, a_norm, a_w_qkv, a_w_o)
    x2 = ffn(x2, ffn0_norm, bf(ffn0_w_gate), bf(ffn0_w_up), bf(ffn0_w_down))
    x2 = mixer_b(x2, b, b_norm, b_w_qkv, b_lambda_q1, b_lambda_k1, b_lambda_q2, b_lambda_k2,
                 b_subln, b_w_o, layer=1)
    x2 = moe(x2, moe1_norm, moe1_w_router, moe1_w_gate, moe1_w_up, moe1_w_down, final_norm,
             final=False)
    x2 = mixer_c(x2, b, c_norm, c_w_in, c_q_norm, c_w_q_up, c_kv_norm, c_w_kv_up, c_w_o)
    x2 = ffn(x2, ffn2_norm, bf(ffn2_w_gate), bf(ffn2_w_up), bf(ffn2_w_down))
    x2 = mixer_d(x2, b, d_norm, d_w_qkv, d_w_o)
    x2 = moe(x2, moe3_norm, moe3_w_router, moe3_w_gate, moe3_w_up, moe3_w_down, final_norm,
             final=True)
    return x2.reshape(b, s, d)
```

```python
import functools
import math

import numpy as np
import jax
import jax.numpy as jnp
from jax import lax
from jax.experimental import pallas as pl
from jax.experimental.pallas import tpu as pltpu

F32 = jnp.float32
BF16 = jnp.bfloat16
I32 = jnp.int32

LANES = 128
MXU_DIM = 256
HEAD_DIM = 64
NORM_EPS = 1e-6
SUBLN_EPS = 1e-5
ROPE_THETA = 10000.0
A_PATTERN = ((128, 1), (512, 4), (2048, 16))
A_HEADS = 8
QB = 128
A_SUB = 4
N_EXPERTS = 8
MOE_ROWS = 512
SWEEP_BLOCK = 1024
VMEM_LIMIT = 56 * 1024 * 1024
NEG_INF = float("-inf")
EXP2_UNDERFLOW = -152.0


def _cparams(n_axes, vmem=VMEM_LIMIT):
    return pltpu.CompilerParams(dimension_semantics=("arbitrary",) * n_axes,
                                vmem_limit_bytes=vmem)


def _dot(a, b):
    return jnp.dot(a, b, preferred_element_type=F32)


def _dot_nt(a, b):
    return lax.dot_general(a, b, (((1,), (1,)), ((), ())), preferred_element_type=F32)


def _rms(x, g, eps):
    return x * lax.rsqrt(jnp.mean(x * x, axis=-1, keepdims=True) + eps) * g


def _lane_tile(x, n):
    return x if n == 1 else jnp.concatenate([x] * n, axis=1)


def _norm_proj_kernel(x_ref, g_ref, w_ref, o_ref, *, col_chunk):
    xn = _rms(x_ref[...], g_ref[...], NORM_EPS).astype(BF16)
    n = o_ref.shape[1]
    for c in range(0, n, col_chunk):
        o_ref[:, c:c + col_chunk] = _dot(xn, w_ref[:, c:c + col_chunk]).astype(o_ref.dtype)


def norm_proj(x2, g, w, *, tm=512, col_chunk=512):
    n, d = x2.shape
    n_out = w.shape[1]
    assert n % tm == 0 and n_out % col_chunk == 0
    return pl.pallas_call(
        functools.partial(_norm_proj_kernel, col_chunk=col_chunk),
        grid=(n // tm,),
        in_specs=[pl.BlockSpec((tm, d), lambda i: (i, 0)),
                  pl.BlockSpec((1, d), lambda i: (0, 0)),
                  pl.BlockSpec((d, n_out), lambda i: (0, 0))],
        out_specs=pl.BlockSpec((tm, n_out), lambda i: (i, 0)),
        out_shape=jax.ShapeDtypeStruct((n, n_out), BF16),
        compiler_params=_cparams(1),
        name="norm_proj",
    )(x2, g.reshape(1, d), w)


def _oproj_kernel(x_ref, o_ref, w_ref, out_ref):
    out_ref[...] = x_ref[...] + _dot(o_ref[...], w_ref[...])


def oproj_res(x2, o2, w, *, tm=1024):
    n, d = x2.shape
    k = o2.shape[1]
    return pl.pallas_call(
        _oproj_kernel,
        grid=(n // tm,),
        in_specs=[pl.BlockSpec((tm, d), lambda i: (i, 0)),
                  pl.BlockSpec((tm, k), lambda i: (i, 0)),
                  pl.BlockSpec((k, d), lambda i: (0, 0))],
        out_specs=pl.BlockSpec((tm, d), lambda i: (i, 0)),
        out_shape=jax.ShapeDtypeStruct((n, d), F32),
        compiler_params=_cparams(1),
        name="oproj_res",
    )(x2, o2, w)


LOG2E = math.log2(math.e)
LN2 = math.log(2.0)


def _a_proj_kernel(x_ref, g_ref, w_ref, *rest, dils, gcols):
    outs, scr = rest[:len(dils)], rest[len(dils)]
    xn = _rms(x_ref[...], g_ref[...], NORM_EPS).astype(BF16)
    tm = x_ref.shape[0]
    for gi, dil in enumerate(dils):
        res = _dot(xn, w_ref[:, gi * gcols:(gi + 1) * gcols])
        if dil == 1:
            outs[gi][...] = res.astype(BF16)
        else:
            for c in range(gcols // LANES):
                scr[c] = res[:, c * LANES:(c + 1) * LANES]
            for r in range(dil):
                for c in range(gcols // LANES):
                    col = r * gcols + c * LANES
                    outs[gi][:, col:col + LANES] = (
                        scr[c, pl.ds(r, tm // dil, stride=dil), :].astype(BF16))


def a_proj(x2, g, w, dils, *, tm=512):
    n, d = x2.shape
    gcols = w.shape[1] // len(dils)
    return pl.pallas_call(
        functools.partial(_a_proj_kernel, dils=dils, gcols=gcols),
        grid=(n // tm,),
        in_specs=[pl.BlockSpec((tm, d), lambda i: (i, 0)),
                  pl.BlockSpec((1, d), lambda i: (0, 0)),
                  pl.BlockSpec((d, w.shape[1]), lambda i: (0, 0))],
        out_specs=[pl.BlockSpec((tm // dil, dil * gcols), lambda i: (i, 0)) for dil in dils],
        out_shape=[jax.ShapeDtypeStruct((n // dil, dil * gcols), BF16) for dil in dils],
        scratch_shapes=[pltpu.VMEM((gcols // LANES, tm, LANES), F32)],
        compiler_params=_cparams(1),
        name="a_proj",
    )(x2, g.reshape(1, d), w)


def _attn_a_kernel(q_ref, kc_ref, kp_ref, vc_ref, vp_ref, bc_ref, bp_ref, o_ref, lse_ref):
    first_step = pl.program_id(2) == 0
    hw = MXU_DIM
    nh = hw // HEAD_DIM
    head_of_lane = lax.broadcasted_iota(I32, (QB, hw), 1) // HEAD_DIM
    for sub in range(q_ref.shape[1] // QB):
        own = slice(sub * QB, (sub + 1) * QB)
        before = slice((sub - 1) * QB, sub * QB)
        for half in range(q_ref.shape[2] // hw):
            sl = slice(half * hw, (half + 1) * hw)
            tab = slice(half * nh * QB, (half + 1) * nh * QB)
            q = q_ref[0, own, sl]
            kp, vp = ((kp_ref[0, :, sl], vp_ref[0, :, sl]) if sub == 0
                      else (kc_ref[0, before, sl], vc_ref[0, before, sl]))
            zero = jnp.zeros_like(q)
            q_all = jnp.concatenate([jnp.where(head_of_lane == h, q, zero) for h in range(nh)],
                                    axis=0)
            sc = _dot_nt(q_all, kc_ref[0, own, sl]) + bc_ref[tab, :]
            sp = _dot_nt(q_all, kp) + bp_ref[tab, :]
            if sub == 0:
                sp = jnp.where(first_step, NEG_INF, sp)
            m = jnp.maximum(jnp.max(sc, axis=1, keepdims=True), jnp.max(sp, axis=1, keepdims=True))
            pc = jnp.exp2(sc - m)
            pp = jnp.exp2(sp - m)
            l = jnp.sum(pc, axis=1, keepdims=True) + jnp.sum(pp, axis=1, keepdims=True)
            acc = _dot(pc.astype(BF16), vc_ref[0, own, sl]) + _dot(pp.astype(BF16), vp)
            lse = (m + jnp.log2(l)) * LN2
            o = jnp.zeros((QB, hw), F32)
            ls = jnp.zeros((QB, hw), F32)
            for h in range(nh):
                rows = slice(h * QB, (h + 1) * QB)
                mine = head_of_lane == h
                o = jnp.where(mine, acc[rows] / l[rows], o)
                ls = jnp.where(mine, lse[rows], ls)
            o_ref[0, own, sl] = o.astype(o_ref.dtype)
            lse_ref[0, own, sl] = ls


def _a_bias_tables(dil, slopes):
    rel_c = np.arange(QB)[:, None] - np.arange(QB)[None, :]
    rel_p = rel_c + QB
    sl = np.asarray(slopes, np.float64)[:, None, None] * dil * LOG2E
    bc = np.where(rel_c >= 0, -sl * rel_c, -np.inf).reshape(-1, QB)
    bp = np.where(rel_p <= QB, -sl * rel_p, -np.inf).reshape(-1, QB)
    return jnp.asarray(bc, F32), jnp.asarray(bp, F32)


def attn_a_group(view2, b, dil, slopes):
    gw = A_HEADS * HEAD_DIM
    l = view2.shape[0] // b
    n_sub = min(A_SUB, l // QB)
    tq = n_sub * QB
    assert l % tq == 0 and view2.shape[1] == dil * 3 * gw
    view = view2.reshape(b, l, dil * 3 * gw)
    bc, bp = _a_bias_tables(dil, slopes)

    def spec(which, prev):
        if prev:
            return pl.BlockSpec((1, QB, gw), lambda bi, r, j: (
                bi, jnp.maximum(j * n_sub - 1, 0), r * 3 + which))
        return pl.BlockSpec((1, tq, gw), lambda bi, r, j: (bi, j, r * 3 + which))

    tab = pl.BlockSpec((A_HEADS * QB, QB), lambda bi, r, j: (0, 0))
    o, lse = pl.pallas_call(
        _attn_a_kernel,
        grid=(b, dil, l // tq),
        in_specs=[spec(0, False), spec(1, False), spec(1, True), spec(2, False), spec(2, True),
                  tab, tab],
        out_specs=[pl.BlockSpec((1, tq, gw), lambda bi, r, j: (bi, j, r)),
                   pl.BlockSpec((1, tq, gw), lambda bi, r, j: (bi, j, r))],
        out_shape=[jax.ShapeDtypeStruct((b, l, dil * gw), BF16),
                   jax.ShapeDtypeStruct((b, l, dil * gw), F32)],
        compiler_params=_cparams(3),
        name=f"attn_a_d{dil}",
    )(view, view, view, view, view, bc, bp)
    return o.reshape(b * l, dil * gw), lse.reshape(b * l, dil * gw)


def _mix_oproj_kernel(x_ref, *rest, dils, gw):
    ng = len(dils)
    o_refs, l_refs = rest[:ng], rest[ng:2 * ng]
    w_ref, out_ref, o_scr, l_scr = rest[2 * ng:]
    tm = x_ref.shape[0]
    os_, ls = [], []
    for gi, dil in enumerate(dils):
        if dil == 1:
            os_.append(o_refs[gi][...].astype(F32))
            ls.append(l_refs[gi][...])
        else:
            nct = gw // LANES
            for r in range(dil):
                rows = pl.ds(r, tm // dil, stride=dil)
                for c in range(nct):
                    cols = slice(r * gw + c * LANES, r * gw + (c + 1) * LANES)
                    o_scr[gi, c, rows, :] = o_refs[gi][:, cols].astype(F32)
                    l_scr[gi, c, rows, :] = l_refs[gi][:, cols]
            os_.append(jnp.concatenate([o_scr[gi, c] for c in range(nct)], axis=1))
            ls.append(jnp.concatenate([l_scr[gi, c] for c in range(nct)], axis=1))
    m = functools.reduce(jnp.maximum, ls)
    es = [jnp.exp(v - m) for v in ls]
    den = functools.reduce(lambda a, c: a + c, es)
    mixed = functools.reduce(lambda a, c: a + c, [(e / den) * o for e, o in zip(es, os_)])
    out_ref[...] = x_ref[...] + _dot(mixed.astype(BF16), w_ref[...])


def mix_oproj_res(x2, outs, lses, w, dils, *, tm=1024):
    n, d = x2.shape
    gw = w.shape[0]
    ng = len(dils)
    views = [pl.BlockSpec((tm // dil, dil * gw), lambda i: (i, 0)) for dil in dils]
    return pl.pallas_call(
        functools.partial(_mix_oproj_kernel, dils=dils, gw=gw),
        grid=(n // tm,),
        in_specs=[pl.BlockSpec((tm, d), lambda i: (i, 0))] + views + views
                 + [pl.BlockSpec((gw, d), lambda i: (0, 0))],
        out_specs=pl.BlockSpec((tm, d), lambda i: (i, 0)),
        out_shape=jax.ShapeDtypeStruct((n, d), F32),
        scratch_shapes=[pltpu.VMEM((ng, gw // LANES, tm, LANES), F32)] * 2,
        compiler_params=_cparams(1),
        name="mix_oproj_res",
    )(x2, *outs, *lses, w)


def _softmax_update(s, v, m_ref, acc_ref, idx):
    tk = s.shape[1]
    m_prev = m_ref[idx]
    m_next = jnp.maximum(m_prev, jnp.max(s, axis=1, keepdims=True))
    p = jnp.exp2(s - _lane_tile(m_next, tk // LANES))
    alpha = jnp.exp2(m_prev - m_next)
    acc_ref[idx] = (_lane_tile(alpha, acc_ref.shape[-1] // LANES) * acc_ref[idx]
                    + _dot(p.astype(BF16), v))
    m_ref[idx] = m_next


def _init_softmax_state(m_ref, acc_ref):
    m_ref[...] = jnp.full(m_ref.shape, NEG_INF, F32)
    acc_ref[...] = jnp.zeros(acc_ref.shape, F32)


def _causal_sweep(qi, t, n_chain, scores, update, s_ref):
    def qk(kb, buf):
        for c in range(n_chain):
            s_ref[buf, c] = scores(kb, c)

    def upd(kb, buf, masked):
        for c in range(n_chain):
            s = s_ref[buf, c]
            if masked:
                s = jnp.where(_causal_mask(s.shape[0], t), s, NEG_INF)
            update(s, kb, c)

    qk(0, 0)

    def body(i, carry):
        kb = 2 * i
        qk(kb + 1, 1)
        upd(kb, 0, False)
        qk(kb + 2, 0)
        upd(kb + 1, 1, False)
        return carry

    lax.fori_loop(0, lax.shift_right_logical(qi, 1), body, 0)
    odd = jnp.bitwise_and(qi, 1) == 1

    @pl.when(odd)
    def _():
        qk(qi, 1)
        upd(qi - 1, 0, False)
        upd(qi, 1, True)

    @pl.when(jnp.logical_not(odd))
    def _():
        upd(qi, 0, True)


def _split3(x):
    hi = x.astype(BF16).astype(F32)
    mid = (x - hi).astype(BF16).astype(F32)
    lo = (x - hi - mid).astype(BF16).astype(F32)
    return hi, mid, lo


def _causal_mask(n_rows, t):
    row = lax.broadcasted_iota(I32, (n_rows, t), 0)
    for _ in range(n_rows // t - 1):
        row = jnp.where(row >= t, row - t, row)
    return row >= lax.broadcasted_iota(I32, (n_rows, t), 1)


def _attn_b_kernel(q_ref, k_ref, v_ref, lq1, lk1, lq2, lk2, g_ref, o_ref, kb_ref, vb_ref, s_ref,
                   m_ref, acc_ref, *, t, slopes, lam_init):
    h = pl.program_id(1)
    qi = pl.program_id(2)
    n_bias = 3

    @pl.when(qi == 0)
    def _():
        slope = jnp.float32(0.0)
        for hh, sv in enumerate(slopes):
            slope = jnp.where(h == hh, jnp.float32(sv), slope)
        shape = (k_ref.shape[1], LANES)
        lane = lax.broadcasted_iota(I32, shape, 1)
        parts = _split3(slope * lax.broadcasted_iota(I32, shape, 0).astype(F32))
        bias = jnp.zeros(shape, F32)
        for j, part in enumerate(parts):
            bias = jnp.where(lane == j, part, bias)
        kb_ref[:, :LANES] = k_ref[0]
        kb_ref[:, LANES:] = bias.astype(BF16)
        vb_ref[:, :LANES] = v_ref[0]
        vb_ref[:, LANES:] = jnp.ones(shape, BF16)

    q = q_ref[0].astype(F32)
    lane = lax.broadcasted_iota(I32, (t, LANES), 1)
    ones = jnp.where(lane < n_bias, 1.0, 0.0)
    q_all = jnp.concatenate(
        [jnp.concatenate([jnp.where(lane < HEAD_DIM, q, 0.0), ones], axis=1),
         jnp.concatenate([jnp.where(lane >= HEAD_DIM, q, 0.0), ones], axis=1)], axis=0).astype(BF16)
    _init_softmax_state(m_ref, acc_ref)

    def rows(kb):
        return pl.ds(pl.multiple_of(kb * t, t), t)

    def scores(kb, c):
        return _dot_nt(q_all, kb_ref[rows(kb), :])

    def update(s, kb, c):
        _softmax_update(s, vb_ref[rows(kb), :], m_ref, acc_ref, c)

    _causal_sweep(qi, t, 1, scores, update, s_ref)

    lam = (jnp.exp(jnp.sum(lq1[...] * lk1[...], axis=1, keepdims=True))
           - jnp.exp(jnp.sum(lq2[...] * lk2[...], axis=1, keepdims=True)) + lam_init)
    acc = acc_ref[0]
    o = acc[:t, :LANES] / acc[:t, LANES:] - lam * (acc[t:, :LANES] / acc[t:, LANES:])
    o = _rms(o, g_ref[...], SUBLN_EPS) * (1.0 - lam_init)
    o_ref[0] = o.astype(o_ref.dtype)


def attn_b(qkv3, lq1, lk1, lq2, lk2, subln, layer, *, t=SWEEP_BLOCK):
    b, s, width = qkv3.shape
    nh = width // (3 * LANES)
    slopes = tuple(float(2.0 ** (-8.0 * (i + 1) / nh)) * LOG2E for i in range(nh))
    lam_init = 0.8 - 0.6 * math.exp(-0.3 * layer)
    vec = pl.BlockSpec((1, HEAD_DIM), lambda bi, h, qi: (0, 0))
    return pl.pallas_call(
        functools.partial(_attn_b_kernel, t=t, slopes=slopes, lam_init=lam_init),
        grid=(b, nh, s // t),
        in_specs=[pl.BlockSpec((1, t, LANES), lambda bi, h, qi: (bi, qi, h)),
                  pl.BlockSpec((1, s, LANES), lambda bi, h, qi: (bi, 0, nh + h)),
                  pl.BlockSpec((1, s, LANES), lambda bi, h, qi: (bi, 0, 2 * nh + h)),
                  vec, vec, vec, vec,
                  pl.BlockSpec((1, LANES), lambda bi, h, qi: (0, 0))],
        out_specs=pl.BlockSpec((1, t, LANES), lambda bi, h, qi: (bi, qi, h)),
        out_shape=jax.ShapeDtypeStruct((b, s, nh * LANES), BF16),
        scratch_shapes=[pltpu.VMEM((s, 2 * LANES), BF16)] * 2 + [pltpu.VMEM((2, 1, 2 * t, t), F32),
                        pltpu.VMEM((1, 2 * t, LANES), F32), pltpu.VMEM((1, 2 * t, 2 * LANES), F32)],
        compiler_params=_cparams(3),
        name="attn_b",
    )(qkv3, qkv3, qkv3, lq1.reshape(1, -1), lk1.reshape(1, -1), lq2.reshape(1, -1),
      lk2.reshape(1, -1), subln.reshape(1, -1))


MLA_HEADS = 16
MLA_Q_RANK = 384
MLA_KV_RANK = 256
MLA_ROPE = 32


def _c_proj_kernel(x_ref, g_ref, wq_ref, wkv_ref, wpe_ref, qn_ref, kvn_ref, wqu_ref, wkvu_ref,
                   cos_ref, sm_ref, sp_ref, q_out, kv_out, pe_out):
    xn = _rms(x_ref[...], g_ref[...], NORM_EPS).astype(BF16)
    q_lat = _dot(xn, wq_ref[...])
    kv_lat = _dot(xn, wkv_ref[...])
    cos, sm, sp = cos_ref[...], sm_ref[...], sp_ref[...]
    pw = cos.shape[1]

    def rope(v):
        return v * cos + pltpu.roll(v, pw - MLA_ROPE // 2, 1) * sm + pltpu.roll(v, MLA_ROPE // 2, 1) * sp

    pe_out[...] = rope(_dot(xn, wpe_ref[...])).astype(pe_out.dtype)
    qn = _rms(q_lat, qn_ref[...], NORM_EPS).astype(BF16)
    for p in range(q_out.shape[1] // pw):
        sl = slice(p * pw, (p + 1) * pw)
        q_out[:, sl] = rope(_dot(qn, wqu_ref[:, sl])).astype(q_out.dtype)
    kvn = _rms(kv_lat, kvn_ref[...], NORM_EPS).astype(BF16)
    for p in range(kv_out.shape[1] // pw):
        sl = slice(p * pw, (p + 1) * pw)
        kv_out[:, sl] = _dot(kvn, wkvu_ref[:, sl]).astype(kv_out.dtype)


def _mla_layout():
    qcols = -np.ones(MLA_HEADS * LANES, np.int64)
    kvcols = np.zeros(MLA_HEADS * LANES, np.int64)
    for h in range(MLA_HEADS):
        qsrc = h * (HEAD_DIM + MLA_ROPE)
        ksrc = h * 2 * HEAD_DIM
        base = h * LANES
        if h % 2 == 0:
            qcols[base:base + MLA_ROPE] = qsrc + HEAD_DIM + np.arange(MLA_ROPE)
            qcols[base + HEAD_DIM:base + LANES] = qsrc + np.arange(HEAD_DIM)
            kvcols[base:base + HEAD_DIM] = ksrc + HEAD_DIM + np.arange(HEAD_DIM)
            kvcols[base + HEAD_DIM:base + LANES] = ksrc + np.arange(HEAD_DIM)
        else:
            qcols[base:base + HEAD_DIM] = qsrc + np.arange(HEAD_DIM)
            qcols[base + HEAD_DIM:base + HEAD_DIM + MLA_ROPE] = qsrc + HEAD_DIM + np.arange(MLA_ROPE)
            kvcols[base:base + HEAD_DIM] = ksrc + np.arange(HEAD_DIM)
            kvcols[base + HEAD_DIM:base + LANES] = ksrc + HEAD_DIM + np.arange(HEAD_DIM)
    pe_offsets = (0, LANES + HEAD_DIM)
    return qcols, kvcols, pe_offsets


def _rope_tables(s, pe_offsets):
    half = MLA_ROPE // 2
    inv_freq = ROPE_THETA ** (-jnp.arange(half, dtype=F32) / half)
    ang = jnp.arange(s, dtype=F32)[:, None] * inv_freq[None, :]
    cos, sin = jnp.cos(ang), jnp.sin(ang)
    one = jnp.ones((s, 1), F32)
    zero = jnp.zeros((s, 1), F32)

    def table(first, second, fill):
        cols, at = [], 0
        for off in sorted(pe_offsets):
            cols += [jnp.broadcast_to(fill, (s, off - at)), first, second]
            at = off + 2 * half
        cols.append(jnp.broadcast_to(fill, (s, 2 * LANES - at)))
        return jnp.concatenate(cols, axis=1)

    return table(cos, cos, one), table(-sin, jnp.zeros_like(sin), zero), table(
        jnp.zeros_like(sin), sin, zero)


def c_proj(x2, s, g, w_in, q_norm, w_q_up, kv_norm, w_kv_up, *, tm=512):
    n, d = x2.shape
    qcols, kvcols, pe_offsets = _mla_layout()
    wq = w_in[:, :MLA_Q_RANK].astype(BF16)
    wkv = w_in[:, MLA_Q_RANK:MLA_Q_RANK + MLA_KV_RANK].astype(BF16)
    w_pe = w_in[:, MLA_Q_RANK + MLA_KV_RANK:]
    wpe = jnp.zeros((d, 2 * LANES), F32)
    for off in pe_offsets:
        wpe = wpe.at[:, off:off + MLA_ROPE].set(w_pe)
    wpe = wpe.astype(BF16)
    q_scale = (HEAD_DIM + MLA_ROPE) ** -0.5 * LOG2E
    wqu = jnp.where(jnp.asarray(qcols >= 0)[None, :],
                    w_q_up[:, np.maximum(qcols, 0)] * q_scale, 0.0).astype(BF16)
    wkvu = w_kv_up[:, kvcols].astype(BF16)
    cos, sm, sp = _rope_tables(s, pe_offsets)
    width = MLA_HEADS * LANES
    assert s % tm == 0
    nsb = s // tm
    full = lambda shape: pl.BlockSpec(shape, lambda i: (0, 0))
    tab = pl.BlockSpec((tm, 2 * LANES), lambda i: (i % nsb, 0))
    return pl.pallas_call(
        _c_proj_kernel,
        grid=(n // tm,),
        in_specs=[pl.BlockSpec((tm, d), lambda i: (i, 0)), full((1, d)),
                  full((d, MLA_Q_RANK)), full((d, MLA_KV_RANK)), full((d, 2 * LANES)),
                  full((1, MLA_Q_RANK)), full((1, MLA_KV_RANK)),
                  full((MLA_Q_RANK, width)), full((MLA_KV_RANK, width)), tab, tab, tab],
        out_specs=[pl.BlockSpec((tm, width), lambda i: (i, 0)),
                   pl.BlockSpec((tm, width), lambda i: (i, 0)),
                   pl.BlockSpec((tm, 2 * LANES), lambda i: (i, 0))],
        out_shape=[jax.ShapeDtypeStruct((n, width), BF16),
                   jax.ShapeDtypeStruct((n, width), BF16),
                   jax.ShapeDtypeStruct((n, 2 * LANES), BF16)],
        compiler_params=_cparams(1),
        name="c_proj",
    )(x2, g.reshape(1, d), wq, wkv, wpe, q_norm.reshape(1, -1), kv_norm.reshape(1, -1),
      wqu, wkvu, cos, sm, sp)


def _attn_c_kernel(q_ref, kv_ref, pe_ref, o_ref, kcat_ref, vcat_ref, s_ref, m_ref, acc_ref, *, t):
    qi = pl.program_id(2)
    pw = 2 * LANES

    @pl.when(qi == 0)
    def _():
        lane = lax.broadcasted_iota(I32, kcat_ref.shape, 1)
        is_v = jnp.logical_or(lane < HEAD_DIM, lane >= pw - HEAD_DIM)
        kv = kv_ref[0]
        kcat_ref[...] = jnp.where(is_v, pe_ref[0], kv)
        vcat_ref[...] = jnp.where(is_v, kv, jnp.ones_like(kv))

    _init_softmax_state(m_ref, acc_ref)
    q = q_ref[0]
    first = lax.broadcasted_iota(I32, q.shape, 1) < LANES
    zero = jnp.zeros_like(q)
    q_all = jnp.concatenate([jnp.where(first, q, zero), jnp.where(first, zero, q)], axis=0)

    def rows(kb):
        return pl.ds(pl.multiple_of(kb * t, t), t)

    def scores(kb, c):
        return _dot_nt(q_all, kcat_ref[rows(kb), :])

    def update(s, kb, c):
        _softmax_update(s, vcat_ref[rows(kb), :], m_ref, acc_ref, c)

    _causal_sweep(qi, t, 1, scores, update, s_ref)
    lo = lax.broadcasted_iota(I32, (t, LANES), 1) < HEAD_DIM
    top = acc_ref[0, :t, :LANES]
    bot = acc_ref[0, t:, LANES:]
    o_ref[0] = jnp.where(lo, top / pltpu.roll(top, HEAD_DIM, 1),
                         bot / pltpu.roll(bot, HEAD_DIM, 1)).astype(o_ref.dtype)


def attn_c(q3, kv3, pe3, *, t=SWEEP_BLOCK):
    b, s, width = q3.shape
    npair = width // (2 * LANES)
    return pl.pallas_call(
        functools.partial(_attn_c_kernel, t=t),
        grid=(b, npair, s // t),
        in_specs=[pl.BlockSpec((1, t, 2 * LANES), lambda bi, p, qi: (bi, qi, p)),
                  pl.BlockSpec((1, s, 2 * LANES), lambda bi, p, qi: (bi, 0, p)),
                  pl.BlockSpec((1, s, 2 * LANES), lambda bi, p, qi: (bi, 0, 0))],
        out_specs=pl.BlockSpec((1, t, LANES), lambda bi, p, qi: (bi, qi, p)),
        out_shape=jax.ShapeDtypeStruct((b, s, npair * LANES), BF16),
        scratch_shapes=[pltpu.VMEM((s, 2 * LANES), BF16)] * 2 + [pltpu.VMEM((2, 1, 2 * t, t), F32),
                        pltpu.VMEM((1, 2 * t, LANES), F32), pltpu.VMEM((1, 2 * t, 2 * LANES), F32)],
        compiler_params=_cparams(3),
        name="attn_c",
    )(q3, kv3, pe3)


def _attn_d_kernel(q_ref, k_ref, v_ref, o_ref, r_ref, acc_ref, *, t):
    qi = pl.program_id(2)
    q = q_ref[0]
    hw = q.shape[1]
    nh = hw // HEAD_DIM
    head_of_lane = lax.broadcasted_iota(I32, (t, hw), 1) // HEAD_DIM
    zero = jnp.zeros_like(q)
    q_all = jnp.concatenate([jnp.where(head_of_lane == h, q, zero) for h in range(nh)], axis=0)
    r_ref[...] = jnp.zeros(r_ref.shape, F32)
    acc_ref[...] = jnp.zeros(acc_ref.shape, F32)

    def row_in_block(n_rows):
        row = lax.broadcasted_iota(I32, (n_rows, t), 0)
        for _ in range(n_rows // t - 1):
            row = jnp.where(row >= t, row - t, row)
        return row

    ones_ge = jnp.where(row_in_block(2 * t) >= lax.broadcasted_iota(I32, (2 * t, t), 1),
                        1.0, 0.0).astype(BF16)
    strict = row_in_block(nh * t) > lax.broadcasted_iota(I32, (nh * t, t), 1)

    def block(kb, masked):
        off = pl.multiple_of(kb * t, t)
        z = _dot_nt(q_all, k_ref[0, pl.ds(off, t), :])
        lg = jnp.log2(1.0 + jnp.exp2(-jnp.abs(z)))
        ls = jnp.minimum(z, 0.0) - lg
        lk = ls - z
        if masked:
            lk = jnp.where(strict, lk, 0.0)
        hi = lk.astype(BF16)
        lw = (lk - hi.astype(F32)).astype(BF16)
        csum = _dot(jnp.concatenate([hi, lw], axis=1), ones_ge)
        r_prev = r_ref[...]
        a = jnp.exp2(ls + (_lane_tile(r_prev, t // LANES) + (csum - lk)))
        if masked:
            a = jnp.where(strict, a, 0.0)
        acc_ref[...] = acc_ref[...] + _dot(a.astype(BF16), v_ref[0, pl.ds(off, t), :])
        r_ref[...] = r_prev + csum[:, 0:1]

    def live():
        return jnp.max(r_ref[...]) >= EXP2_UNDERFLOW

    def cond(carry):
        it, alive = carry
        return jnp.logical_and(it < qi, alive)

    def body(carry):
        it, _ = carry
        block(qi - 1 - it, False)
        return it + 1, live()

    @pl.when(qi == 0)
    def _():
        block(qi, True)

    @pl.when(qi > 0)
    def _():
        block(qi, True)
        block(qi - 1, False)
        lax.while_loop(cond, body, (jnp.int32(1), live()))

    o = acc_ref[:t]
    for h in range(1, nh):
        o = jnp.where(head_of_lane == h, acc_ref[h * t:(h + 1) * t], o)
    o_ref[0] = o.astype(o_ref.dtype)


def attn_d(qkv3, *, t=256):
    b, s, width = qkv3.shape
    hw = MXU_DIM
    ngrp = width // (3 * hw)
    nh = hw // HEAD_DIM
    return pl.pallas_call(
        functools.partial(_attn_d_kernel, t=t),
        grid=(b, ngrp, s // t),
        in_specs=[pl.BlockSpec((1, t, hw), lambda bi, p, qi: (bi, qi, p)),
                  pl.BlockSpec((1, s, hw), lambda bi, p, qi: (bi, 0, ngrp + p)),
                  pl.BlockSpec((1, s, hw), lambda bi, p, qi: (bi, 0, 2 * ngrp + p))],
        out_specs=pl.BlockSpec((1, t, hw), lambda bi, p, qi: (bi, qi, p)),
        out_shape=jax.ShapeDtypeStruct((b, s, ngrp * hw), BF16),
        scratch_shapes=[pltpu.VMEM((nh * t, LANES), F32), pltpu.VMEM((nh * t, hw), F32)],
        compiler_params=_cparams(3),
        name="attn_d",
    )(qkv3, qkv3, qkv3)


def _silu(x):
    return x / (1.0 + jnp.exp(-x))


def _ffn_kernel(x_ref, g_ref, wg_ref, wu_ref, wd_ref, out_ref, *, ff_chunk):
    x = x_ref[...]
    xn = _rms(x, g_ref[...], NORM_EPS).astype(BF16)
    y = x
    for c in range(0, wg_ref.shape[1], ff_chunk):
        h = _silu(_dot(xn, wg_ref[:, c:c + ff_chunk])) * _dot(xn, wu_ref[:, c:c + ff_chunk])
        y = y + _dot(h.astype(BF16), wd_ref[c:c + ff_chunk, :])
    out_ref[...] = y


def ffn(x2, g, wg, wu, wd, *, tm=512):
    n, d = x2.shape
    ff = wg.shape[1]
    ff_chunk = ff // 2 if (ff // 2) % LANES == 0 else ff
    const = lambda shape: pl.BlockSpec(shape, lambda i: (0, 0), pipeline_mode=pl.Buffered(1))
    return pl.pallas_call(
        functools.partial(_ffn_kernel, ff_chunk=ff_chunk),
        grid=(n // tm,),
        in_specs=[pl.BlockSpec((tm, d), lambda i: (i, 0)),
                  pl.BlockSpec((1, d), lambda i: (0, 0)),
                  const((d, ff)), const((d, ff)), const((ff, d))],
        out_specs=pl.BlockSpec((tm, d), lambda i: (i, 0)),
        out_shape=jax.ShapeDtypeStruct((n, d), F32),
        compiler_params=_cparams(1),
        name="ffn",
    )(x2, g.reshape(1, d), wg, wu, wd)


def _router_kernel(x_ref, g_ref, wr_ref, meta_ref, gate_ref, cnt_ref, run_ref):
    i = pl.program_id(0)
    tm = x_ref.shape[0]

    @pl.when(i == 0)
    def _():
        run_ref[...] = jnp.zeros(run_ref.shape, F32)

    xn = _rms(x_ref[...], g_ref[...], NORM_EPS)
    xh = xn.astype(BF16)
    xl = (xn - xh.astype(F32)).astype(BF16)
    logits = _dot(xh, wr_ref[0]) + _dot(xl, wr_ref[0]) + _dot(xh, wr_ref[1])
    lane = lax.broadcasted_iota(I32, (tm, LANES), 1)
    lg = jnp.where(lane < N_EXPERTS, logits, NEG_INF)
    m1 = jnp.max(lg, axis=1, keepdims=True)
    e1 = jnp.min(jnp.where(lg == m1, lane, LANES), axis=1, keepdims=True)
    lg2 = jnp.where(lane == e1, NEG_INF, lg)
    m2 = jnp.max(lg2, axis=1, keepdims=True)
    e2 = jnp.min(jnp.where(lg2 == m2, lane, LANES), axis=1, keepdims=True)
    ex = jnp.exp(m2 - m1)
    g1 = 1.0 / (1.0 + ex)
    g2 = ex / (1.0 + ex)
    hit1 = lane == e1
    hit2 = lane == e2
    onehot = jnp.where(jnp.logical_or(hit1, hit2), 1.0, 0.0)
    earlier = (lax.broadcasted_iota(I32, (tm, tm), 0) > lax.broadcasted_iota(I32, (tm, tm), 1))
    before = _dot(jnp.where(earlier, 1.0, 0.0).astype(BF16), onehot.astype(BF16)) + run_ref[0:1, :]
    pos1 = jnp.sum(jnp.where(hit1, before, 0.0), axis=1, keepdims=True).astype(I32)
    pos2 = jnp.sum(jnp.where(hit2, before, 0.0), axis=1, keepdims=True).astype(I32)
    run_ref[...] = run_ref[...] + jnp.sum(onehot, axis=0, keepdims=True)
    meta_ref[...] = jnp.where(lane == 0, e1, jnp.where(lane == 1, e2, jnp.where(
        lane == 2, pos1, jnp.where(lane == 3, pos2, 0))))
    gate_ref[...] = jnp.where(lane == 0, g1, jnp.where(lane == 1, g2, 0.0))
    cnt_ref[...] = run_ref[...]


def router(x2, g, w_router, *, tm=512):
    n, d = x2.shape
    wr = jnp.zeros((d, LANES), F32).at[:, :N_EXPERTS].set(w_router)
    wr_hi = wr.astype(BF16)
    wr = jnp.stack([wr_hi, (wr - wr_hi.astype(F32)).astype(BF16)])
    row = lambda w: pl.BlockSpec((tm, w), lambda i: (i, 0))
    return pl.pallas_call(
        _router_kernel,
        grid=(n // tm,),
        in_specs=[row(d), pl.BlockSpec((1, d), lambda i: (0, 0)),
                  pl.BlockSpec((2, d, LANES), lambda i: (0, 0, 0))],
        out_specs=[row(LANES), row(LANES), pl.BlockSpec((8, LANES), lambda i: (0, 0))],
        out_shape=[jax.ShapeDtypeStruct((n, LANES), I32),
                   jax.ShapeDtypeStruct((n, LANES), F32), jax.ShapeDtypeStruct((8, LANES), F32)],
        scratch_shapes=[pltpu.VMEM((8, LANES), F32)],
        compiler_params=_cparams(1),
        name="router",
    )(x2, g.reshape(1, d), wr)


ROW_TILE = 8


def _to_row_tiles(ref, x):
    rows = x.shape[0]
    for c in range(ROW_TILE):
        ref[pl.ds(c, rows, stride=ROW_TILE), :] = x[:, c * LANES:(c + 1) * LANES]


def _from_row_tiles(ref):
    rows = ref.shape[0] // ROW_TILE
    return jnp.concatenate([ref[pl.ds(c, rows, stride=ROW_TILE), :] for c in range(ROW_TILE)], axis=1)


def _row_copy(src_ref, src_row, dst_ref, dst_row, sem):
    def tile(ref, row):
        return ref.at[pl.ds(pl.multiple_of(row * ROW_TILE, ROW_TILE), ROW_TILE)]
    return pltpu.make_async_copy(tile(src_ref, src_row), tile(dst_ref, dst_row), sem)


def _dispatch_kernel(dest_ref, pend_ref, x_ref, g_ref, xb_ref, xn_ref, sem, *, tb, n_steps):
    i = pl.program_id(0)
    slot = lax.rem(i, 2)

    def drain(s):
        for k in range(2):
            pltpu.make_async_copy(xn_ref.at[s], xb_ref.at[pl.ds(0, tb * ROW_TILE)], sem.at[s]).wait()

    @pl.when(i == 0)
    def _():
        xn_ref[0] = jnp.zeros(xn_ref.shape[1:], F32)

        def zero_block(row):
            first = pl.multiple_of(row * ROW_TILE, tb * ROW_TILE)
            return pltpu.make_async_copy(
                xn_ref.at[0], xb_ref.at[pl.ds(first, tb * ROW_TILE)], sem.at[0])

        def has_rows(e):
            return pend_ref[e] > (pend_ref[e - 1] if e > 0 else 0)

        total = pend_ref[N_EXPERTS - 1]
        n_rows = xb_ref.shape[0] // ROW_TILE
        for start in (True, False):
            for e in range(N_EXPERTS):
                @pl.when(has_rows(e))
                def _():
                    cp = zero_block(pend_ref[e] - tb)
                    cp.start() if start else cp.wait()

                @pl.when(total + e * tb < n_rows)
                def _():
                    cp = zero_block(total + e * tb)
                    cp.start() if start else cp.wait()

    @pl.when(i >= 2)
    def _():
        drain(slot)

    _to_row_tiles(xn_ref.at[slot], _rms(x_ref[...], g_ref[...], NORM_EPS))

    def issue(tt, carry):
        for k in range(2):
            _row_copy(xn_ref.at[slot], tt, xb_ref, dest_ref[2 * tt + k],
                      sem.at[slot]).start(priority=k)
        return carry

    lax.fori_loop(0, tb, issue, 0, unroll=8)

    @pl.when(i == n_steps - 1)
    def _():
        drain(slot)
        if n_steps > 1:
            drain(1 - slot)


def dispatch(dest_flat, pad_end, x2, g, n_rows):
    n, d = x2.shape
    assert d == ROW_TILE * LANES
    tb = MOE_ROWS
    return pl.pallas_call(
        functools.partial(_dispatch_kernel, tb=tb, n_steps=n // tb),
        grid=(n // tb,),
        in_specs=[pl.BlockSpec((2 * tb,), lambda i: (i,), memory_space=pltpu.SMEM),
                  pl.BlockSpec(memory_space=pltpu.SMEM),
                  pl.BlockSpec((tb, d), lambda i: (i, 0)),
                  pl.BlockSpec((1, d), lambda i: (0, 0))],
        out_specs=pl.BlockSpec(memory_space=pl.ANY),
        out_shape=jax.ShapeDtypeStruct((n_rows * ROW_TILE, LANES), F32),
        scratch_shapes=[pltpu.VMEM((2, tb * ROW_TILE, LANES), F32), pltpu.SemaphoreType.DMA((2,))],
        compiler_params=_cparams(1),
        name="moe_dispatch",
    )(dest_flat, pad_end.astype(I32), x2, g.reshape(1, d))


def _expert_kernel(be_ref, nu_ref, x_ref, wg_ref, wu_ref, wd_ref, y_ref, *, ff_chunk):
    i = pl.program_id(0)

    @pl.when(i < nu_ref[0])
    def _():
        xb = _from_row_tiles(x_ref).astype(BF16)
        y = jnp.zeros(xb.shape, F32)
        for c in range(0, wg_ref.shape[2], ff_chunk):
            h = (_silu(_dot(xb, wg_ref[0, :, c:c + ff_chunk]))
                 * _dot(xb, wu_ref[0, :, c:c + ff_chunk]))
            y = y + _dot(h.astype(BF16), wd_ref[0, c:c + ff_chunk, :])
        _to_row_tiles(y_ref, y)

    @pl.when(i >= nu_ref[0])
    def _():
        y_ref[...] = jnp.zeros(y_ref.shape, F32)


def experts(block_e, n_used, xb, wg, wu, wd, *, ff_chunk=512):
    d, ff = wg.shape[1], wg.shape[2]
    assert ff % ff_chunk == 0
    wspec = lambda shape: pl.BlockSpec(shape, lambda i, be, nu: (be[i], 0, 0),
                                       pipeline_mode=pl.Buffered(1))
    blk = pl.BlockSpec((MOE_ROWS * ROW_TILE, LANES), lambda i, be, nu: (i, 0))
    return pl.pallas_call(
        functools.partial(_expert_kernel, ff_chunk=ff_chunk),
        grid_spec=pltpu.PrefetchScalarGridSpec(
            num_scalar_prefetch=2,
            grid=(xb.shape[0] // (MOE_ROWS * ROW_TILE),),
            in_specs=[blk, wspec((1, d, ff)), wspec((1, d, ff)), wspec((1, ff, d))],
            out_specs=blk,
        ),
        out_shape=jax.ShapeDtypeStruct(xb.shape, F32),
        compiler_params=_cparams(1),
        name="moe_experts",
    )(block_e, n_used, xb, wg, wu, wd)


def _combine_kernel(dest_ref, dest_next_ref, x_ref, gate_ref, fg_ref, yb_ref, out_ref, buf_ref, sem,
                    *, tb, final, n_steps):
    i = pl.program_id(0)
    slot = lax.rem(i, 2)

    def gather(d_ref, s):
        def issue(tt, carry):
            for k in range(2):
                _row_copy(yb_ref, d_ref[2 * tt + k], buf_ref.at[s, k], tt,
                          sem.at[s]).start(priority=k)
            return carry

        lax.fori_loop(0, tb, issue, 0, unroll=8)

    @pl.when(i == 0)
    def _():
        gather(dest_ref, slot)

    @pl.when(i + 1 < n_steps)
    def _():
        gather(dest_next_ref, 1 - slot)

    for k in range(2):
        pltpu.make_async_copy(yb_ref.at[pl.ds(0, tb * ROW_TILE)], buf_ref.at[slot, k],
                              sem.at[slot]).wait()
    gate = gate_ref[...]
    y = (x_ref[...] + gate[:, 0:1] * _from_row_tiles(buf_ref.at[slot, 0])
         + gate[:, 1:2] * _from_row_tiles(buf_ref.at[slot, 1]))
    if final:
        y = _rms(y, fg_ref[...], NORM_EPS)
    out_ref[...] = y


def combine(dest_flat, x2, gates, yb, final_g, *, final, tb=512):
    n, d = x2.shape
    n_steps = n // tb
    return pl.pallas_call(
        functools.partial(_combine_kernel, tb=tb, final=final, n_steps=n_steps),
        grid=(n_steps,),
        in_specs=[pl.BlockSpec((2 * tb,), lambda i: (i,), memory_space=pltpu.SMEM),
                  pl.BlockSpec((2 * tb,), lambda i: (jnp.minimum(i + 1, n_steps - 1),),
                               memory_space=pltpu.SMEM),
                  pl.BlockSpec((tb, d), lambda i: (i, 0)),
                  pl.BlockSpec((tb, LANES), lambda i: (i, 0)),
                  pl.BlockSpec((1, d), lambda i: (0, 0)),
                  pl.BlockSpec(memory_space=pl.ANY)],
        out_specs=pl.BlockSpec((tb, d), lambda i: (i, 0)),
        out_shape=jax.ShapeDtypeStruct((n, d), F32),
        scratch_shapes=[pltpu.VMEM((2, 2, tb * ROW_TILE, LANES), F32),
                        pltpu.SemaphoreType.DMA((2,))],
        compiler_params=_cparams(1),
        name="moe_combine",
    )(dest_flat, dest_flat, x2, gates, final_g.reshape(1, d), yb)


def moe(x2, g, w_router, wg, wu, wd, final_g, *, final):
    n, d = x2.shape
    meta, gates, cnt = router(x2, g, w_router)
    counts = cnt[0, :N_EXPERTS].astype(I32)
    padded = (counts + MOE_ROWS - 1) // MOE_ROWS * MOE_ROWS
    pad_end = jnp.cumsum(padded)
    pad_start = pad_end - padded
    dest = (pad_start[meta[:, 0:2]] + meta[:, 2:4]).reshape(-1).astype(I32)
    n_blocks = -(-2 * n // MOE_ROWS) + N_EXPERTS
    starts = jnp.arange(n_blocks, dtype=I32) * MOE_ROWS
    block_e = jnp.minimum(jnp.sum((pad_end[None, :] <= starts[:, None]).astype(I32), axis=1),
                          N_EXPERTS - 1)
    n_used = (pad_end[-1:] // MOE_ROWS).astype(I32)
    xb = dispatch(dest, pad_end, x2, g, n_blocks * MOE_ROWS)
    yb = experts(block_e, n_used, xb, wg.astype(BF16), wu.astype(BF16), wd.astype(BF16))
    return combine(dest, x2, gates, yb, final_g, final=final)


def _bf16(w):
    return w.astype(BF16)


def mixer_a(x2, b, norm, w_qkv, w_o):
    d = x2.shape[1]
    n_groups = len(A_PATTERN)
    dils = tuple(dil for _, dil in A_PATTERN)
    gw = A_HEADS * HEAD_DIM
    w = w_qkv.reshape(d, 3, n_groups, gw)
    w = w * jnp.asarray([HEAD_DIM ** -0.5 * LOG2E, 1.0, 1.0], F32).reshape(1, 3, 1, 1)
    w = _bf16(w.transpose(0, 2, 1, 3).reshape(d, n_groups * 3 * gw))
    views = a_proj(x2, norm, w, dils)
    slopes = 2.0 ** (-8.0 * np.arange(1, n_groups * A_HEADS + 1) / (n_groups * A_HEADS))
    outs, lses = [], []
    for gi, (window, dil) in enumerate(A_PATTERN):
        assert window // dil == QB
        o, lse = attn_a_group(views[gi], b, dil, slopes[gi * A_HEADS:(gi + 1) * A_HEADS])
        outs.append(o)
        lses.append(lse)
    return mix_oproj_res(x2, outs, lses, _bf16(w_o), dils)


def mixer_b(x2, b, norm, w_qkv, lq1, lk1, lq2, lk2, subln, w_o, *, layer, t=SWEEP_BLOCK):
    n = x2.shape[0]
    n_q = w_qkv.shape[1] // 3
    w = jnp.concatenate([w_qkv[:, :n_q] * (HEAD_DIM ** -0.5 * LOG2E), w_qkv[:, n_q:]], axis=1)
    qkv = norm_proj(x2, norm, _bf16(w)).reshape(b, n // b, -1)
    o = attn_b(qkv, lq1, lk1, lq2, lk2, subln, layer, t=t)
    return oproj_res(x2, o.reshape(n, -1), _bf16(w_o))


def mixer_c(x2, b, norm, w_in, q_norm, w_q_up, kv_norm, w_kv_up, w_o, *, t=SWEEP_BLOCK):
    n = x2.shape[0]
    s = n // b
    q, kv, pe = c_proj(x2, s, norm, w_in, q_norm, w_q_up, kv_norm, w_kv_up)
    o = attn_c(q.reshape(b, s, -1), kv.reshape(b, s, -1), pe.reshape(b, s, -1), t=t)
    return oproj_res(x2, o.reshape(n, -1), _bf16(w_o))


def mixer_d(x2, b, norm, w_qkv, w_o, *, t=256):
    n = x2.shape[0]
    n_q = w_qkv.shape[1] // 3
    w = jnp.concatenate([w_qkv[:, :n_q] * (HEAD_DIM ** -0.5 * LOG2E), w_qkv[:, n_q:]], axis=1)
    qkv = norm_proj(x2, norm, _bf16(w)).reshape(b, n // b, -1)
    o = attn_d(qkv, t=t)
    return oproj_res(x2, o.reshape(n, -1), _bf16(w_o))
def kernel(x, a_norm, a_w_qkv, a_w_o, ffn0_norm, ffn0_w_gate, ffn0_w_up, ffn0_w_down, b_norm, b_w_qkv, b_lambda_q1, b_lambda_k1, b_lambda_q2, b_lambda_k2, b_subln, b_w_o, moe1_norm, moe1_w_router, moe1_w_gate, moe1_w_up, moe1_w_down, c_norm, c_w_in, c_q_norm, c_w_q_up, c_kv_norm, c_w_kv_up, c_w_o, ffn2_norm, ffn2_w_gate, ffn2_w_up, ffn2_w_down, d_norm, d_w_qkv, d_w_o, moe3_norm, moe3_w_router, moe3_w_gate, moe3_w_up, moe3_w_down, final_norm):
    b, s, d = x.shape
    x2 = x.reshape(b * s, d)
    bf = _bf16
    x2 = mixer_a(x2, b, a_norm, a_w_qkv, a_w_o)
    x2 = ffn(x2, ffn0_norm, bf(ffn0_w_gate), bf(ffn0_w_up), bf(ffn0_w_down))
    x2 = mixer_b(x2, b, b_norm, b_w_qkv, b_lambda_q1, b_lambda_k1, b_lambda_q2, b_lambda_k2,
                 b_subln, b_w_o, layer=1)
    x2 = moe(x2, moe1_norm, moe1_w_router, moe1_w_gate, moe1_w_up, moe1_w_down, final_norm,
             final=False)
    x2 = mixer_c(x2, b, c_norm, c_w_in, c_q_norm, c_w_q_up, c_kv_norm, c_w_kv_up, c_w_o)
    x2 = ffn(x2, ffn2_norm, bf(ffn2_w_gate), bf(ffn2_w_up), bf(ffn2_w_down))
    x2 = mixer_d(x2, b, d_norm, d_w_qkv, d_w_o)
    x2 = moe(x2, moe3_norm, moe3_w_router, moe3_w_gate, moe3_w_up, moe3_w_down, final_norm,
             final=True)
    return x2.reshape(b, s, d)
```
